```python
import math
import jax
import jax.numpy as jnp
from jax import lax
import numpy as np

D_MODEL = 2048
BATCH = 32
SEQ = 256
DEPTH = 2
DEC_BATCH = 8
DEC_SEQ = 2048
PAST_LEN = 512

GRID_W = 64
EPS = 1e-6
MIX_WIDTH = D_MODEL
ATT_WIDTH = D_MODEL // 2
RWKV_WIDTH = D_MODEL // 4
S5_WIDTH = D_MODEL - ATT_WIDTH - RWKV_WIDTH
ATT_QK_DIM = 64
ATT_V_DIM = 2 * ATT_QK_DIM
ATT_HEADS = ATT_WIDTH // ATT_V_DIM
ATT_QK_WIDTH = ATT_HEADS * 2 * ATT_QK_DIM
Q_BLOCK = 128
ROPE_THETA = 10000.0
RWKV_HEAD_DIM = 64
RWKV_HEADS = RWKV_WIDTH // RWKV_HEAD_DIM
RWKV_DECAY_RANK = 64
RWKV_ICLR_RANK = 64
RWKV_GATE_RANK = 128
RWKV_GN_EPS = 64e-5
RWKV_PROJ = 3 * RWKV_WIDTH + RWKV_DECAY_RANK + RWKV_ICLR_RANK + RWKV_GATE_RANK
RWKV_SPLITS = [RWKV_WIDTH, 2 * RWKV_WIDTH, 3 * RWKV_WIDTH,
               3 * RWKV_WIDTH + RWKV_DECAY_RANK,
               3 * RWKV_WIDTH + RWKV_DECAY_RANK + RWKV_ICLR_RANK]
S5_GROUP = 16
S5_GROUPS = S5_WIDTH // S5_GROUP
S5_STATE = 64
PROJ_WIDTH = 2 * ATT_QK_WIDTH + ATT_WIDTH + RWKV_PROJ + S5_WIDTH
PROJ_SPLITS = [ATT_QK_WIDTH, 2 * ATT_QK_WIDTH, 2 * ATT_QK_WIDTH + ATT_WIDTH,
               2 * ATT_QK_WIDTH + ATT_WIDTH + RWKV_PROJ]
N_EXPERTS = 64
TOP_K = 8
EXPERT_DIM = 512
SHARED_DIM = 512
ROUTE_SCALE = 2.5
MOE_BLOCK = 128

kernel_name = 'hybrid_diffusion_prefix_trunk_step'


def rms_norm(x, g, eps=EPS):
    xf = x.astype(jnp.float32)
    y = xf * lax.rsqrt(jnp.mean(xf * xf, axis=-1, keepdims=True) + eps)
    return (y * g.astype(jnp.float32)).astype(x.dtype)


def modulation(cvec, w_mod, b_mod):
    m = jax.nn.silu(cvec) @ w_mod + b_mod
    return jnp.split(m[:, None, :], 6, axis=-1)


def axial_rope(n_tok):
    rows = n_tok // GRID_W
    row = jnp.repeat(jnp.arange(rows), GRID_W).astype(jnp.float32)
    col = jnp.tile(jnp.arange(GRID_W), rows).astype(jnp.float32)
    half = ATT_QK_DIM // 2
    inv = ROPE_THETA ** (-jnp.arange(0, half, 2, dtype=jnp.float32) / half)
    ang = jnp.concatenate([row[:, None] * inv, col[:, None] * inv], axis=-1)
    return jnp.cos(ang), jnp.sin(ang)


def apply_rope(x, cos, sin):
    xr = x.astype(jnp.float32).reshape(*x.shape[:-1], ATT_QK_DIM // 2, 2)
    c = cos[None, :, None, None, :]
    s = sin[None, :, None, None, :]
    x1, x2 = xr[..., 0], xr[..., 1]
    out = jnp.stack([x1 * c - x2 * s, x1 * s + x2 * c], axis=-1).reshape(x.shape)
    return out.astype(x.dtype)


def diff_attention(q, k, v, lam, lam_init, subln_g):
    bsz, lq = q.shape[0], q.shape[1]
    nb = lq // Q_BLOCK
    q_blocks = jnp.moveaxis(q.reshape(bsz, nb, Q_BLOCK, ATT_HEADS, 2, ATT_QK_DIM), 1, 0)
    scale = ATT_QK_DIM ** -0.5

    def one_block(qb):
        s = jnp.einsum('bqhmd,bkhmd->bhmqk', qb, k, preferred_element_type=jnp.float32) * scale
        p = jax.nn.softmax(s, axis=-1)
        pd = p[:, :, 0] - lam * p[:, :, 1]
        return jnp.einsum('bhqk,bkhe->bqhe', pd.astype(v.dtype), v)

    o = lax.map(one_block, q_blocks)
    o = jnp.moveaxis(o, 0, 1).reshape(bsz, lq, ATT_HEADS, ATT_V_DIM)
    o = rms_norm(o, subln_g) * (1.0 - lam_init)
    return o.reshape(bsz, lq, ATT_WIDTH)


def token_shift_bidir(z, mu):
    zp = jnp.pad(z, ((0, 0), (1, 1), (0, 0)))
    return z + mu * (0.5 * (zp[:, :-2] + zp[:, 2:]) - z)


def wkv7_scan(r, w, k, v, kk, a, s0, reverse):
    xs = tuple(jnp.moveaxis(t, 1, 0) for t in (r, w, k, v, kk, a))

    def step(S, inp):
        r_t, w_t, k_t, v_t, kk_t, a_t = inp
        sa = jnp.einsum('bhvk,bhk->bhv', S, kk_t)
        S = (S * w_t[:, :, None, :]
             - sa[..., None] * (kk_t * a_t)[:, :, None, :]
             + v_t[..., None] * k_t[:, :, None, :])
        return S, jnp.einsum('bhvk,bhk->bhv', S, r_t)

    s_fin, ys = lax.scan(step, s0.astype(jnp.float32), xs, reverse=reverse)
    return jnp.moveaxis(ys, 0, 1), s_fin


def rwkv7_mixer(z, lp, s0):
    bsz, seq = z.shape[0], z.shape[1]
    f32 = jnp.float32
    z = token_shift_bidir(z, lp['rwkv_mu'])
    r, k, v, wl, al, gl = jnp.split(z, RWKV_SPLITS, axis=-1)

    def heads(t):
        return t.astype(f32).reshape(bsz, seq, RWKV_HEADS, RWKV_HEAD_DIM)

    g = jax.nn.sigmoid(gl) @ lp['rwkv_g_up']
    kk = heads(k * lp['rwkv_k_k'])
    kk = kk * lax.rsqrt(jnp.sum(kk * kk, axis=-1, keepdims=True) + EPS)
    tw = jnp.tanh(wl.astype(f32))
    alf = al.astype(f32)
    rh, vh = heads(r), heads(v)
    o_sum = jnp.zeros_like(rh)
    k_bonus = jnp.zeros_like(rh)
    finals = []
    for d, rev in ((0, False), (1, True)):
        w_log = -jnp.exp(-jax.nn.softplus(-(lp['rwkv_w0'][d].astype(f32)
                                             + tw @ lp['rwkv_w_up'][d].astype(f32))) - 0.5)
        a = jax.nn.sigmoid(lp['rwkv_a0'][d].astype(f32) + alf @ lp['rwkv_a_up'][d].astype(f32))
        kd = k.astype(f32) * (1.0 + (a - 1.0) * lp['rwkv_k_a'].astype(f32))
        o_d, s_d = wkv7_scan(rh, heads(jnp.exp(w_log)), heads(kd), vh, kk, heads(a), s0[:, d], rev)
        o_sum = o_sum + o_d
        k_bonus = k_bonus + heads(kd)
        finals.append(s_d)
    mean = jnp.mean(o_sum, axis=-1, keepdims=True)
    var = jnp.mean(jnp.square(o_sum - mean), axis=-1, keepdims=True)
    o_n = ((o_sum - mean) * lax.rsqrt(var + RWKV_GN_EPS)).reshape(bsz, seq, RWKV_WIDTH)
    o_n = o_n * lp['rwkv_ln_g'].astype(f32) + lp['rwkv_ln_b'].astype(f32)
    bonus = jnp.sum(rh * 0.5 * k_bonus * lp['rwkv_r_k'].astype(f32), axis=-1, keepdims=True) * vh
    out = (o_n + bonus.reshape(bsz, seq, RWKV_WIDTH)) * g.astype(f32)
    return out.astype(z.dtype), jnp.stack(finals, axis=1)


def complex_affine_combine(e1, e2):
    a1r, a1i, b1r, b1i = e1
    a2r, a2i, b2r, b2i = e2
    return (a1r * a2r - a1i * a2i, a1r * a2i + a1i * a2r,
            a2r * b1r - a2i * b1i + b2r, a2r * b1i + a2i * b1r + b2i)


def s5_mixer(u, lp, x0):
    bsz, seq = u.shape[0], u.shape[1]
    f32 = jnp.float32
    uf = u.astype(f32)
    ug = uf.reshape(bsz, seq, S5_GROUPS, S5_GROUP)
    bu_re = jnp.einsum('blgc,gnc->blgn', ug, lp['s5_b_re'].astype(f32))
    bu_im = jnp.einsum('blgc,gnc->blgn', ug, lp['s5_b_im'].astype(f32))
    c_re, c_im = lp['s5_c_re'].astype(f32), lp['s5_c_im'].astype(f32)
    y = lp['s5_d'].astype(f32) * uf
    finals = []
    for d, rev in ((0, False), (1, True)):
        lam_re = lp['s5_lam_re'][d].astype(f32)
        lam_im = lp['s5_lam_im'][d].astype(f32)
        dt = jnp.exp(lp['s5_log_step'][d].astype(f32))[:, None]
        mag = jnp.exp(lam_re * dt)
        ab_re, ab_im = mag * jnp.cos(lam_im * dt), mag * jnp.sin(lam_im * dt)
        den = lam_re * lam_re + lam_im * lam_im
        f_re = ((ab_re - 1.0) * lam_re + ab_im * lam_im) / den
        f_im = (ab_im * lam_re - (ab_re - 1.0) * lam_im) / den
        b_re = f_re * bu_re - f_im * bu_im
        b_im = f_re * bu_im + f_im * bu_re
        first = seq - 1 if rev else 0
        last = 0 if rev else seq - 1
        x0r, x0i = x0[:, d, 0].astype(f32), x0[:, d, 1].astype(f32)
        b_re = b_re.at[:, first].add(ab_re * x0r - ab_im * x0i)
        b_im = b_im.at[:, first].add(ab_re * x0i + ab_im * x0r)
        a_re = jnp.broadcast_to(ab_re, b_re.shape)
        a_im = jnp.broadcast_to(ab_im, b_im.shape)
        _, _, xr, xi = lax.associative_scan(complex_affine_combine, (a_re, a_im, b_re, b_im),
                                            reverse=rev, axis=1)
        y_g = jnp.einsum('blgn,gcn->blgc', xr, c_re) - jnp.einsum('blgn,gcn->blgc', xi, c_im)
        y = y + y_g.reshape(bsz, seq, S5_WIDTH)
        finals.append(jnp.stack([xr[:, last], xi[:, last]], axis=1))
    hg = jax.nn.gelu(y)
    out = hg * jax.nn.sigmoid(hg @ lp['s5_w_glu'].astype(f32))
    return out.astype(u.dtype), jnp.stack(finals, axis=1)


def swiglu(x, w_gate_up, w_down):
    gte, up = jnp.split(x @ w_gate_up, 2, axis=-1)
    return (jax.nn.silu(gte) * up) @ w_down


def moe_ffn(h, lp):
    bsz, seq, dm = h.shape
    xt = h.reshape(-1, dm)
    n_tok = bsz * seq
    scores = jax.nn.sigmoid(xt.astype(jnp.float32) @ lp['moe_router'].astype(jnp.float32))
    _, top_idx = lax.top_k(scores + lp['moe_bias'].astype(jnp.float32), TOP_K)
    top_w = jnp.take_along_axis(scores, top_idx, axis=-1)
    top_w = top_w / jnp.sum(top_w, axis=-1, keepdims=True) * ROUTE_SCALE
    n_assign = n_tok * TOP_K
    flat_e = top_idx.reshape(-1)
    order = jnp.argsort(flat_e)
    e_sorted = flat_e[order]
    tok_sorted = order // TOP_K
    counts = jnp.bincount(flat_e, length=N_EXPERTS)
    padded = (counts + MOE_BLOCK - 1) // MOE_BLOCK * MOE_BLOCK
    pad_end = jnp.cumsum(padded)
    pad_start = pad_end - padded
    start = jnp.cumsum(counts) - counts
    slot = pad_start[e_sorted] + jnp.arange(n_assign) - start[e_sorted]
    n_blocks = -(-n_assign // MOE_BLOCK) + N_EXPERTS
    n_slots = n_blocks * MOE_BLOCK
    slot_tok = jnp.full((n_slots,), n_tok, jnp.int32).at[slot].set(tok_sorted.astype(jnp.int32))
    slot_w = jnp.zeros((n_slots,), jnp.float32).at[slot].set(top_w.reshape(-1)[order])
    block_e = jnp.minimum(jnp.searchsorted(pad_end, jnp.arange(n_blocks) * MOE_BLOCK, side='right'),
                          N_EXPERTS - 1)
    x_pad = jnp.concatenate([xt, jnp.zeros((1, dm), xt.dtype)], axis=0)
    w_gu, w_dn = lp['moe_w_gate_up'], lp['moe_w_down']

    def expert_block(args):
        tok_blk, w_blk, e = args
        yb = swiglu(x_pad[tok_blk], w_gu[e], w_dn[e])
        return yb * w_blk[:, None].astype(yb.dtype)

    y_slots = lax.map(expert_block, (slot_tok.reshape(n_blocks, MOE_BLOCK),
                                     slot_w.reshape(n_blocks, MOE_BLOCK), block_e))
    routed = jnp.zeros((n_tok + 1, dm), h.dtype).at[slot_tok].add(
        y_slots.reshape(n_slots, dm).astype(h.dtype))[:n_tok]
    shared = swiglu(xt, lp['shared_w_gate_up'], lp['shared_w_down'])
    return (routed + shared).reshape(bsz, seq, dm)


def trunk_layer(x, cvec, lp, lam_init, ctx):
    bsz, seq, _ = x.shape
    sh1, sc1, g1, sh2, sc2, g2 = modulation(cvec, lp['w_mod'], lp['b_mod'])
    h = rms_norm(x, lp['norm1']) * (1.0 + sc1) + sh1
    q, k, v, z, u = jnp.split(h @ lp['w_in'], PROJ_SPLITS, axis=-1)
    q = q.reshape(bsz, seq, ATT_HEADS, 2, ATT_QK_DIM)
    k = k.reshape(bsz, seq, ATT_HEADS, 2, ATT_QK_DIM)
    v = v.reshape(bsz, seq, ATT_HEADS, ATT_V_DIM)
    if ctx is None:
        k_all, v_all = k, v
        s_rwkv0 = jnp.zeros((bsz, 2, RWKV_HEADS, RWKV_HEAD_DIM, RWKV_HEAD_DIM), jnp.float32)
        s_s50 = jnp.zeros((bsz, 2, 2, S5_GROUPS, S5_STATE), jnp.float32)
    else:
        ctx_k, ctx_v, s_rwkv0, s_s50 = ctx
        cos, sin = axial_rope(seq)
        q = apply_rope(q, cos, sin)
        k = apply_rope(k, cos, sin)
        k_all = jnp.concatenate([ctx_k.astype(k.dtype), k], axis=1)
        v_all = jnp.concatenate([ctx_v.astype(v.dtype), v], axis=1)
    lv = lp['att_lambda'].astype(jnp.float32)
    lam = jnp.exp(jnp.sum(lv[0] * lv[1])) - jnp.exp(jnp.sum(lv[2] * lv[3])) + lam_init
    o_att = diff_attention(q, k_all, v_all, lam, lam_init, lp['att_subln'])
    o_rwkv, st_rwkv = rwkv7_mixer(z, lp, s_rwkv0)
    o_s5, st_s5 = s5_mixer(u, lp, s_s50)
    o = jnp.concatenate([o_att.astype(x.dtype), o_rwkv, o_s5], axis=-1) @ lp['w_out']
    x = x + g1 * o
    h2 = rms_norm(x, lp['norm2']) * (1.0 + sc2) + sh2
    x = x + g2 * moe_ffn(h2, lp)
    return x, (k, v, st_rwkv, st_s5)


def setup_inputs(seed: int = 0) -> dict:
    key = jax.random.key(seed)
    keys = list(jax.random.split(key, 64))
    f32 = jnp.float32

    def nrm(shape, scale):
        return jax.random.normal(keys.pop(), shape, f32) * scale

    def unif(shape, lo, hi):
        return jax.random.uniform(keys.pop(), shape, f32, lo, hi)

    D = D_MODEL
    n_idx = jnp.arange(S5_STATE, dtype=f32)
    return {
        'x_prompt': nrm((BATCH, SEQ, D), 1.0),
        'x_sample': nrm((DEC_BATCH, DEC_SEQ, D), 1.0),
        'cache_attn_k': nrm((DEC_BATCH, DEPTH, PAST_LEN, ATT_HEADS, 2, ATT_QK_DIM), 1.0),
        'cache_attn_v': nrm((DEC_BATCH, DEPTH, PAST_LEN, ATT_HEADS, ATT_V_DIM), 1.0),
        'state_rwkv': nrm((DEC_BATCH, DEPTH, 2, RWKV_HEADS, RWKV_HEAD_DIM, RWKV_HEAD_DIM), 0.3),
        'state_s5': nrm((DEC_BATCH, DEPTH, 2, 2, S5_GROUPS, S5_STATE), 0.05),
        'c': nrm((DEC_BATCH, D), 1.0),
        'c_ctx': nrm((D,), 1.0),
        'w_mod': nrm((DEPTH, D, 6 * D), 0.5 * D ** -0.5),
        'b_mod': nrm((DEPTH, 6 * D), 0.02),
        'norm1': 1.0 + nrm((DEPTH, D), 0.02),
        'norm2': 1.0 + nrm((DEPTH, D), 0.02),
        'w_in': nrm((DEPTH, D, PROJ_WIDTH), D ** -0.5),
        'w_out': nrm((DEPTH, MIX_WIDTH, D), MIX_WIDTH ** -0.5),
        'att_lambda': nrm((DEPTH, 4, ATT_QK_DIM), 0.1),
        'att_subln': 1.0 + nrm((DEPTH, ATT_V_DIM), 0.02),
        'rwkv_mu': unif((DEPTH, RWKV_PROJ), 0.0, 1.0),
        'rwkv_w0': nrm((DEPTH, 2, RWKV_WIDTH), 0.5),
        'rwkv_w_up': nrm((DEPTH, 2, RWKV_DECAY_RANK, RWKV_WIDTH), 0.5 * RWKV_DECAY_RANK ** -0.5),
        'rwkv_a0': nrm((DEPTH, 2, RWKV_WIDTH), 0.5),
        'rwkv_a_up': nrm((DEPTH, 2, RWKV_ICLR_RANK, RWKV_WIDTH), 0.5 * RWKV_ICLR_RANK ** -0.5),
        'rwkv_g_up': nrm((DEPTH, RWKV_GATE_RANK, RWKV_WIDTH), RWKV_GATE_RANK ** -0.5),
        'rwkv_k_k': 0.85 + nrm((DEPTH, RWKV_WIDTH), 0.05),
        'rwkv_k_a': 1.0 + nrm((DEPTH, RWKV_WIDTH), 0.05),
        'rwkv_r_k': nrm((DEPTH, RWKV_HEADS, RWKV_HEAD_DIM), 0.1),
        'rwkv_ln_g': 1.0 + nrm((DEPTH, RWKV_WIDTH), 0.02),
        'rwkv_ln_b': nrm((DEPTH, RWKV_WIDTH), 0.02),
        's5_lam_re': -0.5 + nrm((DEPTH, 2, S5_GROUPS, S5_STATE), 0.01),
        's5_lam_im': jnp.pi * n_idx + nrm((DEPTH, 2, S5_GROUPS, S5_STATE), 0.01),
        's5_log_step': unif((DEPTH, 2, S5_GROUPS), math.log(1e-3), math.log(1e-1)),
        's5_b_re': nrm((DEPTH, S5_GROUPS, S5_STATE, S5_GROUP), (2 * S5_GROUP) ** -0.5),
        's5_b_im': nrm((DEPTH, S5_GROUPS, S5_STATE, S5_GROUP), (2 * S5_GROUP) ** -0.5),
        's5_c_re': nrm((DEPTH, S5_GROUPS, S5_GROUP, S5_STATE), S5_STATE ** -0.5),
        's5_c_im': nrm((DEPTH, S5_GROUPS, S5_GROUP, S5_STATE), S5_STATE ** -0.5),
        's5_d': nrm((DEPTH, S5_WIDTH), 0.5),
        's5_w_glu': nrm((DEPTH, S5_WIDTH, S5_WIDTH), S5_WIDTH ** -0.5),
        'moe_router': nrm((DEPTH, D, N_EXPERTS), D ** -0.5),
        'moe_bias': nrm((DEPTH, N_EXPERTS), 0.01),
        'moe_w_gate_up': nrm((DEPTH, N_EXPERTS, D, 2 * EXPERT_DIM), D ** -0.5),
        'moe_w_down': nrm((DEPTH, N_EXPERTS, EXPERT_DIM, D), EXPERT_DIM ** -0.5),
        'shared_w_gate_up': nrm((DEPTH, D, 2 * SHARED_DIM), D ** -0.5),
        'shared_w_down': nrm((DEPTH, SHARED_DIM, D), SHARED_DIM ** -0.5),
        'final_norm': 1.0 + nrm((D,), 0.02),
    }


def reference(x_prompt, x_sample, cache_attn_k, cache_attn_v, state_rwkv, state_s5, c, c_ctx,
              w_mod, b_mod, norm1, norm2, w_in, w_out, att_lambda, att_subln,
              rwkv_mu, rwkv_w0, rwkv_w_up, rwkv_a0, rwkv_a_up, rwkv_g_up, rwkv_k_k, rwkv_k_a,
              rwkv_r_k, rwkv_ln_g, rwkv_ln_b,
              s5_lam_re, s5_lam_im, s5_log_step, s5_b_re, s5_b_im, s5_c_re, s5_c_im, s5_d, s5_w_glu,
              moe_router, moe_bias, moe_w_gate_up, moe_w_down, shared_w_gate_up, shared_w_down,
              final_norm):
    params = dict(w_mod=w_mod, b_mod=b_mod, norm1=norm1, norm2=norm2, w_in=w_in, w_out=w_out,
                  att_lambda=att_lambda, att_subln=att_subln,
                  rwkv_mu=rwkv_mu, rwkv_w0=rwkv_w0, rwkv_w_up=rwkv_w_up, rwkv_a0=rwkv_a0,
                  rwkv_a_up=rwkv_a_up, rwkv_g_up=rwkv_g_up, rwkv_k_k=rwkv_k_k, rwkv_k_a=rwkv_k_a,
                  rwkv_r_k=rwkv_r_k, rwkv_ln_g=rwkv_ln_g, rwkv_ln_b=rwkv_ln_b,
                  s5_lam_re=s5_lam_re, s5_lam_im=s5_lam_im, s5_log_step=s5_log_step,
                  s5_b_re=s5_b_re, s5_b_im=s5_b_im, s5_c_re=s5_c_re, s5_c_im=s5_c_im,
                  s5_d=s5_d, s5_w_glu=s5_w_glu,
                  moe_router=moe_router, moe_bias=moe_bias, moe_w_gate_up=moe_w_gate_up,
                  moe_w_down=moe_w_down, shared_w_gate_up=shared_w_gate_up,
                  shared_w_down=shared_w_down)
    yp, ys = x_prompt, x_sample
    c_ctx_vec = c_ctx[None, :]
    new_k, new_v, new_r, new_s = [], [], [], []
    for l in range(DEPTH):
        lp = {name: arr[l] for name, arr in params.items()}
        lam_init = 0.8 - 0.6 * math.exp(-0.3 * l)
        yp, (ck, cv, cr, cs) = trunk_layer(yp, c_ctx_vec, lp, lam_init, None)
        new_k.append(ck)
        new_v.append(cv)
        new_r.append(cr)
        new_s.append(cs)
        ys, _ = trunk_layer(ys, c, lp, lam_init,
                            (cache_attn_k[:, l], cache_attn_v[:, l], state_rwkv[:, l], state_s5[:, l]))
    y_prompt = rms_norm(yp, final_norm)
    y_sample = rms_norm(ys, final_norm)
    new_attn_k = jnp.stack(new_k, axis=1)
    new_attn_v = jnp.stack(new_v, axis=1)
    new_state_rwkv = jnp.stack(new_r, axis=1)
    new_state_s5 = jnp.stack(new_s, axis=1)
    return (y_prompt, y_sample, new_attn_k, new_attn_v, new_state_rwkv, new_state_s5)
```

```python
import functools
import math

import jax
import jax.numpy as jnp
from jax import lax
from jax.experimental import pallas as pl
from jax.experimental.pallas import tpu as pltpu

F32 = jnp.float32
BF16 = jnp.bfloat16
U32 = jnp.uint32
HI = lax.Precision.HIGHEST

D = 2048
DEPTH = 2
EPS = 1e-6
GRID_W = 64
ROPE_THETA = 10000.0
HEADS = 8
QK = 64
ATT_W = 1024
RW = 512
RW_PROJ = 1792
S5W = 512
S5G = 32
S5N = 64
S5C = 16
S5S = S5G * S5N
GN_EPS = 64e-5
NE = 64
TOPK = 8
EDIM = 512
ROUTE_SCALE = 2.5
PROJ_W = 5376
PROJ_STRIDE = 7168
COL_Q, COL_K, COL_V, COL_U, COL_Z = 0, 1024, 2048, 3072, 3584
PROJ_TN = 1792
MOE_BM = 256
VMEM_LIMIT = 56 * 1024 * 1024


def _cp(sem):
    return pltpu.CompilerParams(dimension_semantics=sem, vmem_limit_bytes=VMEM_LIMIT)


def _sigmoid(x):
    return 1.0 / (1.0 + jnp.exp(-x))


def _dot(a, b):
    return jnp.dot(a, b, preferred_element_type=F32)


def _dot_hi(a, b):
    return jnp.dot(a, b, precision=HI, preferred_element_type=F32)


def _pack_bf16_pair(lo, hi):
    lo_b = lax.bitcast_convert_type(lo.astype(BF16).astype(F32), U32) >> 16
    hi_b = lax.bitcast_convert_type(hi.astype(BF16).astype(F32), U32) & jnp.uint32(0xFFFF0000)
    return lo_b | hi_b


def _unpack_bf16_pair(p):
    lo = lax.bitcast_convert_type(p << 16, F32)
    hi = lax.bitcast_convert_type(p & jnp.uint32(0xFFFF0000), F32)
    return lo, hi


def _mod_kernel(c_ref, w_ref, b_ref, o_ref):
    c = c_ref[...]
    s = c * _sigmoid(c)
    o_ref[...] = _dot(s.astype(BF16), w_ref[...].astype(BF16)) + b_ref[...]


def _modulation(cpad, w_mod, b_mod):
    tn = 1024
    return pl.pallas_call(
        _mod_kernel,
        grid=(6 * D // tn,),
        in_specs=[pl.BlockSpec((16, D), lambda j: (0, 0)),
                  pl.BlockSpec((D, tn), lambda j: (0, j)),
                  pl.BlockSpec((1, tn), lambda j: (0, j))],
        out_specs=pl.BlockSpec((16, tn), lambda j: (0, j)),
        out_shape=jax.ShapeDtypeStruct((16, 6 * D), F32),
        compiler_params=_cp(("arbitrary",)),
        name="modulation",
    )(cpad, w_mod, b_mod.reshape(1, 6 * D))


def _in_kernel(x_ref, g_ref, sc_ref, sh_ref, w_ref, o_ref, h_ref):
    @pl.when(pl.program_id(2) == 0)
    def _():
        x = x_ref[...]
        y = x * lax.rsqrt(jnp.mean(x * x, axis=-1, keepdims=True) + EPS)
        h = (y * g_ref[...]) * (1.0 + sc_ref[0]) + sh_ref[0]
        h_ref[...] = h.astype(BF16)

    o_ref[...] = _dot(h_ref[...], w_ref[...])


def _in_proj(x, norm_g, mods6, w_in_bf, seg):
    B, L, mod_base, mod_stride = seg
    tm = min(L, 512)
    nt = L // tm
    ncol = PROJ_W // PROJ_TN
    nstride = PROJ_STRIDE // PROJ_TN

    def mrow(j):
        return lambda b, t, c: ((mod_base + b * mod_stride) * 6 + j, 0, 0)

    return pl.pallas_call(
        _in_kernel,
        grid=(B, nt, ncol),
        in_specs=[pl.BlockSpec((tm, D), lambda b, t, c: (b * nt + t, 0)),
                  pl.BlockSpec((1, D), lambda b, t, c: (0, 0)),
                  pl.BlockSpec((1, 1, D), mrow(1)),
                  pl.BlockSpec((1, 1, D), mrow(0)),
                  pl.BlockSpec((D, PROJ_TN), lambda b, t, c: (0, c))],
        out_specs=pl.BlockSpec((tm, PROJ_TN), lambda b, t, c: (t, b * nstride + c)),
        out_shape=jax.ShapeDtypeStruct((L, B * PROJ_STRIDE), F32),
        scratch_shapes=[pltpu.VMEM((tm, D), BF16)],
        compiler_params=_cp(("arbitrary", "arbitrary", "arbitrary")),
        name="in_proj",
    )(x, norm_g.reshape(1, D), mods6, mods6, w_in_bf)


def _rope(x, c, s):
    lane = lax.broadcasted_iota(jnp.int32, x.shape, 1)
    nxt = pltpu.roll(x, 127, 1)
    prv = pltpu.roll(x, 1, 1)
    swapped = jnp.where((lane & 1) == 0, nxt, prv)
    return x * c + swapped * s


def _attn_kernel(*refs, lam_init, n_ctx, rope):
    if rope:
        (q_ref, k_ref, v_ref, ck_ref, cv_ref, cq_ref, sq_ref, ckk_ref, skk_ref,
         lam_ref, g_ref, o_ref, kall_ref, vall_ref) = refs
    else:
        q_ref, k_ref, v_ref, lam_ref, g_ref, o_ref, kall_ref, vall_ref = refs

    @pl.when(pl.program_id(2) == 0)
    def _():
        k = k_ref[...]
        if rope:
            k = _rope(k, ckk_ref[...], skk_ref[...])
            kall_ref[0:n_ctx, :] = ck_ref[0].astype(BF16)
            vall_ref[0:n_ctx, :] = cv_ref[0].astype(BF16)
        kall_ref[n_ctx:, :] = k.astype(BF16)
        vall_ref[n_ctx:, :] = v_ref[...].astype(BF16)

    q = q_ref[...]
    if rope:
        q = _rope(q, cq_ref[...], sq_ref[...])
    q = q * (QK ** -0.5)
    lane = lax.broadcasted_iota(jnp.int32, q.shape, 1)
    kall = kall_ref[...]
    vall = vall_ref[...]
    outs = []
    for m in range(2):
        qm = jnp.where((lane < QK) == (m == 0), q, 0.0).astype(BF16)
        s = lax.dot_general(qm, kall, (((1,), (1,)), ((), ())), preferred_element_type=F32)
        mx = jnp.max(s, axis=-1, keepdims=True)
        e = jnp.exp(s - mx)
        l = jnp.sum(e, axis=-1, keepdims=True)
        outs.append(_dot(e.astype(BF16), vall) / l)
    lv = lam_ref[...]
    lam = (jnp.exp(jnp.sum(lv[0:1] * lv[1:2], axis=-1, keepdims=True))
           - jnp.exp(jnp.sum(lv[2:3] * lv[3:4], axis=-1, keepdims=True)) + lam_init)
    o = outs[0] - lam * outs[1]
    o = o * lax.rsqrt(jnp.mean(o * o, axis=-1, keepdims=True) + EPS) * g_ref[...]
    o_ref[...] = o * (1.0 - lam_init)


def _attention(proj, att_lambda, subln, lam_init, seg, ctx=None, tables=None):
    B, L = seg[0], seg[1]
    tq = min(L, 512)
    nq = L // tq
    cs = PROJ_STRIDE // 128
    rope = ctx is not None
    n_ctx = ctx[0].shape[1] if rope else 0
    specs = [pl.BlockSpec((tq, 128), lambda b, h, t: (t, b * cs + COL_Q // 128 + h)),
             pl.BlockSpec((L, 128), lambda b, h, t: (0, b * cs + COL_K // 128 + h)),
             pl.BlockSpec((L, 128), lambda b, h, t: (0, b * cs + COL_V // 128 + h))]
    args = [proj, proj, proj]
    if rope:
        cos_t, sin_t = tables
        specs += [pl.BlockSpec((1, n_ctx, 128), lambda b, h, t: (b, 0, h)),
                  pl.BlockSpec((1, n_ctx, 128), lambda b, h, t: (b, 0, h)),
                  pl.BlockSpec((tq, 128), lambda b, h, t: (t, 0)),
                  pl.BlockSpec((tq, 128), lambda b, h, t: (t, 0)),
                  pl.BlockSpec((L, 128), lambda b, h, t: (0, 0)),
                  pl.BlockSpec((L, 128), lambda b, h, t: (0, 0))]
        args += [ctx[0], ctx[1], cos_t, sin_t, cos_t, sin_t]
    specs += [pl.BlockSpec((4, QK), lambda b, h, t: (0, 0)),
              pl.BlockSpec((1, 128), lambda b, h, t: (0, 0))]
    args += [att_lambda, subln.reshape(1, 128)]
    return pl.pallas_call(
        functools.partial(_attn_kernel, lam_init=lam_init, n_ctx=n_ctx, rope=rope),
        grid=(B, HEADS, nq),
        in_specs=specs,
        out_specs=pl.BlockSpec((tq, 128), lambda b, h, t: (b * nq + t, h)),
        out_shape=jax.ShapeDtypeStruct((B * L, ATT_W), F32),
        scratch_shapes=[pltpu.VMEM((n_ctx + L, 128), BF16), pltpu.VMEM((n_ctx + L, 128), BF16)],
        compiler_params=_cp(("arbitrary", "arbitrary", "arbitrary")),
        name="diff_attention",
    )(*args)


def _rwkv_pre_kernel(z_ref, zp_ref, zn_ref, mu_ref, seg_ref, wup_ref, aup_ref, gup_ref,
                     w0_ref, a0_ref, kk_ref, ka_ref, sh_ref, pd_ref, ms_ref, *, tm):
    t = pl.program_id(1)
    nt = pl.num_programs(1)
    z = z_ref[...]
    row = lax.broadcasted_iota(jnp.int32, z.shape, 0)
    prev_row = jnp.where(t > 0, zp_ref[7:8, :], 0.0)
    next_row = jnp.where(t < nt - 1, zn_ref[0:1, :], 0.0)
    zp = jnp.where(row == 0, prev_row, pltpu.roll(z, 1, 0))
    zn = jnp.where(row == tm - 1, next_row, pltpu.roll(z, tm - 1, 0))
    zs = z + mu_ref[...] * (0.5 * (zp + zn) - z)
    r = zs[:, 0:RW]
    k = zs[:, RW:2 * RW]
    v = zs[:, 2 * RW:3 * RW]
    wa = zs[:, 3 * RW:3 * RW + 128]
    gl = zs[:, 3 * RW + 128:3 * RW + 256]
    lane = lax.broadcasted_iota(jnp.int32, wa.shape, 1)
    wa = jnp.where(lane < 64, jnp.tanh(wa), wa)
    g = _dot_hi(_sigmoid(gl), gup_ref[...])
    kk = k * kk_ref[...]
    kk = kk * lax.rsqrt(_dot_hi(kk * kk, seg_ref[...]) + EPS)
    sh_ref[0] = r
    sh_ref[1] = v
    sh_ref[2] = kk
    kb = jnp.zeros_like(k)
    for d in range(2):
        xw = w0_ref[d] + _dot_hi(wa, wup_ref[d])
        w = jnp.exp(-math.exp(-0.5) * _sigmoid(xw))
        a = _sigmoid(a0_ref[d] + _dot_hi(wa, aup_ref[d]))
        kd = k * (1.0 + (a - 1.0) * ka_ref[...])
        pd_ref[0, d] = w
        pd_ref[1, d] = kd
        pd_ref[2, d] = kk * a
        kb = kb + kd
    ms_ref[0] = g
    ms_ref[1] = kb


def _rwkv_pre(proj, lp, segmat, seg):
    B, L = seg[0], seg[1]
    tm = min(L, 256)
    nt = L // tm
    zs = PROJ_STRIDE // RW_PROJ
    zc = COL_Z // RW_PROJ
    nb8 = L // 8
    zeros64 = jnp.zeros((2, 64, RW), F32)
    wup = jnp.concatenate([lp['rwkv_w_up'], zeros64], axis=1)
    aup = jnp.concatenate([zeros64, lp['rwkv_a_up']], axis=1)
    c2 = lambda b, t: (0, 0)
    c3 = lambda b, t: (0, 0, 0)
    return pl.pallas_call(
        functools.partial(_rwkv_pre_kernel, tm=tm),
        grid=(B, nt),
        in_specs=[pl.BlockSpec((tm, RW_PROJ), lambda b, t: (t, b * zs + zc)),
                  pl.BlockSpec((8, RW_PROJ), lambda b, t: (jnp.maximum(t * (tm // 8) - 1, 0), b * zs + zc)),
                  pl.BlockSpec((8, RW_PROJ), lambda b, t: (jnp.minimum((t + 1) * (tm // 8), nb8 - 1), b * zs + zc)),
                  pl.BlockSpec((1, RW_PROJ), c2),
                  pl.BlockSpec((RW, RW), c2),
                  pl.BlockSpec((2, 128, RW), c3),
                  pl.BlockSpec((2, 128, RW), c3),
                  pl.BlockSpec((128, RW), c2),
                  pl.BlockSpec((2, 1, RW), c3),
                  pl.BlockSpec((2, 1, RW), c3),
                  pl.BlockSpec((1, RW), c2),
                  pl.BlockSpec((1, RW), c2)],
        out_specs=[pl.BlockSpec((3, tm, RW), lambda b, t: (0, t, b)),
                   pl.BlockSpec((3, 2, tm, RW), lambda b, t: (0, 0, t, b)),
                   pl.BlockSpec((2, tm, RW), lambda b, t: (0, t, b))],
        out_shape=[jax.ShapeDtypeStruct((3, L, B * RW), F32),
                   jax.ShapeDtypeStruct((3, 2, L, B * RW), F32),
                   jax.ShapeDtypeStruct((2, L, B * RW), F32)],
        compiler_params=_cp(("arbitrary", "arbitrary")),
        name="rwkv_pre",
    )(proj, proj, proj, lp['rwkv_mu'].reshape(1, RW_PROJ), segmat, wup, aup, lp['rwkv_g_up'],
      lp['rwkv_w0'].reshape(2, 1, RW), lp['rwkv_a0'].reshape(2, 1, RW),
      lp['rwkv_k_k'].reshape(1, RW), lp['rwkv_k_a'].reshape(1, RW))


def _wkv_kernel(x_ref, s0_ref, y_ref, sf_ref, st_ref, *, tc):
    c = pl.program_id(1)

    @pl.when(c == 0)
    def _():
        st_ref[...] = s0_ref[...]

    def step(t, carry):
        def vrow(vi, carry2):
            s = st_ref[vi]
            sa = jnp.sum(s * x_ref[2, t], axis=0, keepdims=True)
            vv = x_ref[1, t, pl.ds(vi, 1), :]
            s_new = s * x_ref[3, t] + (vv * x_ref[4, t] - sa * x_ref[5, t])
            st_ref[vi] = s_new
            y_ref[t, pl.ds(vi, 1), :] = jnp.sum(s_new * x_ref[0, t], axis=0, keepdims=True)
            return carry2

        return lax.fori_loop(0, QK, vrow, carry)

    lax.fori_loop(0, tc, step, 0)

    @pl.when(c == pl.num_programs(1) - 1)
    def _():
        sf_ref[...] = st_ref[...]


def _wkv_scan(xs, s0):
    _, L, _, lanes = xs.shape
    tc = 32
    return pl.pallas_call(
        functools.partial(_wkv_kernel, tc=tc),
        grid=(lanes // 128, L // tc),
        in_specs=[pl.BlockSpec((6, tc, QK, 128), lambda g, c: (0, c, 0, g)),
                  pl.BlockSpec((QK, QK, 128), lambda g, c: (0, 0, g))],
        out_specs=[pl.BlockSpec((tc, QK, 128), lambda g, c: (c, 0, g)),
                   pl.BlockSpec((QK, QK, 128), lambda g, c: (0, 0, g))],
        out_shape=[jax.ShapeDtypeStruct((L, QK, lanes), F32),
                   jax.ShapeDtypeStruct((QK, QK, lanes), F32)],
        scratch_shapes=[pltpu.VMEM((QK, QK, 128), F32)],
        compiler_params=_cp(("arbitrary", "arbitrary")),
        name="wkv7_scan",
    )(xs, s0)


def _rwkv_mix(proj, lp, segmat, seg, s0_bdhvk):
    B, L = seg[0], seg[1]
    shared, perdir, misc = _rwkv_pre(proj, lp, segmat, seg)
    sh = shared.reshape(3, L, B, HEADS, QK)
    pd = perdir.reshape(3, 2, L, B, HEADS, QK)
    full = jnp.concatenate(
        [jnp.stack([sh, sh[:, ::-1]], axis=1),
         jnp.concatenate([pd[:, 0:1], pd[:, 1:2, ::-1]], axis=1)], axis=0)
    xs = full.transpose(0, 2, 5, 1, 3, 4).reshape(6, L, QK, 2 * B * HEADS)
    lanes = 2 * B * HEADS
    if s0_bdhvk is None:
        s0 = jnp.zeros((QK, QK, lanes), F32)
    else:
        s0 = s0_bdhvk.astype(F32).transpose(3, 4, 1, 0, 2).reshape(QK, QK, lanes)
    y, sf = _wkv_scan(xs, s0)
    y = y.reshape(L, QK, 2, B, HEADS).transpose(2, 0, 3, 4, 1)
    o_sum = (y[0] + y[1, ::-1]).reshape(L, B * RW)
    s_fin = sf.reshape(QK, QK, 2, B, HEADS).transpose(3, 2, 4, 0, 1)
    return o_sum, shared, misc, s_fin


def _s5_kernel(u_ref, wb_ref, wc_ref, ab_ref, x0_ref, y_ref, xf_ref, bx_ref, st_ref, *, tc, nb):
    d = pl.program_id(0)
    c = pl.program_id(1)

    @pl.when(c == 0)
    def _():
        st_ref[...] = x0_ref[0]

    u = u_ref[...].reshape(tc * nb, S5W).astype(BF16)
    bx_ref[...] = _dot(u, wb_ref[0])
    ar = jnp.broadcast_to(ab_ref[0, 0:1, :], (nb, S5S))
    ai = jnp.broadcast_to(ab_ref[0, 1:2, :], (nb, S5S))

    def step(i, carry):
        tt = jnp.where(d == 0, i, tc - 1 - i)
        rows = pl.ds(pl.multiple_of(tt * nb, nb), nb)
        xr = st_ref[0]
        xi = st_ref[1]
        nr = ar * xr - ai * xi + bx_ref[rows, 0:S5S]
        ni = ar * xi + ai * xr + bx_ref[rows, S5S:2 * S5S]
        st_ref[0] = nr
        st_ref[1] = ni
        bx_ref[rows, 0:S5S] = nr
        bx_ref[rows, S5S:2 * S5S] = ni
        return carry

    lax.fori_loop(0, tc, step, 0)
    y = _dot(bx_ref[...].astype(BF16), wc_ref[...])
    y_ref[0] = y.reshape(tc, nb, S5W)

    @pl.when(c == pl.num_programs(1) - 1)
    def _():
        xf_ref[0] = st_ref[...]


def _s5_scan(proj3, wb, wc, ab, x0, seg):
    B, L = seg[0], seg[1]
    tc = 64 if B <= 8 else 16
    nc = L // tc
    tmap = lambda d, c: jnp.where(d == 0, c, nc - 1 - c)
    return pl.pallas_call(
        functools.partial(_s5_kernel, tc=tc, nb=B),
        grid=(2, nc),
        in_specs=[pl.BlockSpec((tc, B, S5W), lambda d, c: (tmap(d, c), 0, COL_U // S5W)),
                  pl.BlockSpec((1, S5W, 2 * S5S), lambda d, c: (d, 0, 0)),
                  pl.BlockSpec((2 * S5S, S5W), lambda d, c: (0, 0)),
                  pl.BlockSpec((1, 2, S5S), lambda d, c: (d, 0, 0)),
                  pl.BlockSpec((1, 2, B, S5S), lambda d, c: (d, 0, 0, 0))],
        out_specs=[pl.BlockSpec((1, tc, B, S5W), lambda d, c: (d, tmap(d, c), 0, 0)),
                   pl.BlockSpec((1, 2, B, S5S), lambda d, c: (d, 0, 0, 0))],
        out_shape=[jax.ShapeDtypeStruct((2, L, B, S5W), F32),
                   jax.ShapeDtypeStruct((2, 2, B, S5S), F32)],
        scratch_shapes=[pltpu.VMEM((tc * B, 2 * S5S), F32), pltpu.VMEM((2, B, S5S), F32)],
        compiler_params=_cp(("arbitrary", "arbitrary")),
        name="s5_scan",
    )(proj3, wb, wc, ab, x0)


def _s5_params(lp):
    lam_re, lam_im = lp['s5_lam_re'], lp['s5_lam_im']
    dt = jnp.exp(lp['s5_log_step'])[:, :, None]
    mag = jnp.exp(lam_re * dt)
    ab_re, ab_im = mag * jnp.cos(lam_im * dt), mag * jnp.sin(lam_im * dt)
    den = lam_re * lam_re + lam_im * lam_im
    f_re = ((ab_re - 1.0) * lam_re + ab_im * lam_im) / den
    f_im = (ab_im * lam_re - (ab_re - 1.0) * lam_im) / den
    b_re, b_im = lp['s5_b_re'], lp['s5_b_im']
    wre = f_re[..., None] * b_re[None] - f_im[..., None] * b_im[None]
    wim = f_re[..., None] * b_im[None] + f_im[..., None] * b_re[None]
    eye = jnp.eye(S5G, dtype=F32)

    def block_in(w):
        return jnp.einsum('dgnc,gh->dgchn', w, eye).reshape(2, S5W, S5S)

    wb = jnp.concatenate([block_in(wre), block_in(wim)], axis=-1).astype(BF16)

    def block_out(cm):
        return jnp.einsum('gcn,gh->gnhc', cm, eye).reshape(S5S, S5W)

    wc = jnp.concatenate([block_out(lp['s5_c_re']), -block_out(lp['s5_c_im'])], axis=0).astype(BF16)
    ab = jnp.stack([ab_re.reshape(2, S5S), ab_im.reshape(2, S5S)], axis=1)
    return wb, wc, ab


def _mix_kernel(oatt_ref, osum_ref, sh_ref, ms_ref, y_ref, u_ref, seg_ref, lng_ref, lnb_ref,
                rk_ref, d_ref, wglu_ref, o_ref):
    segm = seg_ref[...]
    o = osum_ref[...]
    mean = _dot_hi(o, segm) * (1.0 / QK)
    oc = o - mean
    var = _dot_hi(oc * oc, segm) * (1.0 / QK)
    o_n = oc * lax.rsqrt(var + GN_EPS) * lng_ref[...] + lnb_ref[...]
    r = sh_ref[0]
    v = sh_ref[1]
    bonus = _dot_hi(r * 0.5 * ms_ref[1] * rk_ref[...], segm) * v
    rw = (o_n + bonus) * ms_ref[0]
    u = u_ref[...]
    y = d_ref[...] * u + y_ref[0] + y_ref[1]
    hg = 0.5 * y * (1.0 + jnp.tanh(math.sqrt(2.0 / math.pi) * (y + 0.044715 * (y * y * y))))
    s5 = hg * _sigmoid(_dot(hg.astype(BF16), wglu_ref[...]))
    o_ref[:, 0:ATT_W] = oatt_ref[...].astype(BF16)
    o_ref[:, ATT_W:ATT_W + RW] = rw.astype(BF16)
    o_ref[:, ATT_W + RW:D] = s5.astype(BF16)


def _mix_post(o_att, o_sum, shared, misc, y_s5, proj, segmat, lp, seg):
    B, L = seg[0], seg[1]
    tm = min(L, 256)
    nt = L // tm
    us = PROJ_STRIDE // S5W
    c2 = lambda b, t: (0, 0)
    return pl.pallas_call(
        _mix_kernel,
        grid=(B, nt),
        in_specs=[pl.BlockSpec((tm, ATT_W), lambda b, t: (b * nt + t, 0)),
                  pl.BlockSpec((tm, RW), lambda b, t: (t, b)),
                  pl.BlockSpec((3, tm, RW), lambda b, t: (0, t, b)),
                  pl.BlockSpec((2, tm, RW), lambda b, t: (0, t, b)),
                  pl.BlockSpec((2, tm, S5W), lambda b, t: (0, t, b)),
                  pl.BlockSpec((tm, S5W), lambda b, t: (t, b * us + COL_U // S5W)),
                  pl.BlockSpec((RW, RW), c2),
                  pl.BlockSpec((1, RW), c2),
                  pl.BlockSpec((1, RW), c2),
                  pl.BlockSpec((1, RW), c2),
                  pl.BlockSpec((1, S5W), c2),
                  pl.BlockSpec((S5W, S5W), c2)],
        out_specs=pl.BlockSpec((tm, D), lambda b, t: (b * nt + t, 0)),
        out_shape=jax.ShapeDtypeStruct((B * L, D), BF16),
        compiler_params=_cp(("arbitrary", "arbitrary")),
        name="mix_post",
    )(o_att, o_sum, shared, misc, y_s5.reshape(2, L, B * S5W), proj, segmat,
      lp['rwkv_ln_g'].reshape(1, RW), lp['rwkv_ln_b'].reshape(1, RW), lp['rwkv_r_k'].reshape(1, RW),
      lp['s5_d'].reshape(1, S5W), lp['s5_w_glu'].astype(BF16))


def _out_kernel(oc_ref, x_ref, w_ref, g1_ref, sc_ref, sh_ref, n2_ref, rt_ref, rb_ref,
                x1_ref, hp_ref, tw_ref, ti_ref):
    x1 = x_ref[...] + g1_ref[0] * _dot(oc_ref[...], w_ref[...])
    x1_ref[...] = x1
    y = x1 * lax.rsqrt(jnp.mean(x1 * x1, axis=-1, keepdims=True) + EPS)
    h2 = (y * n2_ref[...]) * (1.0 + sc_ref[0]) + sh_ref[0]
    hp_ref[...] = _pack_bf16_pair(h2[:, 0:D // 2], h2[:, D // 2:D])
    scores = _sigmoid(_dot_hi(h2, rt_ref[...]))
    sel = scores + rb_ref[...]
    lane = lax.broadcasted_iota(jnp.int32, sel.shape, 1)
    lane8 = lax.broadcasted_iota(jnp.int32, tw_ref.shape, 1)
    tw = jnp.zeros(tw_ref.shape, F32)
    ti = jnp.zeros(ti_ref.shape, jnp.int32)
    for k in range(TOPK):
        m = jnp.max(sel, axis=-1, keepdims=True)
        idx = jnp.min(jnp.where(sel == m, lane, NE), axis=-1, keepdims=True)
        hit = lane == idx
        wk = jnp.sum(jnp.where(hit, scores, 0.0), axis=-1, keepdims=True)
        tw = jnp.where(lane8 == k, wk, tw)
        ti = jnp.where(lane8 == k, idx, ti)
        sel = jnp.where(hit, -jnp.inf, sel)
    tw_ref[...] = tw / jnp.sum(tw, axis=-1, keepdims=True) * ROUTE_SCALE
    ti_ref[...] = ti


def _out_proj(o_cat, x, w_out_bf, mods6, norm2, router, moe_bias, seg):
    B, L, mod_base, mod_stride = seg
    tm = min(L, 256)
    nt = L // tm

    def mrow(j):
        return lambda i: ((mod_base + (i // nt) * mod_stride) * 6 + j, 0, 0)

    c2 = lambda i: (0, 0)
    n = B * L
    return pl.pallas_call(
        _out_kernel,
        grid=(n // tm,),
        in_specs=[pl.BlockSpec((tm, D), lambda i: (i, 0)),
                  pl.BlockSpec((tm, D), lambda i: (i, 0)),
                  pl.BlockSpec((D, D), c2),
                  pl.BlockSpec((1, 1, D), mrow(2)),
                  pl.BlockSpec((1, 1, D), mrow(4)),
                  pl.BlockSpec((1, 1, D), mrow(3)),
                  pl.BlockSpec((1, D), c2),
                  pl.BlockSpec((D, NE), c2),
                  pl.BlockSpec((1, NE), c2)],
        out_specs=[pl.BlockSpec((tm, D), lambda i: (i, 0)),
                   pl.BlockSpec((tm, D // 2), lambda i: (i, 0)),
                   pl.BlockSpec((tm, TOPK), lambda i: (i, 0)),
                   pl.BlockSpec((tm, TOPK), lambda i: (i, 0))],
        out_shape=[jax.ShapeDtypeStruct((n, D), F32),
                   jax.ShapeDtypeStruct((n, D // 2), U32),
                   jax.ShapeDtypeStruct((n, TOPK), F32),
                   jax.ShapeDtypeStruct((n, TOPK), jnp.int32)],
        compiler_params=_cp(("arbitrary",)),
        name="out_proj",
    )(o_cat, x, w_out_bf, mods6, mods6, mods6, norm2.reshape(1, D), router, moe_bias.reshape(1, NE))


def _moe_gather(tok_ref, x_hbm, xbuf, sem, slot):
    for r in range(MOE_BM):
        pltpu.make_async_copy(x_hbm.at[pl.ds(tok_ref[0, 0, r], 1)],
                              xbuf.at[slot, pl.ds(r, 1)], sem.at[slot]).start()


def _moe_kernel(be_ref, tokc_ref, tokn_ref, x_hbm, wgu_ref, wdn_ref, sw_ref, o_ref,
                xbuf, sem, wgu_bf, wdn_bf):
    i = pl.program_id(0)
    nb = pl.num_programs(0)
    slot = i % 2

    @pl.when(i == 0)
    def _():
        _moe_gather(tokc_ref, x_hbm, xbuf, sem, 0)

    @pl.when(i + 1 < nb)
    def _():
        _moe_gather(tokn_ref, x_hbm, xbuf, sem, 1 - slot)

    @pl.when(jnp.logical_or(i == 0, be_ref[i] != be_ref[jnp.maximum(i - 1, 0)]))
    def _():
        wgu_bf[...] = wgu_ref[0].astype(BF16)
        wdn_bf[...] = wdn_ref[0].astype(BF16)

    for r in range(MOE_BM):
        pltpu.make_async_copy(x_hbm.at[pl.ds(0, 1)], xbuf.at[slot, pl.ds(r, 1)], sem.at[slot]).wait()

    lo, hi = _unpack_bf16_pair(xbuf[slot])
    h = _dot(lo.astype(BF16), wgu_bf[0:D // 2, :]) + _dot(hi.astype(BF16), wgu_bf[D // 2:D, :])
    gte = h[:, 0:EDIM]
    act = (gte * _sigmoid(gte)) * h[:, EDIM:2 * EDIM]
    y = _dot(act.astype(BF16), wdn_bf[...]) * sw_ref[...]
    o_ref[...] = _pack_bf16_pair(y[:, 0:D // 2], y[:, D // 2:D])


def _moe_experts(block_e, slot_tok, slot_w, h2p, w_gu, w_dn):
    n_slots = slot_tok.shape[0]
    nb = n_slots // MOE_BM
    tok3 = slot_tok.reshape(nb, 1, MOE_BM)
    grid_spec = pltpu.PrefetchScalarGridSpec(
        num_scalar_prefetch=1,
        grid=(nb,),
        in_specs=[pl.BlockSpec((1, 1, MOE_BM), lambda i, be: (i, 0, 0), memory_space=pltpu.SMEM),
                  pl.BlockSpec((1, 1, MOE_BM), lambda i, be: (jnp.minimum(i + 1, nb - 1), 0, 0),
                               memory_space=pltpu.SMEM),
                  pl.BlockSpec(memory_space=pl.ANY),
                  pl.BlockSpec((1, D, 2 * EDIM), lambda i, be: (be[i], 0, 0)),
                  pl.BlockSpec((1, EDIM, D), lambda i, be: (be[i], 0, 0)),
                  pl.BlockSpec((MOE_BM, 1), lambda i, be: (i, 0))],
        out_specs=pl.BlockSpec((MOE_BM, D // 2), lambda i, be: (i, 0)),
        scratch_shapes=[pltpu.VMEM((2, MOE_BM, D // 2), U32),
                        pltpu.SemaphoreType.DMA((2,)),
                        pltpu.VMEM((D, 2 * EDIM), BF16),
                        pltpu.VMEM((EDIM, D), BF16)],
    )
    return pl.pallas_call(
        _moe_kernel,
        grid_spec=grid_spec,
        out_shape=jax.ShapeDtypeStruct((n_slots, D // 2), U32),
        compiler_params=_cp(("arbitrary",)),
        name="moe_experts",
    )(block_e, tok3, tok3, h2p, w_gu, w_dn, slot_w.reshape(n_slots, 1))


COMB_TM = 128


def _comb_gather(idx_ref, y_hbm, ybuf, sem, slot):
    for r in range(COMB_TM):
        for k in range(TOPK):
            pltpu.make_async_copy(y_hbm.at[pl.ds(idx_ref[0, 0, r * TOPK + k], 1)],
                                  ybuf.at[slot, k, pl.ds(r, 1)], sem.at[slot]).start()


def _comb_kernel(idxc_ref, idxn_ref, y_hbm, hp_ref, x1_ref, g2_ref, wgu_ref, wdn_ref, fn_ref,
                 o_ref, ybuf, sem, *, final):
    i = pl.program_id(0)
    nb = pl.num_programs(0)
    slot = i % 2

    @pl.when(i == 0)
    def _():
        _comb_gather(idxc_ref, y_hbm, ybuf, sem, 0)

    @pl.when(i + 1 < nb)
    def _():
        _comb_gather(idxn_ref, y_hbm, ybuf, sem, 1 - slot)

    lo, hi = _unpack_bf16_pair(hp_ref[...])
    h = _dot(lo.astype(BF16), wgu_ref[0:D // 2, :]) + _dot(hi.astype(BF16), wgu_ref[D // 2:D, :])
    gte = h[:, 0:EDIM]
    act = (gte * _sigmoid(gte)) * h[:, EDIM:2 * EDIM]
    shared = _dot(act.astype(BF16), wdn_ref[...])

    for r in range(COMB_TM):
        for k in range(TOPK):
            pltpu.make_async_copy(y_hbm.at[pl.ds(0, 1)], ybuf.at[slot, k, pl.ds(r, 1)],
                                  sem.at[slot]).wait()
    rlo = jnp.zeros((COMB_TM, D // 2), F32)
    rhi = jnp.zeros((COMB_TM, D // 2), F32)
    for k in range(TOPK):
        a, b = _unpack_bf16_pair(ybuf[slot, k])
        rlo = rlo + a
        rhi = rhi + b
    g2 = g2_ref[0]
    x1 = x1_ref[...]
    out_lo = x1[:, 0:D // 2] + g2[:, 0:D // 2] * (rlo + shared[:, 0:D // 2])
    out_hi = x1[:, D // 2:D] + g2[:, D // 2:D] * (rhi + shared[:, D // 2:D])
    if final:
        ms = (jnp.sum(out_lo * out_lo, axis=-1, keepdims=True)
              + jnp.sum(out_hi * out_hi, axis=-1, keepdims=True)) * (1.0 / D)
        inv = lax.rsqrt(ms + EPS)
        fn = fn_ref[...]
        out_lo = out_lo * inv * fn[:, 0:D // 2]
        out_hi = out_hi * inv * fn[:, D // 2:D]
    o_ref[:, 0:D // 2] = out_lo
    o_ref[:, D // 2:D] = out_hi


def _combine(slot_of, y_slots, h2p, x1, mods6, w_sgu_bf, w_sdn_bf, final_norm, seg, row0, final):
    B, L, mod_base, mod_stride = seg
    n = B * L
    tm = COMB_TM
    nb = n // tm
    nt = L // tm
    blk0 = row0 // tm
    idx3 = slot_of.reshape(-1, 1, tm * TOPK)
    c2 = lambda i: (0, 0)
    return pl.pallas_call(
        functools.partial(_comb_kernel, final=final),
        grid=(nb,),
        in_specs=[pl.BlockSpec((1, 1, tm * TOPK), lambda i: (blk0 + i, 0, 0), memory_space=pltpu.SMEM),
                  pl.BlockSpec((1, 1, tm * TOPK), lambda i: (blk0 + jnp.minimum(i + 1, nb - 1), 0, 0),
                               memory_space=pltpu.SMEM),
                  pl.BlockSpec(memory_space=pl.ANY),
                  pl.BlockSpec((tm, D // 2), lambda i: (blk0 + i, 0)),
                  pl.BlockSpec((tm, D), lambda i: (i, 0)),
                  pl.BlockSpec((1, 1, D), lambda i: ((mod_base + (i // nt) * mod_stride) * 6 + 5, 0, 0)),
                  pl.BlockSpec((D, 2 * EDIM), c2),
                  pl.BlockSpec((EDIM, D), c2),
                  pl.BlockSpec((1, D), c2)],
        out_specs=pl.BlockSpec((tm, D), lambda i: (i, 0)),
        out_shape=jax.ShapeDtypeStruct((n, D), F32),
        scratch_shapes=[pltpu.VMEM((2, TOPK, tm, D // 2), U32), pltpu.SemaphoreType.DMA((2,))],
        compiler_params=_cp(("arbitrary",)),
        name="moe_combine",
    )(idx3, idx3, y_slots, h2p, x1, mods6, w_sgu_bf, w_sdn_bf, final_norm.reshape(1, D))


def _routing_tables(top_i, top_w):
    n = top_i.shape[0]
    onehot = (top_i[:, :, None] == jnp.arange(NE, dtype=jnp.int32)[None, None, :])
    mask = jnp.any(onehot, axis=1).astype(jnp.int32)
    counts = jnp.sum(mask, axis=0)
    rank = jnp.cumsum(mask, axis=0) - mask
    padded = (counts + MOE_BM - 1) // MOE_BM * MOE_BM
    pad_end = jnp.cumsum(padded)
    pad_start = pad_end - padded
    slot_all = pad_start[None, :] + rank
    slot_of = jnp.take_along_axis(slot_all, top_i, axis=1).astype(jnp.int32)
    n_blocks = n * TOPK // MOE_BM + NE
    n_slots = n_blocks * MOE_BM
    tok = jnp.broadcast_to(jnp.arange(n, dtype=jnp.int32)[:, None], (n, TOPK))
    slot_tok = jnp.full((n_slots,), n, jnp.int32).at[slot_of.reshape(-1)].set(tok.reshape(-1))
    slot_w = jnp.zeros((n_slots,), F32).at[slot_of.reshape(-1)].set(top_w.reshape(-1))
    block_e = jnp.minimum(jnp.searchsorted(pad_end, jnp.arange(n_blocks) * MOE_BM, side='right'),
                          NE - 1).astype(jnp.int32)
    return slot_of, slot_tok, slot_w, block_e


def _rope_tables(n_tok):
    rows = n_tok // GRID_W
    row = jnp.repeat(jnp.arange(rows), GRID_W).astype(F32)
    col = jnp.tile(jnp.arange(GRID_W), rows).astype(F32)
    half = QK // 2
    inv = ROPE_THETA ** (-jnp.arange(0, half, 2, dtype=F32) / half)
    ang = jnp.concatenate([row[:, None] * inv, col[:, None] * inv], axis=-1)
    cos = jnp.repeat(jnp.cos(ang), 2, axis=-1)
    sin = jnp.repeat(jnp.sin(ang), 2, axis=-1)
    sign = jnp.tile(jnp.array([-1.0, 1.0], F32), QK // 2)
    return jnp.tile(cos, (1, 2)), jnp.tile(sin * sign, (1, 2))


def kernel(x_prompt, x_sample, cache_attn_k, cache_attn_v, state_rwkv, state_s5, c, c_ctx, w_mod, b_mod, norm1, norm2, w_in, w_out, att_lambda, att_subln, rwkv_mu, rwkv_w0, rwkv_w_up, rwkv_a0, rwkv_a_up, rwkv_g_up, rwkv_k_k, rwkv_k_a, rwkv_r_k, rwkv_ln_g, rwkv_ln_b, s5_lam_re, s5_lam_im, s5_log_step, s5_b_re, s5_b_im, s5_c_re, s5_c_im, s5_d, s5_w_glu, moe_router, moe_bias, moe_w_gate_up, moe_w_down, shared_w_gate_up, shared_w_down, final_norm):
    params = dict(w_mod=w_mod, b_mod=b_mod, norm1=norm1, norm2=norm2, w_in=w_in, w_out=w_out,
                  att_lambda=att_lambda, att_subln=att_subln,
                  rwkv_mu=rwkv_mu, rwkv_w0=rwkv_w0, rwkv_w_up=rwkv_w_up, rwkv_a0=rwkv_a0,
                  rwkv_a_up=rwkv_a_up, rwkv_g_up=rwkv_g_up, rwkv_k_k=rwkv_k_k, rwkv_k_a=rwkv_k_a,
                  rwkv_r_k=rwkv_r_k, rwkv_ln_g=rwkv_ln_g, rwkv_ln_b=rwkv_ln_b,
                  s5_lam_re=s5_lam_re, s5_lam_im=s5_lam_im, s5_log_step=s5_log_step,
                  s5_b_re=s5_b_re, s5_b_im=s5_b_im, s5_c_re=s5_c_re, s5_c_im=s5_c_im,
                  s5_d=s5_d, s5_w_glu=s5_w_glu,
                  moe_router=moe_router, moe_bias=moe_bias, moe_w_gate_up=moe_w_gate_up,
                  moe_w_down=moe_w_down, shared_w_gate_up=shared_w_gate_up,
                  shared_w_down=shared_w_down)
    bp, lp_len, _ = x_prompt.shape
    bs, ls_len, _ = x_sample.shape
    segs = ((bp, lp_len, 0, 0), (bs, ls_len, 1, 1))
    xs = [x_prompt.reshape(bp * lp_len, D), x_sample.reshape(bs * ls_len, D)]
    cpad = jnp.zeros((16, D), F32).at[0].set(c_ctx).at[1:1 + bs].set(c)
    new_k, new_v, new_r, new_s = [], [], [], []
    for l in range(DEPTH):
        lp = {name: arr[l] for name, arr in params.items()}
        lam_init = 0.8 - 0.6 * math.exp(-0.3 * l)
        ctxs = (None, (cache_attn_k[:, l], cache_attn_v[:, l], state_rwkv[:, l], state_s5[:, l]))
        xs, caches = _layer(xs, segs, cpad, lp, lam_init, ctxs, final_norm, l == DEPTH - 1)
        ck, cv, cr, cs = caches[0]
        new_k.append(ck)
        new_v.append(cv)
        new_r.append(cr)
        new_s.append(cs)
    return (xs[0].reshape(bp, lp_len, D), xs[1].reshape(bs, ls_len, D),
            jnp.stack(new_k, axis=1), jnp.stack(new_v, axis=1),
            jnp.stack(new_r, axis=1), jnp.stack(new_s, axis=1))


def _layer(xs, segs, cpad, lp, lam_init, ctxs, final_norm, final):
    hh = jnp.arange(RW) // QK
    segmat = (hh[:, None] == hh[None, :]).astype(F32)
    perm = jnp.concatenate([jnp.arange(0, 3072), jnp.arange(3072 + RW_PROJ, PROJ_W),
                            jnp.arange(3072, 3072 + RW_PROJ)])
    mods6 = _modulation(cpad, lp['w_mod'], lp['b_mod']).reshape(16 * 6, 1, D)
    w_in_bf = lp['w_in'][:, perm].astype(BF16)
    w_out_bf = lp['w_out'].astype(BF16)
    wb, wc, ab = _s5_params(lp)
    x1s, h2ps, tws, tis, caches = [], [], [], [], []
    for si, seg in enumerate(segs):
        B, L = seg[0], seg[1]
        proj = _in_proj(xs[si], lp['norm1'], mods6, w_in_bf, seg)
        proj3 = proj.reshape(L, B, PROJ_STRIDE)
        if ctxs[si] is None:
            o_att = _attention(proj, lp['att_lambda'], lp['att_subln'], lam_init, seg)
            s0_rwkv = None
            x0 = jnp.zeros((2, 2, B, S5S), F32)
        else:
            ck, cv, s0_rwkv, s0_s5 = ctxs[si]
            ctx = (ck.reshape(B, -1, ATT_W), cv.reshape(B, -1, ATT_W))
            o_att = _attention(proj, lp['att_lambda'], lp['att_subln'], lam_init, seg, ctx,
                               _rope_tables(L))
            x0 = s0_s5.astype(F32).reshape(B, 2, 2, S5S).transpose(1, 2, 0, 3)
        o_sum, shared, misc, s_fin = _rwkv_mix(proj, lp, segmat, seg, s0_rwkv)
        y_s5, xf = _s5_scan(proj3, wb, wc, ab, x0, seg)
        o_cat = _mix_post(o_att, o_sum, shared, misc, y_s5, proj, segmat, lp, seg)
        x1, h2p, tw, ti = _out_proj(o_cat, xs[si], w_out_bf, mods6, lp['norm2'],
                                    lp['moe_router'], lp['moe_bias'], seg)
        x1s.append(x1)
        h2ps.append(h2p)
        tws.append(tw)
        tis.append(ti)
        caches.append((proj3[:, :, COL_K:COL_K + ATT_W].transpose(1, 0, 2).reshape(B, L, HEADS, 2, QK),
                       proj3[:, :, COL_V:COL_V + ATT_W].transpose(1, 0, 2).reshape(B, L, HEADS, 2 * QK),
                       s_fin,
                       xf.transpose(2, 0, 1, 3).reshape(B, 2, 2, S5G, S5N)))
    h2p_all = jnp.concatenate(h2ps + [jnp.zeros((8, D // 2), U32)], axis=0)
    slot_of, slot_tok, slot_w, block_e = _routing_tables(jnp.concatenate(tis, axis=0),
                                                         jnp.concatenate(tws, axis=0))
    y_slots = _moe_experts(block_e, slot_tok, slot_w, h2p_all, lp['moe_w_gate_up'], lp['moe_w_down'])
    w_sgu_bf = lp['shared_w_gate_up'].astype(BF16)
    w_sdn_bf = lp['shared_w_down'].astype(BF16)
    outs = []
    row0 = 0
    for si, seg in enumerate(segs):
        outs.append(_combine(slot_of, y_slots, h2p_all, x1s[si], mods6, w_sgu_bf, w_sdn_bf,
                             final_norm, seg, row0, final))
        row0 += seg[0] * seg[1]
    return outs, caches
```

```python
import functools
import math

import jax
import jax.numpy as jnp
from jax import lax
from jax.experimental import pallas as pl
from jax.experimental.pallas import tpu as pltpu

F32 = jnp.float32
BF16 = jnp.bfloat16
U32 = jnp.uint32
HI = lax.Precision.HIGHEST

D = 2048
DEPTH = 2
EPS = 1e-6
GRID_W = 64
ROPE_THETA = 10000.0
HEADS = 8
QK = 64
ATT_W = 1024
RW = 512
RW_PROJ = 1792
S5W = 512
S5G = 32
S5N = 64
S5C = 16
S5S = S5G * S5N
GN_EPS = 64e-5
NE = 64
TOPK = 8
EDIM = 512
ROUTE_SCALE = 2.5
PROJ_W = 5376
PROJ_STRIDE = 7168
COL_Q, COL_K, COL_V, COL_U, COL_Z = 0, 1024, 2048, 3072, 3584
PROJ_TN = 1792
MOE_BM = 256
VMEM_LIMIT = 56 * 1024 * 1024


def _cp(sem):
    return pltpu.CompilerParams(dimension_semantics=sem, vmem_limit_bytes=VMEM_LIMIT)


def _sigmoid(x):
    return 1.0 / (1.0 + jnp.exp(-x))


def _dot(a, b):
    return jnp.dot(a, b, preferred_element_type=F32)


def _dot_hi(a, b):
    return jnp.dot(a, b, precision=HI, preferred_element_type=F32)


def _pack_bf16_pair(lo, hi):
    lo_b = lax.bitcast_convert_type(lo.astype(BF16).astype(F32), U32) >> 16
    hi_b = lax.bitcast_convert_type(hi.astype(BF16).astype(F32), U32) & jnp.uint32(0xFFFF0000)
    return lo_b | hi_b


def _unpack_bf16_pair(p):
    lo = lax.bitcast_convert_type(p << 16, F32)
    hi = lax.bitcast_convert_type(p & jnp.uint32(0xFFFF0000), F32)
    return lo, hi


def _mod_kernel(c_ref, w_ref, b_ref, o_ref):
    c = c_ref[...]
    s = c * _sigmoid(c)
    o_ref[...] = _dot(s.astype(BF16), w_ref[0].astype(BF16)) + b_ref[...]


def _modulation(cpad, w_mod_all, b_mod, layer):
    tn = 1024
    return pl.pallas_call(
        _mod_kernel,
        grid=(6 * D // tn,),
        in_specs=[pl.BlockSpec((16, D), lambda j: (0, 0)),
                  pl.BlockSpec((1, D, tn), lambda j: (layer, 0, j)),
                  pl.BlockSpec((1, tn), lambda j: (0, j))],
        out_specs=pl.BlockSpec((16, tn), lambda j: (0, j)),
        out_shape=jax.ShapeDtypeStruct((16, 6 * D), F32),
        compiler_params=_cp(("arbitrary",)),
        name="modulation",
    )(cpad, w_mod_all, b_mod.reshape(1, 6 * D))


def _in_kernel(x_ref, g_ref, sc_ref, sh_ref, w_ref, o_ref, h_ref):
    @pl.when(pl.program_id(2) == 0)
    def _():
        x = x_ref[...]
        y = x * lax.rsqrt(jnp.mean(x * x, axis=-1, keepdims=True) + EPS)
        h = (y * g_ref[...]) * (1.0 + sc_ref[0]) + sh_ref[0]
        h_ref[...] = h.astype(BF16)

    o_ref[...] = _dot(h_ref[...], w_ref[...])


def _in_proj(x, norm_g, mods6, w_in_bf, seg):
    B, L, mod_base, mod_stride = seg
    tm = min(L, 512)
    nt = L // tm
    ncol = PROJ_W // PROJ_TN
    nstride = PROJ_STRIDE // PROJ_TN

    def mrow(j):
        return lambda b, t, c: ((mod_base + b * mod_stride) * 6 + j, 0, 0)

    return pl.pallas_call(
        _in_kernel,
        grid=(B, nt, ncol),
        in_specs=[pl.BlockSpec((tm, D), lambda b, t, c: (b * nt + t, 0)),
                  pl.BlockSpec((1, D), lambda b, t, c: (0, 0)),
                  pl.BlockSpec((1, 1, D), mrow(1)),
                  pl.BlockSpec((1, 1, D), mrow(0)),
                  pl.BlockSpec((D, PROJ_TN), lambda b, t, c: (0, c))],
        out_specs=pl.BlockSpec((tm, PROJ_TN), lambda b, t, c: (t, b * nstride + c)),
        out_shape=jax.ShapeDtypeStruct((L, B * PROJ_STRIDE), F32),
        scratch_shapes=[pltpu.VMEM((tm, D), BF16)],
        compiler_params=_cp(("arbitrary", "arbitrary", "arbitrary")),
        name="in_proj",
    )(x, norm_g.reshape(1, D), mods6, mods6, w_in_bf)


def _rope(x, c, s):
    lane = lax.broadcasted_iota(jnp.int32, x.shape, 1)
    nxt = pltpu.roll(x, 127, 1)
    prv = pltpu.roll(x, 1, 1)
    swapped = jnp.where((lane & 1) == 0, nxt, prv)
    return x * c + swapped * s


def _attn_kernel(*refs, lam_init, n_ctx, rope):
    if rope:
        (q_ref, k_ref, v_ref, ck_ref, cv_ref, cq_ref, sq_ref, ckk_ref, skk_ref,
         lam_ref, g_ref, o_ref, kall_ref, vall_ref) = refs
    else:
        q_ref, k_ref, v_ref, lam_ref, g_ref, o_ref, kall_ref, vall_ref = refs

    @pl.when(pl.program_id(2) == 0)
    def _():
        k = k_ref[...]
        if rope:
            k = _rope(k, ckk_ref[...], skk_ref[...])
            kall_ref[0:n_ctx, :] = ck_ref[0].astype(BF16)
            vall_ref[0:n_ctx, :] = cv_ref[0].astype(BF16)
        kall_ref[n_ctx:, :] = k.astype(BF16)
        vall_ref[n_ctx:, :] = v_ref[...].astype(BF16)

    q = q_ref[...]
    if rope:
        q = _rope(q, cq_ref[...], sq_ref[...])
    q = q * (QK ** -0.5)
    lane = lax.broadcasted_iota(jnp.int32, q.shape, 1)
    kall = kall_ref[...]
    vall = vall_ref[...]
    outs = []
    for m in range(2):
        qm = jnp.where((lane < QK) == (m == 0), q, 0.0).astype(BF16)
        s = lax.dot_general(qm, kall, (((1,), (1,)), ((), ())), preferred_element_type=F32)
        mx = jnp.max(s, axis=-1, keepdims=True)
        e = jnp.exp(s - mx)
        l = jnp.sum(e, axis=-1, keepdims=True)
        outs.append(_dot(e.astype(BF16), vall) / l)
    lv = lam_ref[...]
    lam = (jnp.exp(jnp.sum(lv[0:1] * lv[1:2], axis=-1, keepdims=True))
           - jnp.exp(jnp.sum(lv[2:3] * lv[3:4], axis=-1, keepdims=True)) + lam_init)
    o = outs[0] - lam * outs[1]
    o = o * lax.rsqrt(jnp.mean(o * o, axis=-1, keepdims=True) + EPS) * g_ref[...]
    o_ref[...] = o * (1.0 - lam_init)


def _attention(proj, att_lambda, subln, lam_init, seg, ctx=None, tables=None):
    B, L = seg[0], seg[1]
    tq = min(L, 512)
    nq = L // tq
    cs = PROJ_STRIDE // 128
    rope = ctx is not None
    n_ctx = ctx[0].shape[1] if rope else 0
    specs = [pl.BlockSpec((tq, 128), lambda b, h, t: (t, b * cs + COL_Q // 128 + h)),
             pl.BlockSpec((L, 128), lambda b, h, t: (0, b * cs + COL_K // 128 + h)),
             pl.BlockSpec((L, 128), lambda b, h, t: (0, b * cs + COL_V // 128 + h))]
    args = [proj, proj, proj]
    if rope:
        cos_t, sin_t = tables
        specs += [pl.BlockSpec((1, n_ctx, 128), lambda b, h, t: (b, 0, h)),
                  pl.BlockSpec((1, n_ctx, 128), lambda b, h, t: (b, 0, h)),
                  pl.BlockSpec((tq, 128), lambda b, h, t: (t, 0)),
                  pl.BlockSpec((tq, 128), lambda b, h, t: (t, 0)),
                  pl.BlockSpec((L, 128), lambda b, h, t: (0, 0)),
                  pl.BlockSpec((L, 128), lambda b, h, t: (0, 0))]
        args += [ctx[0], ctx[1], cos_t, sin_t, cos_t, sin_t]
    specs += [pl.BlockSpec((4, QK), lambda b, h, t: (0, 0)),
              pl.BlockSpec((1, 128), lambda b, h, t: (0, 0))]
    args += [att_lambda, subln.reshape(1, 128)]
    return pl.pallas_call(
        functools.partial(_attn_kernel, lam_init=lam_init, n_ctx=n_ctx, rope=rope),
        grid=(B, HEADS, nq),
        in_specs=specs,
        out_specs=pl.BlockSpec((tq, 128), lambda b, h, t: (b * nq + t, h)),
        out_shape=jax.ShapeDtypeStruct((B * L, ATT_W), F32),
        scratch_shapes=[pltpu.VMEM((n_ctx + L, 128), BF16), pltpu.VMEM((n_ctx + L, 128), BF16)],
        compiler_params=_cp(("arbitrary", "arbitrary", "arbitrary")),
        name="diff_attention",
    )(*args)


def _rwkv_pre_kernel(z_ref, zp_ref, zn_ref, mu_ref, seg_ref, wup_ref, aup_ref, gup_ref,
                     w0_ref, a0_ref, kk_ref, ka_ref, sh_ref, pd_ref, ms_ref, *, tm):
    t = pl.program_id(1)
    nt = pl.num_programs(1)
    z = z_ref[...]
    row = lax.broadcasted_iota(jnp.int32, z.shape, 0)
    prev_row = jnp.where(t > 0, zp_ref[7:8, :], 0.0)
    next_row = jnp.where(t < nt - 1, zn_ref[0:1, :], 0.0)
    zp = jnp.where(row == 0, prev_row, pltpu.roll(z, 1, 0))
    zn = jnp.where(row == tm - 1, next_row, pltpu.roll(z, tm - 1, 0))
    zs = z + mu_ref[...] * (0.5 * (zp + zn) - z)
    r = zs[:, 0:RW]
    k = zs[:, RW:2 * RW]
    v = zs[:, 2 * RW:3 * RW]
    wa = zs[:, 3 * RW:3 * RW + 128]
    gl = zs[:, 3 * RW + 128:3 * RW + 256]
    lane = lax.broadcasted_iota(jnp.int32, wa.shape, 1)
    wa = jnp.where(lane < 64, jnp.tanh(wa), wa)
    g = _dot_hi(_sigmoid(gl), gup_ref[...])
    kk = k * kk_ref[...]
    kk = kk * lax.rsqrt(_dot_hi(kk * kk, seg_ref[...]) + EPS)
    sh_ref[0] = r
    sh_ref[1] = v
    sh_ref[2] = kk
    kb = jnp.zeros_like(k)
    for d in range(2):
        xw = w0_ref[d] + _dot_hi(wa, wup_ref[d])
        w = jnp.exp(-math.exp(-0.5) * _sigmoid(xw))
        a = _sigmoid(a0_ref[d] + _dot_hi(wa, aup_ref[d]))
        kd = k * (1.0 + (a - 1.0) * ka_ref[...])
        pd_ref[0, d] = w
        pd_ref[1, d] = kd
        pd_ref[2, d] = kk * a
        kb = kb + kd
    ms_ref[0] = g
    ms_ref[1] = kb


def _rwkv_pre(proj, lp, segmat, seg):
    B, L = seg[0], seg[1]
    tm = min(L, 256)
    nt = L // tm
    zs = PROJ_STRIDE // RW_PROJ
    zc = COL_Z // RW_PROJ
    nb8 = L // 8
    zeros64 = jnp.zeros((2, 64, RW), F32)
    wup = jnp.concatenate([lp['rwkv_w_up'], zeros64], axis=1)
    aup = jnp.concatenate([zeros64, lp['rwkv_a_up']], axis=1)
    c2 = lambda b, t: (0, 0)
    c3 = lambda b, t: (0, 0, 0)
    return pl.pallas_call(
        functools.partial(_rwkv_pre_kernel, tm=tm),
        grid=(B, nt),
        in_specs=[pl.BlockSpec((tm, RW_PROJ), lambda b, t: (t, b * zs + zc)),
                  pl.BlockSpec((8, RW_PROJ), lambda b, t: (jnp.maximum(t * (tm // 8) - 1, 0), b * zs + zc)),
                  pl.BlockSpec((8, RW_PROJ), lambda b, t: (jnp.minimum((t + 1) * (tm // 8), nb8 - 1), b * zs + zc)),
                  pl.BlockSpec((1, RW_PROJ), c2),
                  pl.BlockSpec((RW, RW), c2),
                  pl.BlockSpec((2, 128, RW), c3),
                  pl.BlockSpec((2, 128, RW), c3),
                  pl.BlockSpec((128, RW), c2),
                  pl.BlockSpec((2, 1, RW), c3),
                  pl.BlockSpec((2, 1, RW), c3),
                  pl.BlockSpec((1, RW), c2),
                  pl.BlockSpec((1, RW), c2)],
        out_specs=[pl.BlockSpec((3, tm, RW), lambda b, t: (0, t, b)),
                   pl.BlockSpec((3, 2, tm, RW), lambda b, t: (0, 0, t, b)),
                   pl.BlockSpec((2, tm, RW), lambda b, t: (0, t, b))],
        out_shape=[jax.ShapeDtypeStruct((3, L, B * RW), F32),
                   jax.ShapeDtypeStruct((3, 2, L, B * RW), F32),
                   jax.ShapeDtypeStruct((2, L, B * RW), F32)],
        compiler_params=_cp(("arbitrary", "arbitrary")),
        name="rwkv_pre",
    )(proj, proj, proj, lp['rwkv_mu'].reshape(1, RW_PROJ), segmat, wup, aup, lp['rwkv_g_up'],
      lp['rwkv_w0'].reshape(2, 1, RW), lp['rwkv_a0'].reshape(2, 1, RW),
      lp['rwkv_k_k'].reshape(1, RW), lp['rwkv_k_a'].reshape(1, RW))


def _wkv_kernel(x_ref, s0_ref, y_ref, sf_ref, st_ref, *, tc):
    c = pl.program_id(1)

    @pl.when(c == 0)
    def _():
        st_ref[...] = s0_ref[...]

    def step(t, carry):
        kk_t = x_ref[2, t]
        w_t = x_ref[3, t]
        kd_t = x_ref[4, t]
        b_t = x_ref[5, t]
        r_t = x_ref[0, t]
        wr = w_t * r_t
        kr = jnp.sum(kd_t * r_t, axis=0, keepdims=True)
        br = jnp.sum(b_t * r_t, axis=0, keepdims=True)

        def vgroup(g, carry2):
            base = pl.multiple_of(g * 8, 8)
            vtile = x_ref[1, t, pl.ds(base, 8), :]
            ys = []
            for j in range(8):
                s = st_ref[base + j]
                sa = jnp.sum(s * kk_t, axis=0, keepdims=True)
                y0 = jnp.sum(s * wr, axis=0, keepdims=True)
                vv = vtile[j:j + 1]
                st_ref[base + j] = s * w_t + (vv * kd_t - sa * b_t)
                ys.append(y0 + vv * kr - sa * br)
            y_ref[t, pl.ds(base, 8), :] = jnp.concatenate(ys, axis=0)
            return carry2

        return lax.fori_loop(0, QK // 8, vgroup, carry)

    lax.fori_loop(0, tc, step, 0)

    @pl.when(c == pl.num_programs(1) - 1)
    def _():
        sf_ref[...] = st_ref[...]


def _wkv_scan(xs, s0):
    _, L, _, lanes = xs.shape
    tc = 32
    return pl.pallas_call(
        functools.partial(_wkv_kernel, tc=tc),
        grid=(lanes // 128, L // tc),
        in_specs=[pl.BlockSpec((6, tc, QK, 128), lambda g, c: (0, c, 0, g)),
                  pl.BlockSpec((QK, QK, 128), lambda g, c: (0, 0, g))],
        out_specs=[pl.BlockSpec((tc, QK, 128), lambda g, c: (c, 0, g)),
                   pl.BlockSpec((QK, QK, 128), lambda g, c: (0, 0, g))],
        out_shape=[jax.ShapeDtypeStruct((L, QK, lanes), F32),
                   jax.ShapeDtypeStruct((QK, QK, lanes), F32)],
        scratch_shapes=[pltpu.VMEM((QK, QK, 128), F32)],
        compiler_params=_cp(("arbitrary", "arbitrary")),
        name="wkv7_scan",
    )(xs, s0)


def _rwkv_mix(proj, lp, segmat, seg, s0_bdhvk):
    B, L = seg[0], seg[1]
    shared, perdir, misc = _rwkv_pre(proj, lp, segmat, seg)
    sh = shared.reshape(3, L, B, HEADS, QK)
    pd = perdir.reshape(3, 2, L, B, HEADS, QK)
    full = jnp.concatenate(
        [jnp.stack([sh, sh[:, ::-1]], axis=1),
         jnp.concatenate([pd[:, 0:1], pd[:, 1:2, ::-1]], axis=1)], axis=0)
    xs = full.transpose(0, 2, 5, 1, 3, 4).reshape(6, L, QK, 2 * B * HEADS)
    lanes = 2 * B * HEADS
    if s0_bdhvk is None:
        s0 = jnp.zeros((QK, QK, lanes), F32)
    else:
        s0 = s0_bdhvk.astype(F32).transpose(3, 4, 1, 0, 2).reshape(QK, QK, lanes)
    y, sf = _wkv_scan(xs, s0)
    y = y.reshape(L, QK, 2, B, HEADS).transpose(2, 0, 3, 4, 1)
    o_sum = (y[0] + y[1, ::-1]).reshape(L, B * RW)
    s_fin = sf.reshape(QK, QK, 2, B, HEADS).transpose(3, 2, 4, 0, 1)
    return o_sum, shared, misc, s_fin


def _s5_kernel(u_ref, wb_ref, wc_ref, ab_ref, x0_ref, y_ref, xf_ref, bx_ref, st_ref, *, tc, nb):
    d = pl.program_id(0)
    c = pl.program_id(1)

    @pl.when(c == 0)
    def _():
        st_ref[...] = x0_ref[0]

    u = u_ref[...].reshape(tc * nb, S5W).astype(BF16)
    bx_ref[...] = _dot(u, wb_ref[0])
    ar = jnp.broadcast_to(ab_ref[0, 0:1, :], (nb, S5S))
    ai = jnp.broadcast_to(ab_ref[0, 1:2, :], (nb, S5S))

    def step(i, carry):
        tt = jnp.where(d == 0, i, tc - 1 - i)
        rows = pl.ds(pl.multiple_of(tt * nb, nb), nb)
        xr = st_ref[0]
        xi = st_ref[1]
        nr = ar * xr - ai * xi + bx_ref[rows, 0:S5S]
        ni = ar * xi + ai * xr + bx_ref[rows, S5S:2 * S5S]
        st_ref[0] = nr
        st_ref[1] = ni
        bx_ref[rows, 0:S5S] = nr
        bx_ref[rows, S5S:2 * S5S] = ni
        return carry

    lax.fori_loop(0, tc, step, 0)
    y = _dot(bx_ref[...].astype(BF16), wc_ref[...])
    y_ref[0] = y.reshape(tc, nb, S5W)

    @pl.when(c == pl.num_programs(1) - 1)
    def _():
        xf_ref[0] = st_ref[...]


def _s5_scan(proj3, wb, wc, ab, x0, seg):
    B, L = seg[0], seg[1]
    tc = 64 if B <= 8 else 16
    nc = L // tc
    tmap = lambda d, c: jnp.where(d == 0, c, nc - 1 - c)
    return pl.pallas_call(
        functools.partial(_s5_kernel, tc=tc, nb=B),
        grid=(2, nc),
        in_specs=[pl.BlockSpec((tc, B, S5W), lambda d, c: (tmap(d, c), 0, COL_U // S5W)),
                  pl.BlockSpec((1, S5W, 2 * S5S), lambda d, c: (d, 0, 0)),
                  pl.BlockSpec((2 * S5S, S5W), lambda d, c: (0, 0)),
                  pl.BlockSpec((1, 2, S5S), lambda d, c: (d, 0, 0)),
                  pl.BlockSpec((1, 2, B, S5S), lambda d, c: (d, 0, 0, 0))],
        out_specs=[pl.BlockSpec((1, tc, B, S5W), lambda d, c: (d, tmap(d, c), 0, 0)),
                   pl.BlockSpec((1, 2, B, S5S), lambda d, c: (d, 0, 0, 0))],
        out_shape=[jax.ShapeDtypeStruct((2, L, B, S5W), F32),
                   jax.ShapeDtypeStruct((2, 2, B, S5S), F32)],
        scratch_shapes=[pltpu.VMEM((tc * B, 2 * S5S), F32), pltpu.VMEM((2, B, S5S), F32)],
        compiler_params=_cp(("arbitrary", "arbitrary")),
        name="s5_scan",
    )(proj3, wb, wc, ab, x0)


def _s5_params(lp):
    lam_re, lam_im = lp['s5_lam_re'], lp['s5_lam_im']
    dt = jnp.exp(lp['s5_log_step'])[:, :, None]
    mag = jnp.exp(lam_re * dt)
    ab_re, ab_im = mag * jnp.cos(lam_im * dt), mag * jnp.sin(lam_im * dt)
    den = lam_re * lam_re + lam_im * lam_im
    f_re = ((ab_re - 1.0) * lam_re + ab_im * lam_im) / den
    f_im = (ab_im * lam_re - (ab_re - 1.0) * lam_im) / den
    b_re, b_im = lp['s5_b_re'], lp['s5_b_im']
    wre = f_re[..., None] * b_re[None] - f_im[..., None] * b_im[None]
    wim = f_re[..., None] * b_im[None] + f_im[..., None] * b_re[None]
    eye = jnp.eye(S5G, dtype=F32)

    def block_in(w):
        return jnp.einsum('dgnc,gh->dgchn', w, eye).reshape(2, S5W, S5S)

    wb = jnp.concatenate([block_in(wre), block_in(wim)], axis=-1).astype(BF16)

    def block_out(cm):
        return jnp.einsum('gcn,gh->gnhc', cm, eye).reshape(S5S, S5W)

    wc = jnp.concatenate([block_out(lp['s5_c_re']), -block_out(lp['s5_c_im'])], axis=0).astype(BF16)
    ab = jnp.stack([ab_re.reshape(2, S5S), ab_im.reshape(2, S5S)], axis=1)
    return wb, wc, ab


def _mix_kernel(oatt_ref, osum_ref, sh_ref, ms_ref, y_ref, u_ref, seg_ref, lng_ref, lnb_ref,
                rk_ref, d_ref, wglu_ref, o_ref):
    segm = seg_ref[...]
    o = osum_ref[...]
    mean = _dot_hi(o, segm) * (1.0 / QK)
    oc = o - mean
    var = _dot_hi(oc * oc, segm) * (1.0 / QK)
    o_n = oc * lax.rsqrt(var + GN_EPS) * lng_ref[...] + lnb_ref[...]
    r = sh_ref[0]
    v = sh_ref[1]
    bonus = _dot_hi(r * 0.5 * ms_ref[1] * rk_ref[...], segm) * v
    rw = (o_n + bonus) * ms_ref[0]
    u = u_ref[...]
    y = d_ref[...] * u + y_ref[0] + y_ref[1]
    hg = 0.5 * y * (1.0 + jnp.tanh(math.sqrt(2.0 / math.pi) * (y + 0.044715 * (y * y * y))))
    s5 = hg * _sigmoid(_dot(hg.astype(BF16), wglu_ref[...]))
    o_ref[:, 0:ATT_W] = oatt_ref[...].astype(BF16)
    o_ref[:, ATT_W:ATT_W + RW] = rw.astype(BF16)
    o_ref[:, ATT_W + RW:D] = s5.astype(BF16)


def _mix_post(o_att, o_sum, shared, misc, y_s5, proj, segmat, lp, seg):
    B, L = seg[0], seg[1]
    tm = min(L, 256)
    nt = L // tm
    us = PROJ_STRIDE // S5W
    c2 = lambda b, t: (0, 0)
    return pl.pallas_call(
        _mix_kernel,
        grid=(B, nt),
        in_specs=[pl.BlockSpec((tm, ATT_W), lambda b, t: (b * nt + t, 0)),
                  pl.BlockSpec((tm, RW), lambda b, t: (t, b)),
                  pl.BlockSpec((3, tm, RW), lambda b, t: (0, t, b)),
                  pl.BlockSpec((2, tm, RW), lambda b, t: (0, t, b)),
                  pl.BlockSpec((2, tm, S5W), lambda b, t: (0, t, b)),
                  pl.BlockSpec((tm, S5W), lambda b, t: (t, b * us + COL_U // S5W)),
                  pl.BlockSpec((RW, RW), c2),
                  pl.BlockSpec((1, RW), c2),
                  pl.BlockSpec((1, RW), c2),
                  pl.BlockSpec((1, RW), c2),
                  pl.BlockSpec((1, S5W), c2),
                  pl.BlockSpec((S5W, S5W), c2)],
        out_specs=pl.BlockSpec((tm, D), lambda b, t: (b * nt + t, 0)),
        out_shape=jax.ShapeDtypeStruct((B * L, D), BF16),
        compiler_params=_cp(("arbitrary", "arbitrary")),
        name="mix_post",
    )(o_att, o_sum, shared, misc, y_s5.reshape(2, L, B * S5W), proj, segmat,
      lp['rwkv_ln_g'].reshape(1, RW), lp['rwkv_ln_b'].reshape(1, RW), lp['rwkv_r_k'].reshape(1, RW),
      lp['s5_d'].reshape(1, S5W), lp['s5_w_glu'].astype(BF16))


def _out_kernel(oc_ref, x_ref, w_ref, g1_ref, sc_ref, sh_ref, n2_ref, rt_ref, rb_ref,
                x1_ref, hp_ref, tw_ref, ti_ref):
    x1 = x_ref[...] + g1_ref[0] * _dot(oc_ref[...], w_ref[...])
    x1_ref[...] = x1
    y = x1 * lax.rsqrt(jnp.mean(x1 * x1, axis=-1, keepdims=True) + EPS)
    h2 = (y * n2_ref[...]) * (1.0 + sc_ref[0]) + sh_ref[0]
    hp_ref[...] = _pack_bf16_pair(h2[:, 0:D // 2], h2[:, D // 2:D])
    scores = _sigmoid(_dot_hi(h2, rt_ref[...]))
    sel = scores + rb_ref[...]
    lane = lax.broadcasted_iota(jnp.int32, sel.shape, 1)
    lane8 = lax.broadcasted_iota(jnp.int32, tw_ref.shape, 1)
    tw = jnp.zeros(tw_ref.shape, F32)
    ti = jnp.zeros(ti_ref.shape, jnp.int32)
    for k in range(TOPK):
        m = jnp.max(sel, axis=-1, keepdims=True)
        idx = jnp.min(jnp.where(sel == m, lane, NE), axis=-1, keepdims=True)
        hit = lane == idx
        wk = jnp.sum(jnp.where(hit, scores, 0.0), axis=-1, keepdims=True)
        tw = jnp.where(lane8 == k, wk, tw)
        ti = jnp.where(lane8 == k, idx, ti)
        sel = jnp.where(hit, -jnp.inf, sel)
    tw_ref[...] = tw / jnp.sum(tw, axis=-1, keepdims=True) * ROUTE_SCALE
    ti_ref[...] = ti


def _out_proj(o_cat, x, w_out_bf, mods6, norm2, router, moe_bias, seg):
    B, L, mod_base, mod_stride = seg
    tm = min(L, 256)
    nt = L // tm

    def mrow(j):
        return lambda i: ((mod_base + (i // nt) * mod_stride) * 6 + j, 0, 0)

    c2 = lambda i: (0, 0)
    n = B * L
    return pl.pallas_call(
        _out_kernel,
        grid=(n // tm,),
        in_specs=[pl.BlockSpec((tm, D), lambda i: (i, 0)),
                  pl.BlockSpec((tm, D), lambda i: (i, 0)),
                  pl.BlockSpec((D, D), c2),
                  pl.BlockSpec((1, 1, D), mrow(2)),
                  pl.BlockSpec((1, 1, D), mrow(4)),
                  pl.BlockSpec((1, 1, D), mrow(3)),
                  pl.BlockSpec((1, D), c2),
                  pl.BlockSpec((D, NE), c2),
                  pl.BlockSpec((1, NE), c2)],
        out_specs=[pl.BlockSpec((tm, D), lambda i: (i, 0)),
                   pl.BlockSpec((tm, D // 2), lambda i: (i, 0)),
                   pl.BlockSpec((tm, TOPK), lambda i: (i, 0)),
                   pl.BlockSpec((tm, TOPK), lambda i: (i, 0))],
        out_shape=[jax.ShapeDtypeStruct((n, D), F32),
                   jax.ShapeDtypeStruct((n, D // 2), U32),
                   jax.ShapeDtypeStruct((n, TOPK), F32),
                   jax.ShapeDtypeStruct((n, TOPK), jnp.int32)],
        compiler_params=_cp(("arbitrary",)),
        name="out_proj",
    )(o_cat, x, w_out_bf, mods6, mods6, mods6, norm2.reshape(1, D), router, moe_bias.reshape(1, NE))


DISP_TM = 128


def _dispatch_copy(hp_ref, xs_ref, sem, r, dst_row):
    return pltpu.make_async_copy(hp_ref.at[pl.ds(r, 1)], xs_ref.at[pl.ds(dst_row, 1)], sem.at[0])


def _dispatch_kernel(idx_ref, hp_ref, xs_in_ref, xs_ref, sem):
    del xs_in_ref
    for r in range(DISP_TM):
        for k in range(TOPK):
            _dispatch_copy(hp_ref, xs_ref, sem, r, idx_ref[0, 0, r * TOPK + k]).start()
    for r in range(DISP_TM):
        for k in range(TOPK):
            _dispatch_copy(hp_ref, xs_ref, sem, r, 0).wait()


def _moe_dispatch(slot_of, h2p, xs_init):
    n = slot_of.shape[0]
    tm = DISP_TM
    idx3 = slot_of.reshape(n // tm, 1, tm * TOPK)
    return pl.pallas_call(
        _dispatch_kernel,
        grid=(n // tm,),
        in_specs=[pl.BlockSpec((1, 1, tm * TOPK), lambda i: (i, 0, 0), memory_space=pltpu.SMEM),
                  pl.BlockSpec((tm, D // 2), lambda i: (i, 0)),
                  pl.BlockSpec(memory_space=pl.ANY)],
        out_specs=pl.BlockSpec(memory_space=pl.ANY),
        out_shape=jax.ShapeDtypeStruct(xs_init.shape, U32),
        scratch_shapes=[pltpu.SemaphoreType.DMA((1,))],
        input_output_aliases={2: 0},
        compiler_params=_cp(("arbitrary",)),
        name="moe_dispatch",
    )(idx3, h2p, xs_init)


def _moe_kernel(be_ref, nu_ref, x_ref, wgu_ref, wdn_ref, o_ref, wgu_bf, wdn_bf):
    i = pl.program_id(0)

    @pl.when(i < nu_ref[0])
    def _():
        @pl.when(jnp.logical_or(i == 0, be_ref[i] != be_ref[jnp.maximum(i - 1, 0)]))
        def _():
            wgu_bf[...] = wgu_ref[0, 0].astype(BF16)
            wdn_bf[...] = wdn_ref[0, 0].astype(BF16)

        lo, hi = _unpack_bf16_pair(x_ref[...])
        h = _dot(lo.astype(BF16), wgu_bf[0:D // 2, :]) + _dot(hi.astype(BF16), wgu_bf[D // 2:D, :])
        gte = h[:, 0:EDIM]
        act = (gte * _sigmoid(gte)) * h[:, EDIM:2 * EDIM]
        y = _dot(act.astype(BF16), wdn_bf[...])
        o_ref[...] = _pack_bf16_pair(y[:, 0:D // 2], y[:, D // 2:D])

    @pl.when(i >= nu_ref[0])
    def _():
        o_ref[...] = jnp.zeros(o_ref.shape, U32)


def _moe_experts(block_e, n_used, x_sorted, w_gu_all, w_dn_all, layer):
    n_slots = x_sorted.shape[0]
    nb = n_slots // MOE_BM
    grid_spec = pltpu.PrefetchScalarGridSpec(
        num_scalar_prefetch=2,
        grid=(nb,),
        in_specs=[pl.BlockSpec((MOE_BM, D // 2), lambda i, be, nu: (i, 0)),
                  pl.BlockSpec((1, 1, D, 2 * EDIM), lambda i, be, nu: (layer, be[i], 0, 0)),
                  pl.BlockSpec((1, 1, EDIM, D), lambda i, be, nu: (layer, be[i], 0, 0))],
        out_specs=pl.BlockSpec((MOE_BM, D // 2), lambda i, be, nu: (i, 0)),
        scratch_shapes=[pltpu.VMEM((D, 2 * EDIM), BF16),
                        pltpu.VMEM((EDIM, D), BF16)],
    )
    return pl.pallas_call(
        _moe_kernel,
        grid_spec=grid_spec,
        out_shape=jax.ShapeDtypeStruct((n_slots, D // 2), U32),
        compiler_params=_cp(("arbitrary",)),
        name="moe_experts",
    )(block_e, n_used, x_sorted, w_gu_all, w_dn_all)


COMB_TM = 128


def _comb_gather(idx_ref, y_hbm, ybuf, sem, slot):
    for r in range(COMB_TM):
        for k in range(TOPK):
            pltpu.make_async_copy(y_hbm.at[pl.ds(idx_ref[0, 0, r * TOPK + k], 1)],
                                  ybuf.at[slot, k, pl.ds(r, 1)], sem.at[slot]).start()


def _comb_kernel(idxc_ref, idxn_ref, y_hbm, hp_ref, tw_ref, x1_ref, g2_ref, wgu_ref, wdn_ref, fn_ref,
                 o_ref, ybuf, sem, *, final):
    i = pl.program_id(0)
    nb = pl.num_programs(0)
    slot = i % 2

    @pl.when(i == 0)
    def _():
        _comb_gather(idxc_ref, y_hbm, ybuf, sem, 0)

    @pl.when(i + 1 < nb)
    def _():
        _comb_gather(idxn_ref, y_hbm, ybuf, sem, 1 - slot)

    lo, hi = _unpack_bf16_pair(hp_ref[...])
    h = _dot(lo.astype(BF16), wgu_ref[0:D // 2, :]) + _dot(hi.astype(BF16), wgu_ref[D // 2:D, :])
    gte = h[:, 0:EDIM]
    act = (gte * _sigmoid(gte)) * h[:, EDIM:2 * EDIM]
    shared = _dot(act.astype(BF16), wdn_ref[...])

    for r in range(COMB_TM):
        for k in range(TOPK):
            pltpu.make_async_copy(y_hbm.at[pl.ds(0, 1)], ybuf.at[slot, k, pl.ds(r, 1)],
                                  sem.at[slot]).wait()
    rlo = jnp.zeros((COMB_TM, D // 2), F32)
    rhi = jnp.zeros((COMB_TM, D // 2), F32)
    tw = tw_ref[...]
    for k in range(TOPK):
        a, b = _unpack_bf16_pair(ybuf[slot, k])
        wk = tw[:, k:k + 1]
        rlo = rlo + wk * a
        rhi = rhi + wk * b
    g2 = g2_ref[0]
    x1 = x1_ref[...]
    out_lo = x1[:, 0:D // 2] + g2[:, 0:D // 2] * (rlo + shared[:, 0:D // 2])
    out_hi = x1[:, D // 2:D] + g2[:, D // 2:D] * (rhi + shared[:, D // 2:D])
    if final:
        ms = (jnp.sum(out_lo * out_lo, axis=-1, keepdims=True)
              + jnp.sum(out_hi * out_hi, axis=-1, keepdims=True)) * (1.0 / D)
        inv = lax.rsqrt(ms + EPS)
        fn = fn_ref[...]
        out_lo = out_lo * inv * fn[:, 0:D // 2]
        out_hi = out_hi * inv * fn[:, D // 2:D]
    o_ref[:, 0:D // 2] = out_lo
    o_ref[:, D // 2:D] = out_hi


def _combine(slot_of, top_w, y_slots, h2p, x1, mods6, w_sgu_bf, w_sdn_bf, final_norm, seg, row0, final):
    B, L, mod_base, mod_stride = seg
    n = B * L
    tm = COMB_TM
    nb = n // tm
    nt = L // tm
    blk0 = row0 // tm
    idx3 = slot_of.reshape(-1, 1, tm * TOPK)
    c2 = lambda i: (0, 0)
    return pl.pallas_call(
        functools.partial(_comb_kernel, final=final),
        grid=(nb,),
        in_specs=[pl.BlockSpec((1, 1, tm * TOPK), lambda i: (blk0 + i, 0, 0), memory_space=pltpu.SMEM),
                  pl.BlockSpec((1, 1, tm * TOPK), lambda i: (blk0 + jnp.minimum(i + 1, nb - 1), 0, 0),
                               memory_space=pltpu.SMEM),
                  pl.BlockSpec(memory_space=pl.ANY),
                  pl.BlockSpec((tm, D // 2), lambda i: (blk0 + i, 0)),
                  pl.BlockSpec((tm, TOPK), lambda i: (blk0 + i, 0)),
                  pl.BlockSpec((tm, D), lambda i: (i, 0)),
                  pl.BlockSpec((1, 1, D), lambda i: ((mod_base + (i // nt) * mod_stride) * 6 + 5, 0, 0)),
                  pl.BlockSpec((D, 2 * EDIM), c2),
                  pl.BlockSpec((EDIM, D), c2),
                  pl.BlockSpec((1, D), c2)],
        out_specs=pl.BlockSpec((tm, D), lambda i: (i, 0)),
        out_shape=jax.ShapeDtypeStruct((n, D), F32),
        scratch_shapes=[pltpu.VMEM((2, TOPK, tm, D // 2), U32), pltpu.SemaphoreType.DMA((2,))],
        compiler_params=_cp(("arbitrary",)),
        name="moe_combine",
    )(idx3, idx3, y_slots, h2p, top_w, x1, mods6, w_sgu_bf, w_sdn_bf, final_norm.reshape(1, D))


def _routing_tables(top_i):
    n = top_i.shape[0]
    onehot = (top_i[:, :, None] == jnp.arange(NE, dtype=jnp.int32)[None, None, :])
    mask = jnp.any(onehot, axis=1).astype(jnp.int32)
    counts = jnp.sum(mask, axis=0)
    rank = jnp.cumsum(mask, axis=0) - mask
    padded = (counts + MOE_BM - 1) // MOE_BM * MOE_BM
    pad_end = jnp.cumsum(padded)
    pad_start = pad_end - padded
    slot_all = pad_start[None, :] + rank
    slot_of = jnp.take_along_axis(slot_all, top_i, axis=1).astype(jnp.int32)
    n_blocks = n * TOPK // MOE_BM + NE
    starts = jnp.arange(n_blocks, dtype=jnp.int32) * MOE_BM
    block_e = jnp.minimum(jnp.sum((pad_end[None, :] <= starts[:, None]).astype(jnp.int32), axis=1),
                          NE - 1).astype(jnp.int32)
    n_used = (pad_end[NE - 1:NE] // MOE_BM).astype(jnp.int32)
    return slot_of, block_e, n_used


def _rope_tables(n_tok):
    rows = n_tok // GRID_W
    row = jnp.repeat(jnp.arange(rows), GRID_W).astype(F32)
    col = jnp.tile(jnp.arange(GRID_W), rows).astype(F32)
    half = QK // 2
    inv = ROPE_THETA ** (-jnp.arange(0, half, 2, dtype=F32) / half)
    ang = jnp.concatenate([row[:, None] * inv, col[:, None] * inv], axis=-1)
    cos = jnp.repeat(jnp.cos(ang), 2, axis=-1)
    sin = jnp.repeat(jnp.sin(ang), 2, axis=-1)
    sign = jnp.tile(jnp.array([-1.0, 1.0], F32), QK // 2)
    return jnp.tile(cos, (1, 2)), jnp.tile(sin * sign, (1, 2))


def kernel(x_prompt, x_sample, cache_attn_k, cache_attn_v, state_rwkv, state_s5, c, c_ctx, w_mod, b_mod, norm1, norm2, w_in, w_out, att_lambda, att_subln, rwkv_mu, rwkv_w0, rwkv_w_up, rwkv_a0, rwkv_a_up, rwkv_g_up, rwkv_k_k, rwkv_k_a, rwkv_r_k, rwkv_ln_g, rwkv_ln_b, s5_lam_re, s5_lam_im, s5_log_step, s5_b_re, s5_b_im, s5_c_re, s5_c_im, s5_d, s5_w_glu, moe_router, moe_bias, moe_w_gate_up, moe_w_down, shared_w_gate_up, shared_w_down, final_norm):
    params = dict(w_mod=w_mod, b_mod=b_mod, norm1=norm1, norm2=norm2, w_in=w_in, w_out=w_out,
                  att_lambda=att_lambda, att_subln=att_subln,
                  rwkv_mu=rwkv_mu, rwkv_w0=rwkv_w0, rwkv_w_up=rwkv_w_up, rwkv_a0=rwkv_a0,
                  rwkv_a_up=rwkv_a_up, rwkv_g_up=rwkv_g_up, rwkv_k_k=rwkv_k_k, rwkv_k_a=rwkv_k_a,
                  rwkv_r_k=rwkv_r_k, rwkv_ln_g=rwkv_ln_g, rwkv_ln_b=rwkv_ln_b,
                  s5_lam_re=s5_lam_re, s5_lam_im=s5_lam_im, s5_log_step=s5_log_step,
                  s5_b_re=s5_b_re, s5_b_im=s5_b_im, s5_c_re=s5_c_re, s5_c_im=s5_c_im,
                  s5_d=s5_d, s5_w_glu=s5_w_glu,
                  moe_router=moe_router, moe_bias=moe_bias, moe_w_gate_up=moe_w_gate_up,
                  moe_w_down=moe_w_down, shared_w_gate_up=shared_w_gate_up,
                  shared_w_down=shared_w_down)
    bp, lp_len, _ = x_prompt.shape
    bs, ls_len, _ = x_sample.shape
    segs = ((bp, lp_len, 0, 0), (bs, ls_len, 1, 1))
    xs = [x_prompt.reshape(bp * lp_len, D), x_sample.reshape(bs * ls_len, D)]
    cpad = jnp.zeros((16, D), F32).at[0].set(c_ctx).at[1:1 + bs].set(c)
    new_k, new_v, new_r, new_s = [], [], [], []
    n_slots = ((bp * lp_len + bs * ls_len) * TOPK // MOE_BM + NE) * MOE_BM
    x_sorted = jnp.zeros((n_slots, D // 2), U32)
    for l in range(DEPTH):
        lp = {name: arr[l] for name, arr in params.items() if name not in _WHOLE}
        whole = {name: params[name] for name in _WHOLE}
        lam_init = 0.8 - 0.6 * math.exp(-0.3 * l)
        ctxs = (None, (cache_attn_k[:, l], cache_attn_v[:, l], state_rwkv[:, l], state_s5[:, l]))
        xs, caches, x_sorted = _layer(xs, segs, cpad, lp, whole, l, lam_init, ctxs, final_norm,
                                      l == DEPTH - 1, x_sorted)
        ck, cv, cr, cs = caches[0]
        new_k.append(ck)
        new_v.append(cv)
        new_r.append(cr)
        new_s.append(cs)
    return (xs[0].reshape(bp, lp_len, D), xs[1].reshape(bs, ls_len, D),
            jnp.stack(new_k, axis=1), jnp.stack(new_v, axis=1),
            jnp.stack(new_r, axis=1), jnp.stack(new_s, axis=1))


_WHOLE = ('w_mod', 'moe_w_gate_up', 'moe_w_down')


def _layer(xs, segs, cpad, lp, whole, layer, lam_init, ctxs, final_norm, final, x_sorted):
    hh = jnp.arange(RW) // QK
    segmat = (hh[:, None] == hh[None, :]).astype(F32)
    perm = jnp.concatenate([jnp.arange(0, 3072), jnp.arange(3072 + RW_PROJ, PROJ_W),
                            jnp.arange(3072, 3072 + RW_PROJ)])
    mods6 = _modulation(cpad, whole['w_mod'], lp['b_mod'], layer).reshape(16 * 6, 1, D)
    w_in_bf = lp['w_in'][:, perm].astype(BF16)
    w_out_bf = lp['w_out'].astype(BF16)
    wb, wc, ab = _s5_params(lp)
    x1s, h2ps, tws, tis, caches = [], [], [], [], []
    for si, seg in enumerate(segs):
        B, L = seg[0], seg[1]
        proj = _in_proj(xs[si], lp['norm1'], mods6, w_in_bf, seg)
        proj3 = proj.reshape(L, B, PROJ_STRIDE)
        if ctxs[si] is None:
            o_att = _attention(proj, lp['att_lambda'], lp['att_subln'], lam_init, seg)
            s0_rwkv = None
            x0 = jnp.zeros((2, 2, B, S5S), F32)
        else:
            ck, cv, s0_rwkv, s0_s5 = ctxs[si]
            ctx = (ck.reshape(B, -1, ATT_W), cv.reshape(B, -1, ATT_W))
            o_att = _attention(proj, lp['att_lambda'], lp['att_subln'], lam_init, seg, ctx,
                               _rope_tables(L))
            x0 = s0_s5.astype(F32).reshape(B, 2, 2, S5S).transpose(1, 2, 0, 3)
        o_sum, shared, misc, s_fin = _rwkv_mix(proj, lp, segmat, seg, s0_rwkv)
        y_s5, xf = _s5_scan(proj3, wb, wc, ab, x0, seg)
        o_cat = _mix_post(o_att, o_sum, shared, misc, y_s5, proj, segmat, lp, seg)
        x1, h2p, tw, ti = _out_proj(o_cat, xs[si], w_out_bf, mods6, lp['norm2'],
                                    lp['moe_router'], lp['moe_bias'], seg)
        x1s.append(x1)
        h2ps.append(h2p)
        tws.append(tw)
        tis.append(ti)
        caches.append((proj3[:, :, COL_K:COL_K + ATT_W].transpose(1, 0, 2).reshape(B, L, HEADS, 2, QK),
                       proj3[:, :, COL_V:COL_V + ATT_W].transpose(1, 0, 2).reshape(B, L, HEADS, 2 * QK),
                       s_fin,
                       xf.transpose(2, 0, 1, 3).reshape(B, 2, 2, S5G, S5N)))
    h2p_all = jnp.concatenate(h2ps, axis=0)
    top_w = jnp.concatenate(tws, axis=0)
    slot_of, block_e, n_used = _routing_tables(jnp.concatenate(tis, axis=0))
    x_sorted = _moe_dispatch(slot_of, h2p_all, x_sorted)
    y_slots = _moe_experts(block_e, n_used, x_sorted, whole['moe_w_gate_up'], whole['moe_w_down'], layer)
    w_sgu_bf = lp['shared_w_gate_up'].astype(BF16)
    w_sdn_bf = lp['shared_w_down'].astype(BF16)
    outs = []
    row0 = 0
    for si, seg in enumerate(segs):
        outs.append(_combine(slot_of, top_w, y_slots, h2p_all, x1s[si], mods6, w_sgu_bf, w_sdn_bf,
                             final_norm, seg, row0, final))
        row0 += seg[0] * seg[1]
    return outs, caches, x_sorted
```

```python
import functools
import math

import jax
import jax.numpy as jnp
from jax import lax
from jax.experimental import pallas as pl
from jax.experimental.pallas import tpu as pltpu

F32 = jnp.float32
BF16 = jnp.bfloat16
U32 = jnp.uint32
HI = lax.Precision.HIGHEST

D = 2048
DEPTH = 2
EPS = 1e-6
GRID_W = 64
ROPE_THETA = 10000.0
HEADS = 8
QK = 64
ATT_W = 1024
RW = 512
RW_PROJ = 1792
S5W = 512
S5G = 32
S5N = 64
S5C = 16
S5S = S5G * S5N
S5_PARTS = 4
S5_PW = S5W // S5_PARTS
S5_PS = S5S // S5_PARTS
GN_EPS = 64e-5
NE = 64
TOPK = 8
EDIM = 512
ROUTE_SCALE = 2.5
PROJ_W = 5376
PROJ_STRIDE = 7168
COL_Q, COL_K, COL_V, COL_U, COL_Z = 0, 1024, 2048, 3072, 3584
PROJ_TN = 1792
MOE_BM = 256
VMEM_LIMIT = 56 * 1024 * 1024


def _cp(sem):
    return pltpu.CompilerParams(dimension_semantics=sem, vmem_limit_bytes=VMEM_LIMIT)


def _sigmoid(x):
    return 1.0 / (1.0 + jnp.exp(-x))


def _dot(a, b):
    return jnp.dot(a, b, preferred_element_type=F32)


def _dot_nt(a, b):
    return lax.dot_general(a, b, (((1,), (1,)), ((), ())), preferred_element_type=F32)


def _dot_hi(a, b):
    return jnp.dot(a, b, precision=HI, preferred_element_type=F32)


def _bf16_bits(x):
    b = lax.bitcast_convert_type(x, U32)
    return b + jnp.uint32(0x7FFF) + ((b >> 16) & jnp.uint32(1))


def _pack_bf16_pair(lo, hi):
    return (_bf16_bits(lo) >> 16) | (_bf16_bits(hi) & jnp.uint32(0xFFFF0000))


def _unpack_bf16_pair(p):
    lo = lax.bitcast_convert_type(p << 16, F32)
    hi = lax.bitcast_convert_type(p & jnp.uint32(0xFFFF0000), F32)
    return lo, hi


def _mod_kernel(c_ref, w_ref, b_ref, o_ref):
    c = c_ref[...]
    s = c * _sigmoid(c)
    o_ref[...] = _dot(s.astype(BF16), w_ref[0].astype(BF16)) + b_ref[...]


def _modulation(cpad, w_mod_all, b_mod, layer):
    tn = 1024
    return pl.pallas_call(
        _mod_kernel,
        grid=(6 * D // tn,),
        in_specs=[pl.BlockSpec((16, D), lambda j: (0, 0)),
                  pl.BlockSpec((1, D, tn), lambda j: (layer, 0, j)),
                  pl.BlockSpec((1, tn), lambda j: (0, j))],
        out_specs=pl.BlockSpec((16, tn), lambda j: (0, j)),
        out_shape=jax.ShapeDtypeStruct((16, 6 * D), F32),
        compiler_params=_cp(("arbitrary",)),
        name="modulation",
    )(cpad, w_mod_all, b_mod.reshape(1, 6 * D))


def _in_kernel(x_ref, g_ref, sc_ref, sh_ref, w_ref, o_ref, h_ref):
    @pl.when(pl.program_id(2) == 0)
    def _():
        x = x_ref[...]
        y = x * lax.rsqrt(jnp.mean(x * x, axis=-1, keepdims=True) + EPS)
        h = (y * g_ref[...]) * (1.0 + sc_ref[0]) + sh_ref[0]
        h_ref[...] = h.astype(BF16)

    o_ref[...] = _dot(h_ref[...], w_ref[...])


def _in_proj(x, norm_g, mods6, w_in_bf, seg):
    B, L, mod_base, mod_stride = seg
    tm = min(L, 512)
    nt = L // tm
    ncol = PROJ_W // PROJ_TN
    nstride = PROJ_STRIDE // PROJ_TN

    def mrow(j):
        return lambda b, t, c: ((mod_base + b * mod_stride) * 6 + j, 0, 0)

    return pl.pallas_call(
        _in_kernel,
        grid=(B, nt, ncol),
        in_specs=[pl.BlockSpec((tm, D), lambda b, t, c: (b * nt + t, 0)),
                  pl.BlockSpec((1, D), lambda b, t, c: (0, 0)),
                  pl.BlockSpec((1, 1, D), mrow(1)),
                  pl.BlockSpec((1, 1, D), mrow(0)),
                  pl.BlockSpec((D, PROJ_TN), lambda b, t, c: (0, c))],
        out_specs=pl.BlockSpec((tm, PROJ_TN), lambda b, t, c: (t, b * nstride + c)),
        out_shape=jax.ShapeDtypeStruct((L, B * PROJ_STRIDE), F32),
        scratch_shapes=[pltpu.VMEM((tm, D), BF16)],
        compiler_params=_cp(("arbitrary", "arbitrary", "arbitrary")),
        name="in_proj",
    )(x, norm_g.reshape(1, D), mods6, mods6, w_in_bf)


def _rope(x, c, s):
    lane = lax.broadcasted_iota(jnp.int32, x.shape, 1)
    nxt = pltpu.roll(x, 127, 1)
    prv = pltpu.roll(x, 1, 1)
    swapped = jnp.where((lane & 1) == 0, nxt, prv)
    return x * c + swapped * s


def _attn_kernel(*refs, lam_init, n_ctx, rope):
    if rope:
        (q_ref, k_ref, v_ref, ck_ref, cv_ref, cq_ref, sq_ref, ckk_ref, skk_ref,
         lam_ref, g_ref, o_ref, kall_ref, vall_ref) = refs
    else:
        q_ref, k_ref, v_ref, lam_ref, g_ref, o_ref, kall_ref, vall_ref = refs

    @pl.when(pl.program_id(2) == 0)
    def _():
        k = k_ref[...]
        if rope:
            k = _rope(k, ckk_ref[...], skk_ref[...])
            kall_ref[0:n_ctx, :] = ck_ref[0].astype(BF16)
            vall_ref[0:n_ctx, 0:128] = cv_ref[0].astype(BF16)
        kall_ref[n_ctx:, :] = k.astype(BF16)
        vall_ref[n_ctx:, 0:128] = v_ref[...].astype(BF16)
        vall_ref[:, 128:256] = jnp.ones((vall_ref.shape[0], 128), BF16)

    q = q_ref[...]
    if rope:
        q = _rope(q, cq_ref[...], sq_ref[...])
    q = q * (QK ** -0.5 * math.log2(math.e))
    lane = lax.broadcasted_iota(jnp.int32, q.shape, 1)
    kall = kall_ref[...]
    vall = vall_ref[...]
    outs = []
    for m in range(2):
        qm = jnp.where((lane < QK) == (m == 0), q, 0.0).astype(BF16)
        s = lax.dot_general(qm, kall, (((1,), (1,)), ((), ())), preferred_element_type=F32)
        mx = jnp.max(s, axis=-1, keepdims=True)
        acc = _dot(jnp.exp2(s - mx).astype(BF16), vall)
        outs.append(acc[:, 0:128] / acc[:, 128:129])
    lv = lam_ref[...]
    lam = (jnp.exp(jnp.sum(lv[0:1] * lv[1:2], axis=-1, keepdims=True))
           - jnp.exp(jnp.sum(lv[2:3] * lv[3:4], axis=-1, keepdims=True)) + lam_init)
    o = outs[0] - lam * outs[1]
    o = o * lax.rsqrt(jnp.mean(o * o, axis=-1, keepdims=True) + EPS) * g_ref[...]
    o_ref[...] = o * (1.0 - lam_init)


def _attention(proj, att_lambda, subln, lam_init, seg, ctx=None, tables=None):
    B, L = seg[0], seg[1]
    tq = min(L, 512)
    nq = L // tq
    cs = PROJ_STRIDE // 128
    rope = ctx is not None
    n_ctx = ctx[0].shape[1] if rope else 0
    specs = [pl.BlockSpec((tq, 128), lambda b, h, t: (t, b * cs + COL_Q // 128 + h)),
             pl.BlockSpec((L, 128), lambda b, h, t: (0, b * cs + COL_K // 128 + h)),
             pl.BlockSpec((L, 128), lambda b, h, t: (0, b * cs + COL_V // 128 + h))]
    args = [proj, proj, proj]
    if rope:
        cos_t, sin_t = tables
        specs += [pl.BlockSpec((1, n_ctx, 128), lambda b, h, t: (b, 0, h)),
                  pl.BlockSpec((1, n_ctx, 128), lambda b, h, t: (b, 0, h)),
                  pl.BlockSpec((tq, 128), lambda b, h, t: (t, 0)),
                  pl.BlockSpec((tq, 128), lambda b, h, t: (t, 0)),
                  pl.BlockSpec((L, 128), lambda b, h, t: (0, 0)),
                  pl.BlockSpec((L, 128), lambda b, h, t: (0, 0))]
        args += [ctx[0], ctx[1], cos_t, sin_t, cos_t, sin_t]
    specs += [pl.BlockSpec((4, QK), lambda b, h, t: (0, 0)),
              pl.BlockSpec((1, 128), lambda b, h, t: (0, 0))]
    args += [att_lambda, subln.reshape(1, 128)]
    return pl.pallas_call(
        functools.partial(_attn_kernel, lam_init=lam_init, n_ctx=n_ctx, rope=rope),
        grid=(B, HEADS, nq),
        in_specs=specs,
        out_specs=pl.BlockSpec((tq, 128), lambda b, h, t: (b * nq + t, h)),
        out_shape=jax.ShapeDtypeStruct((B * L, ATT_W), F32),
        scratch_shapes=[pltpu.VMEM((n_ctx + L, 128), BF16), pltpu.VMEM((n_ctx + L, 256), BF16)],
        compiler_params=_cp(("arbitrary", "arbitrary", "arbitrary")),
        name="diff_attention",
    )(*args)


def _rwkv_pre_kernel(z_ref, zp_ref, zn_ref, mu_ref, seg_ref, wup_ref, aup_ref, gup_ref,
                     w0_ref, a0_ref, kk_ref, ka_ref, sh_ref, pd_ref, ms_ref, *, tm):
    t = pl.program_id(1)
    nt = pl.num_programs(1)
    z = z_ref[...]
    row = lax.broadcasted_iota(jnp.int32, z.shape, 0)
    prev_row = jnp.where(t > 0, zp_ref[7:8, :], 0.0)
    next_row = jnp.where(t < nt - 1, zn_ref[0:1, :], 0.0)
    zp = jnp.where(row == 0, prev_row, pltpu.roll(z, 1, 0))
    zn = jnp.where(row == tm - 1, next_row, pltpu.roll(z, tm - 1, 0))
    zs = z + mu_ref[...] * (0.5 * (zp + zn) - z)
    r = zs[:, 0:RW]
    k = zs[:, RW:2 * RW]
    v = zs[:, 2 * RW:3 * RW]
    wa = zs[:, 3 * RW:3 * RW + 128]
    gl = zs[:, 3 * RW + 128:3 * RW + 256]
    lane = lax.broadcasted_iota(jnp.int32, wa.shape, 1)
    wa = jnp.where(lane < 64, jnp.tanh(wa), wa)
    g = _dot_hi(_sigmoid(gl), gup_ref[...])
    kk = k * kk_ref[...]
    kk = kk * lax.rsqrt(_dot_hi(kk * kk, seg_ref[...]) + EPS)
    sh_ref[0] = r
    sh_ref[1] = v
    sh_ref[2] = kk
    kb = jnp.zeros_like(k)
    for d in range(2):
        xw = w0_ref[d] + _dot_hi(wa, wup_ref[d])
        w = jnp.exp(-math.exp(-0.5) * _sigmoid(xw))
        a = _sigmoid(a0_ref[d] + _dot_hi(wa, aup_ref[d]))
        kd = k * (1.0 + (a - 1.0) * ka_ref[...])
        pd_ref[0, d] = w
        pd_ref[1, d] = kd
        pd_ref[2, d] = kk * a
        kb = kb + kd
    ms_ref[0] = g
    ms_ref[1] = kb


def _rwkv_pre(proj, lp, segmat, seg):
    B, L = seg[0], seg[1]
    tm = min(L, 256)
    nt = L // tm
    zs = PROJ_STRIDE // RW_PROJ
    zc = COL_Z // RW_PROJ
    nb8 = L // 8
    zeros64 = jnp.zeros((2, 64, RW), F32)
    wup = jnp.concatenate([lp['rwkv_w_up'], zeros64], axis=1)
    aup = jnp.concatenate([zeros64, lp['rwkv_a_up']], axis=1)
    c2 = lambda b, t: (0, 0)
    c3 = lambda b, t: (0, 0, 0)
    return pl.pallas_call(
        functools.partial(_rwkv_pre_kernel, tm=tm),
        grid=(B, nt),
        in_specs=[pl.BlockSpec((tm, RW_PROJ), lambda b, t: (t, b * zs + zc)),
                  pl.BlockSpec((8, RW_PROJ), lambda b, t: (jnp.maximum(t * (tm // 8) - 1, 0), b * zs + zc)),
                  pl.BlockSpec((8, RW_PROJ), lambda b, t: (jnp.minimum((t + 1) * (tm // 8), nb8 - 1), b * zs + zc)),
                  pl.BlockSpec((1, RW_PROJ), c2),
                  pl.BlockSpec((RW, RW), c2),
                  pl.BlockSpec((2, 128, RW), c3),
                  pl.BlockSpec((2, 128, RW), c3),
                  pl.BlockSpec((128, RW), c2),
                  pl.BlockSpec((2, 1, RW), c3),
                  pl.BlockSpec((2, 1, RW), c3),
                  pl.BlockSpec((1, RW), c2),
                  pl.BlockSpec((1, RW), c2)],
        out_specs=[pl.BlockSpec((3, tm, RW), lambda b, t: (0, t, b)),
                   pl.BlockSpec((3, 2, tm, RW), lambda b, t: (0, 0, t, b)),
                   pl.BlockSpec((2, tm, RW), lambda b, t: (0, t, b))],
        out_shape=[jax.ShapeDtypeStruct((3, L, B * RW), F32),
                   jax.ShapeDtypeStruct((3, 2, L, B * RW), F32),
                   jax.ShapeDtypeStruct((2, L, B * RW), F32)],
        compiler_params=_cp(("arbitrary", "arbitrary")),
        name="rwkv_pre",
    )(proj, proj, proj, lp['rwkv_mu'].reshape(1, RW_PROJ), segmat, wup, aup, lp['rwkv_g_up'],
      lp['rwkv_w0'].reshape(2, 1, RW), lp['rwkv_a0'].reshape(2, 1, RW),
      lp['rwkv_k_k'].reshape(1, RW), lp['rwkv_k_a'].reshape(1, RW))


def _wkv_kernel(xa_ref, xb_ref, s0_ref, yf_ref, yr_ref, sf_ref, st_ref, *, tc):
    c = pl.program_id(1)

    @pl.when(c == 0)
    def _():
        st_ref[...] = s0_ref[...]

    fwd = lax.broadcasted_iota(jnp.int32, (QK, 128), 1) < 64

    def step(t, carry):
        tr = tc - 1 - t

        def tile(q):
            return jnp.where(fwd, xa_ref[q, t], xb_ref[q, tr])

        r_t = tile(0)
        v_t = tile(1)
        kk_t = tile(2)
        w_t = tile(3)
        kd_t = tile(4)
        b_t = tile(5)
        wr = w_t * r_t
        kr = jnp.sum(kd_t * r_t, axis=0, keepdims=True)
        br = jnp.sum(b_t * r_t, axis=0, keepdims=True)
        for g in range(QK // 8):
            ys = []
            for j in range(8):
                vi = g * 8 + j
                s = st_ref[vi]
                sa = jnp.sum(s * kk_t, axis=0, keepdims=True)
                y0 = jnp.sum(s * wr, axis=0, keepdims=True)
                vv = v_t[vi:vi + 1]
                st_ref[vi] = s * w_t + (vv * kd_t - sa * b_t)
                ys.append(y0 + vv * kr - sa * br)
            ytile = jnp.concatenate(ys, axis=0)
            yf_ref[t, g * 8:(g + 1) * 8, :] = ytile
            yr_ref[tr, g * 8:(g + 1) * 8, :] = ytile
        return carry

    lax.fori_loop(0, tc, step, 0)

    @pl.when(c == pl.num_programs(1) - 1)
    def _():
        sf_ref[...] = st_ref[...]


def _wkv_scan(xs, s0):
    _, L, _, lanes = xs.shape
    tc = 16
    nc = L // tc
    return pl.pallas_call(
        functools.partial(_wkv_kernel, tc=tc),
        grid=(lanes // 128, nc),
        in_specs=[pl.BlockSpec((6, tc, QK, 128), lambda g, c: (0, c, 0, g)),
                  pl.BlockSpec((6, tc, QK, 128), lambda g, c: (0, nc - 1 - c, 0, g)),
                  pl.BlockSpec((QK, QK, 128), lambda g, c: (0, 0, g))],
        out_specs=[pl.BlockSpec((tc, QK, 128), lambda g, c: (c, 0, g)),
                   pl.BlockSpec((tc, QK, 128), lambda g, c: (nc - 1 - c, 0, g)),
                   pl.BlockSpec((QK, QK, 128), lambda g, c: (0, 0, g))],
        out_shape=[jax.ShapeDtypeStruct((L, QK, lanes), F32),
                   jax.ShapeDtypeStruct((L, QK, lanes), F32),
                   jax.ShapeDtypeStruct((QK, QK, lanes), F32)],
        scratch_shapes=[pltpu.VMEM((QK, QK, 128), F32)],
        compiler_params=_cp(("arbitrary", "arbitrary")),
        name="wkv7_scan",
    )(xs, xs, s0)


def _rwkv_mix(proj, lp, segmat, seg, s0_bdhvk):
    B, L = seg[0], seg[1]
    G = B // 8
    shared, perdir, misc = _rwkv_pre(proj, lp, segmat, seg)
    sh = shared.reshape(3, L, G, 8, HEADS, QK)
    pd = perdir.reshape(3, 2, L, G, 8, HEADS, QK)
    full = jnp.concatenate([jnp.stack([sh, sh], axis=1), pd], axis=0)
    xs = full.transpose(0, 2, 6, 3, 1, 4, 5).reshape(6, L, QK, G * 128)
    if s0_bdhvk is None:
        s0 = jnp.zeros((QK, QK, G * 128), F32)
    else:
        s0 = (s0_bdhvk.astype(F32).reshape(G, 8, 2, HEADS, QK, QK)
              .transpose(4, 5, 0, 2, 1, 3).reshape(QK, QK, G * 128))
    yf, yr, sf = _wkv_scan(xs, s0)
    y = (yf.reshape(L, QK, G, 2, 8, HEADS)[:, :, :, 0] + yr.reshape(L, QK, G, 2, 8, HEADS)[:, :, :, 1])
    o_sum = y.transpose(0, 2, 3, 4, 1).reshape(L, B * RW)
    s_fin = (sf.reshape(QK, QK, G, 2, 8, HEADS).transpose(2, 4, 3, 5, 0, 1)
             .reshape(B, 2, HEADS, QK, QK))
    return o_sum, shared, misc, s_fin


def _s5_kernel(u_ref, wb_ref, wc_ref, ab_ref, x0_ref, y_ref, xf_ref, bx_ref, st_ref, *, tc, nb):
    d = pl.program_id(0)
    c = pl.program_id(1)

    @pl.when(c == 0)
    def _():
        st_ref[...] = x0_ref[0]

    u = u_ref[...].reshape(tc * nb, S5W).astype(BF16)
    for j in range(S5_PARTS):
        bj = _dot(u[:, j * S5_PW:(j + 1) * S5_PW], wb_ref[0, j])
        bx_ref[:, j * S5_PS:(j + 1) * S5_PS] = bj[:, 0:S5_PS]
        bx_ref[:, S5S + j * S5_PS:S5S + (j + 1) * S5_PS] = bj[:, S5_PS:2 * S5_PS]
    ar = jnp.broadcast_to(ab_ref[0, 0:1, :], (nb, S5S))
    ai = jnp.broadcast_to(ab_ref[0, 1:2, :], (nb, S5S))

    def step(i, carry):
        tt = jnp.where(d == 0, i, tc - 1 - i)
        rows = pl.ds(pl.multiple_of(tt * nb, nb), nb)
        xr = st_ref[0]
        xi = st_ref[1]
        nr = ar * xr - ai * xi + bx_ref[rows, 0:S5S]
        ni = ar * xi + ai * xr + bx_ref[rows, S5S:2 * S5S]
        st_ref[0] = nr
        st_ref[1] = ni
        bx_ref[rows, 0:S5S] = nr
        bx_ref[rows, S5S:2 * S5S] = ni
        return carry

    lax.fori_loop(0, tc, step, 0)
    ys = []
    for j in range(S5_PARTS):
        xr = bx_ref[:, j * S5_PS:(j + 1) * S5_PS].astype(BF16)
        xi = bx_ref[:, S5S + j * S5_PS:S5S + (j + 1) * S5_PS].astype(BF16)
        ys.append(_dot(xr, wc_ref[j, 0:S5_PS, :]) + _dot(xi, wc_ref[j, S5_PS:2 * S5_PS, :]))
    y_ref[0] = jnp.concatenate(ys, axis=-1).reshape(tc, nb, S5W)

    @pl.when(c == pl.num_programs(1) - 1)
    def _():
        xf_ref[0] = st_ref[...]


def _s5_scan(proj3, wb, wc, ab, x0, seg):
    B, L = seg[0], seg[1]
    tc = 64 if B <= 8 else 16
    nc = L // tc
    tmap = lambda d, c: jnp.where(d == 0, c, nc - 1 - c)
    return pl.pallas_call(
        functools.partial(_s5_kernel, tc=tc, nb=B),
        grid=(2, nc),
        in_specs=[pl.BlockSpec((tc, B, S5W), lambda d, c: (tmap(d, c), 0, COL_U // S5W)),
                  pl.BlockSpec((1, S5_PARTS, S5_PW, 2 * S5_PS), lambda d, c: (d, 0, 0, 0)),
                  pl.BlockSpec((S5_PARTS, 2 * S5_PS, S5_PW), lambda d, c: (0, 0, 0)),
                  pl.BlockSpec((1, 2, S5S), lambda d, c: (d, 0, 0)),
                  pl.BlockSpec((1, 2, B, S5S), lambda d, c: (d, 0, 0, 0))],
        out_specs=[pl.BlockSpec((1, tc, B, S5W), lambda d, c: (d, tmap(d, c), 0, 0)),
                   pl.BlockSpec((1, 2, B, S5S), lambda d, c: (d, 0, 0, 0))],
        out_shape=[jax.ShapeDtypeStruct((2, L, B, S5W), F32),
                   jax.ShapeDtypeStruct((2, 2, B, S5S), F32)],
        scratch_shapes=[pltpu.VMEM((tc * B, 2 * S5S), F32), pltpu.VMEM((2, B, S5S), F32)],
        compiler_params=_cp(("arbitrary", "arbitrary")),
        name="s5_scan",
    )(proj3, wb, wc, ab, x0)


def _s5_params(lp):
    lam_re, lam_im = lp['s5_lam_re'], lp['s5_lam_im']
    dt = jnp.exp(lp['s5_log_step'])[:, :, None]
    mag = jnp.exp(lam_re * dt)
    ab_re, ab_im = mag * jnp.cos(lam_im * dt), mag * jnp.sin(lam_im * dt)
    den = lam_re * lam_re + lam_im * lam_im
    f_re = ((ab_re - 1.0) * lam_re + ab_im * lam_im) / den
    f_im = (ab_im * lam_re - (ab_re - 1.0) * lam_im) / den
    b_re, b_im = lp['s5_b_re'], lp['s5_b_im']
    wre = f_re[..., None] * b_re[None] - f_im[..., None] * b_im[None]
    wim = f_re[..., None] * b_im[None] + f_im[..., None] * b_re[None]
    eye = jnp.eye(S5G, dtype=F32)

    def block_in(w):
        return jnp.einsum('dgnc,gh->dgchn', w, eye).reshape(2, S5W, S5S)

    wb_re, wb_im = block_in(wre), block_in(wim)
    wb = jnp.stack([jnp.concatenate([wb_re[:, j * S5_PW:(j + 1) * S5_PW, j * S5_PS:(j + 1) * S5_PS],
                                     wb_im[:, j * S5_PW:(j + 1) * S5_PW, j * S5_PS:(j + 1) * S5_PS]], axis=-1)
                    for j in range(S5_PARTS)], axis=1).astype(BF16)

    def block_out(cm):
        return jnp.einsum('gcn,gh->gnhc', cm, eye).reshape(S5S, S5W)

    wc_re, wc_im = block_out(lp['s5_c_re']), -block_out(lp['s5_c_im'])
    wc = jnp.stack([jnp.concatenate([wc_re[j * S5_PS:(j + 1) * S5_PS, j * S5_PW:(j + 1) * S5_PW],
                                     wc_im[j * S5_PS:(j + 1) * S5_PS, j * S5_PW:(j + 1) * S5_PW]], axis=0)
                    for j in range(S5_PARTS)], axis=0).astype(BF16)
    ab = jnp.stack([ab_re.reshape(2, S5S), ab_im.reshape(2, S5S)], axis=1)
    return wb, wc, ab


def _mix_kernel(oatt_ref, osum_ref, sh_ref, ms_ref, y_ref, u_ref, seg_ref, lng_ref, lnb_ref,
                rk_ref, d_ref, wglu_ref, o_ref):
    segm = seg_ref[...]
    o = osum_ref[...]
    mean = _dot_hi(o, segm) * (1.0 / QK)
    oc = o - mean
    var = _dot_hi(oc * oc, segm) * (1.0 / QK)
    o_n = oc * lax.rsqrt(var + GN_EPS) * lng_ref[...] + lnb_ref[...]
    r = sh_ref[0]
    v = sh_ref[1]
    bonus = _dot_hi(r * 0.5 * ms_ref[1] * rk_ref[...], segm) * v
    rw = (o_n + bonus) * ms_ref[0]
    u = u_ref[...]
    y = d_ref[...] * u + y_ref[0] + y_ref[1]
    hg = 0.5 * y * (1.0 + jnp.tanh(math.sqrt(2.0 / math.pi) * (y + 0.044715 * (y * y * y))))
    s5 = hg * _sigmoid(_dot(hg.astype(BF16), wglu_ref[...]))
    o_ref[:, 0:ATT_W] = oatt_ref[...].astype(BF16)
    o_ref[:, ATT_W:ATT_W + RW] = rw.astype(BF16)
    o_ref[:, ATT_W + RW:D] = s5.astype(BF16)


def _mix_post(o_att, o_sum, shared, misc, y_s5, proj, segmat, lp, seg):
    B, L = seg[0], seg[1]
    tm = min(L, 256)
    nt = L // tm
    us = PROJ_STRIDE // S5W
    c2 = lambda b, t: (0, 0)
    return pl.pallas_call(
        _mix_kernel,
        grid=(B, nt),
        in_specs=[pl.BlockSpec((tm, ATT_W), lambda b, t: (b * nt + t, 0)),
                  pl.BlockSpec((tm, RW), lambda b, t: (t, b)),
                  pl.BlockSpec((3, tm, RW), lambda b, t: (0, t, b)),
                  pl.BlockSpec((2, tm, RW), lambda b, t: (0, t, b)),
                  pl.BlockSpec((2, tm, S5W), lambda b, t: (0, t, b)),
                  pl.BlockSpec((tm, S5W), lambda b, t: (t, b * us + COL_U // S5W)),
                  pl.BlockSpec((RW, RW), c2),
                  pl.BlockSpec((1, RW), c2),
                  pl.BlockSpec((1, RW), c2),
                  pl.BlockSpec((1, RW), c2),
                  pl.BlockSpec((1, S5W), c2),
                  pl.BlockSpec((S5W, S5W), c2)],
        out_specs=pl.BlockSpec((tm, D), lambda b, t: (b * nt + t, 0)),
        out_shape=jax.ShapeDtypeStruct((B * L, D), BF16),
        compiler_params=_cp(("arbitrary", "arbitrary")),
        name="mix_post",
    )(o_att, o_sum, shared, misc, y_s5.reshape(2, L, B * S5W), proj, segmat,
      lp['rwkv_ln_g'].reshape(1, RW), lp['rwkv_ln_b'].reshape(1, RW), lp['rwkv_r_k'].reshape(1, RW),
      lp['s5_d'].reshape(1, S5W), lp['s5_w_glu'].astype(BF16))


def _out_kernel(oc_ref, x_ref, w_ref, g1_ref, sc_ref, sh_ref, n2_ref, rt_ref, rb_ref,
                x1_ref, hp_ref, tw_ref, ti_ref):
    x1 = x_ref[...] + g1_ref[0] * _dot(oc_ref[...], w_ref[...])
    x1_ref[...] = x1
    y = x1 * lax.rsqrt(jnp.mean(x1 * x1, axis=-1, keepdims=True) + EPS)
    h2 = (y * n2_ref[...]) * (1.0 + sc_ref[0]) + sh_ref[0]
    hp_ref[...] = _pack_bf16_pair(h2[:, 0:D // 2], h2[:, D // 2:D])
    h_hi = h2.astype(BF16)
    h_lo = (h2 - h_hi.astype(F32)).astype(BF16)
    r_hi = rt_ref[0]
    r_lo = rt_ref[1]
    logits = _dot_nt(r_hi, h_hi) + (_dot_nt(r_hi, h_lo) + _dot_nt(r_lo, h_hi))
    scores = _sigmoid(logits)
    sel = scores + rb_ref[...]
    row = lax.broadcasted_iota(jnp.int32, sel.shape, 0)
    tws, tis = [], []
    for k in range(TOPK):
        m = jnp.max(sel, axis=0, keepdims=True)
        idx = jnp.min(jnp.where(sel == m, row, NE), axis=0, keepdims=True)
        hit = row == idx
        tws.append(jnp.sum(jnp.where(hit, scores, 0.0), axis=0, keepdims=True))
        tis.append(idx)
        sel = jnp.where(hit, -jnp.inf, sel)
    tw = jnp.concatenate(tws, axis=0)
    tw_ref[...] = tw / jnp.sum(tw, axis=0, keepdims=True) * ROUTE_SCALE
    ti_ref[...] = jnp.concatenate(tis, axis=0)


def _out_proj(o_cat, x, w_out_bf, mods6, norm2, router_t2, moe_bias, seg):
    B, L, mod_base, mod_stride = seg
    tm = min(L, 256)
    nt = L // tm

    def mrow(j):
        return lambda i: ((mod_base + (i // nt) * mod_stride) * 6 + j, 0, 0)

    c2 = lambda i: (0, 0)
    n = B * L
    return pl.pallas_call(
        _out_kernel,
        grid=(n // tm,),
        in_specs=[pl.BlockSpec((tm, D), lambda i: (i, 0)),
                  pl.BlockSpec((tm, D), lambda i: (i, 0)),
                  pl.BlockSpec((D, D), c2),
                  pl.BlockSpec((1, 1, D), mrow(2)),
                  pl.BlockSpec((1, 1, D), mrow(4)),
                  pl.BlockSpec((1, 1, D), mrow(3)),
                  pl.BlockSpec((1, D), c2),
                  pl.BlockSpec((2, NE, D), lambda i: (0, 0, 0)),
                  pl.BlockSpec((NE, 1), c2)],
        out_specs=[pl.BlockSpec((tm, D), lambda i: (i, 0)),
                   pl.BlockSpec((tm, D // 2), lambda i: (i, 0)),
                   pl.BlockSpec((TOPK, tm), lambda i: (0, i)),
                   pl.BlockSpec((TOPK, tm), lambda i: (0, i))],
        out_shape=[jax.ShapeDtypeStruct((n, D), F32),
                   jax.ShapeDtypeStruct((n, D // 2), U32),
                   jax.ShapeDtypeStruct((TOPK, n), F32),
                   jax.ShapeDtypeStruct((TOPK, n), jnp.int32)],
        compiler_params=_cp(("arbitrary",)),
        name="out_proj",
    )(o_cat, x, w_out_bf, mods6, mods6, mods6, norm2.reshape(1, D), router_t2, moe_bias.reshape(NE, 1))


DISP_TM = 128


def _dispatch_copy(hp_ref, xs_ref, sem, r, dst_row):
    return pltpu.make_async_copy(hp_ref.at[pl.ds(r, 1)], xs_ref.at[pl.ds(dst_row, 1)], sem.at[0])


def _dispatch_kernel(idx_ref, hp_ref, xs_in_ref, xs_ref, sem):
    del xs_in_ref
    for r in range(DISP_TM):
        for k in range(TOPK):
            _dispatch_copy(hp_ref, xs_ref, sem, r, idx_ref[0, 0, r * TOPK + k]).start(priority=k % 2)
    for r in range(DISP_TM):
        for k in range(TOPK):
            _dispatch_copy(hp_ref, xs_ref, sem, r, 0).wait()


def _moe_dispatch(slot_of, h2p, xs_init):
    n = slot_of.shape[0]
    tm = DISP_TM
    idx3 = slot_of.reshape(n // tm, 1, tm * TOPK)
    return pl.pallas_call(
        _dispatch_kernel,
        grid=(n // tm,),
        in_specs=[pl.BlockSpec((1, 1, tm * TOPK), lambda i: (i, 0, 0), memory_space=pltpu.SMEM),
                  pl.BlockSpec((tm, D // 2), lambda i: (i, 0)),
                  pl.BlockSpec(memory_space=pl.ANY)],
        out_specs=pl.BlockSpec(memory_space=pl.ANY),
        out_shape=jax.ShapeDtypeStruct(xs_init.shape, U32),
        scratch_shapes=[pltpu.SemaphoreType.DMA((1,))],
        input_output_aliases={2: 0},
        compiler_params=_cp(("arbitrary",)),
        name="moe_dispatch",
    )(idx3, h2p, xs_init)


def _moe_kernel(be_ref, nu_ref, x_ref, wgu_ref, wdn_ref, o_ref, wgu_bf, wdn_bf):
    i = pl.program_id(0)

    @pl.when(i < nu_ref[0])
    def _():
        @pl.when(jnp.logical_or(i == 0, be_ref[i] != be_ref[jnp.maximum(i - 1, 0)]))
        def _():
            wgu_bf[...] = wgu_ref[0, 0].astype(BF16)
            wdn_bf[...] = wdn_ref[0, 0].astype(BF16)

        lo, hi = _unpack_bf16_pair(x_ref[...])
        h = _dot(lo.astype(BF16), wgu_bf[0:D // 2, :]) + _dot(hi.astype(BF16), wgu_bf[D // 2:D, :])
        gte = h[:, 0:EDIM]
        act = (gte * _sigmoid(gte)) * h[:, EDIM:2 * EDIM]
        y = _dot(act.astype(BF16), wdn_bf[...])
        o_ref[...] = _pack_bf16_pair(y[:, 0:D // 2], y[:, D // 2:D])

    @pl.when(i >= nu_ref[0])
    def _():
        o_ref[...] = jnp.zeros(o_ref.shape, U32)


def _moe_experts(block_e, n_used, x_sorted, w_gu_all, w_dn_all, layer):
    n_slots = x_sorted.shape[0]
    nb = n_slots // MOE_BM
    grid_spec = pltpu.PrefetchScalarGridSpec(
        num_scalar_prefetch=2,
        grid=(nb,),
        in_specs=[pl.BlockSpec((MOE_BM, D // 2), lambda i, be, nu: (i, 0)),
                  pl.BlockSpec((1, 1, D, 2 * EDIM), lambda i, be, nu: (layer, be[i], 0, 0)),
                  pl.BlockSpec((1, 1, EDIM, D), lambda i, be, nu: (layer, be[i], 0, 0))],
        out_specs=pl.BlockSpec((MOE_BM, D // 2), lambda i, be, nu: (i, 0)),
        scratch_shapes=[pltpu.VMEM((D, 2 * EDIM), BF16),
                        pltpu.VMEM((EDIM, D), BF16)],
    )
    return pl.pallas_call(
        _moe_kernel,
        grid_spec=grid_spec,
        out_shape=jax.ShapeDtypeStruct((n_slots, D // 2), U32),
        compiler_params=_cp(("arbitrary",)),
        name="moe_experts",
    )(block_e, n_used, x_sorted, w_gu_all, w_dn_all)


COMB_TM = 128


def _comb_gather(idx_ref, y_hbm, ybuf, sem, slot):
    for r in range(COMB_TM):
        for k in range(TOPK):
            pltpu.make_async_copy(y_hbm.at[pl.ds(idx_ref[0, 0, r * TOPK + k], 1)],
                                  ybuf.at[slot, k, pl.ds(r, 1)], sem.at[slot]).start(priority=k % 2)


def _comb_kernel(idxc_ref, idxn_ref, y_hbm, hp_ref, tw_ref, x1_ref, g2_ref, wgu_ref, wdn_ref, fn_ref,
                 o_ref, ybuf, sem, *, final):
    i = pl.program_id(0)
    nb = pl.num_programs(0)
    slot = i % 2

    @pl.when(i == 0)
    def _():
        _comb_gather(idxc_ref, y_hbm, ybuf, sem, 0)

    @pl.when(i + 1 < nb)
    def _():
        _comb_gather(idxn_ref, y_hbm, ybuf, sem, 1 - slot)

    lo, hi = _unpack_bf16_pair(hp_ref[...])
    h = _dot(lo.astype(BF16), wgu_ref[0:D // 2, :]) + _dot(hi.astype(BF16), wgu_ref[D // 2:D, :])
    gte = h[:, 0:EDIM]
    act = (gte * _sigmoid(gte)) * h[:, EDIM:2 * EDIM]
    shared = _dot(act.astype(BF16), wdn_ref[...])

    for r in range(COMB_TM):
        for k in range(TOPK):
            pltpu.make_async_copy(y_hbm.at[pl.ds(0, 1)], ybuf.at[slot, k, pl.ds(r, 1)],
                                  sem.at[slot]).wait()
    rlo = jnp.zeros((COMB_TM, D // 2), F32)
    rhi = jnp.zeros((COMB_TM, D // 2), F32)
    tw = tw_ref[...]
    for k in range(TOPK):
        a, b = _unpack_bf16_pair(ybuf[slot, k])
        wk = tw[:, k:k + 1]
        rlo = rlo + wk * a
        rhi = rhi + wk * b
    g2 = g2_ref[0]
    x1 = x1_ref[...]
    out_lo = x1[:, 0:D // 2] + g2[:, 0:D // 2] * (rlo + shared[:, 0:D // 2])
    out_hi = x1[:, D // 2:D] + g2[:, D // 2:D] * (rhi + shared[:, D // 2:D])
    if final:
        ms = (jnp.sum(out_lo * out_lo, axis=-1, keepdims=True)
              + jnp.sum(out_hi * out_hi, axis=-1, keepdims=True)) * (1.0 / D)
        inv = lax.rsqrt(ms + EPS)
        fn = fn_ref[...]
        out_lo = out_lo * inv * fn[:, 0:D // 2]
        out_hi = out_hi * inv * fn[:, D // 2:D]
    o_ref[:, 0:D // 2] = out_lo
    o_ref[:, D // 2:D] = out_hi


def _combine(slot_of, top_w, y_slots, h2p, x1, mods6, w_sgu_bf, w_sdn_bf, final_norm, seg, row0, final):
    B, L, mod_base, mod_stride = seg
    n = B * L
    tm = COMB_TM
    nb = n // tm
    nt = L // tm
    blk0 = row0 // tm
    idx3 = slot_of.reshape(-1, 1, tm * TOPK)
    c2 = lambda i: (0, 0)
    return pl.pallas_call(
        functools.partial(_comb_kernel, final=final),
        grid=(nb,),
        in_specs=[pl.BlockSpec((1, 1, tm * TOPK), lambda i: (blk0 + i, 0, 0), memory_space=pltpu.SMEM),
                  pl.BlockSpec((1, 1, tm * TOPK), lambda i: (blk0 + jnp.minimum(i + 1, nb - 1), 0, 0),
                               memory_space=pltpu.SMEM),
                  pl.BlockSpec(memory_space=pl.ANY),
                  pl.BlockSpec((tm, D // 2), lambda i: (blk0 + i, 0)),
                  pl.BlockSpec((tm, TOPK), lambda i: (blk0 + i, 0)),
                  pl.BlockSpec((tm, D), lambda i: (i, 0)),
                  pl.BlockSpec((1, 1, D), lambda i: ((mod_base + (i // nt) * mod_stride) * 6 + 5, 0, 0)),
                  pl.BlockSpec((D, 2 * EDIM), c2),
                  pl.BlockSpec((EDIM, D), c2),
                  pl.BlockSpec((1, D), c2)],
        out_specs=pl.BlockSpec((tm, D), lambda i: (i, 0)),
        out_shape=jax.ShapeDtypeStruct((n, D), F32),
        scratch_shapes=[pltpu.VMEM((2, TOPK, tm, D // 2), U32), pltpu.SemaphoreType.DMA((2,))],
        compiler_params=_cp(("arbitrary",)),
        name="moe_combine",
    )(idx3, idx3, y_slots, h2p, top_w, x1, mods6, w_sgu_bf, w_sdn_bf, final_norm.reshape(1, D))


def _routing_tables(top_i):
    n = top_i.shape[0]
    onehot = (top_i[:, :, None] == jnp.arange(NE, dtype=jnp.int32)[None, None, :])
    mask = jnp.any(onehot, axis=1).astype(jnp.int32)
    counts = jnp.sum(mask, axis=0)
    rank = jnp.cumsum(mask, axis=0) - mask
    padded = (counts + MOE_BM - 1) // MOE_BM * MOE_BM
    pad_end = jnp.cumsum(padded)
    pad_start = pad_end - padded
    slot_all = pad_start[None, :] + rank
    slot_of = jnp.take_along_axis(slot_all, top_i, axis=1).astype(jnp.int32)
    n_blocks = n * TOPK // MOE_BM + NE
    starts = jnp.arange(n_blocks, dtype=jnp.int32) * MOE_BM
    block_e = jnp.minimum(jnp.sum((pad_end[None, :] <= starts[:, None]).astype(jnp.int32), axis=1),
                          NE - 1).astype(jnp.int32)
    n_used = (pad_end[NE - 1:NE] // MOE_BM).astype(jnp.int32)
    return slot_of, block_e, n_used


def _rope_tables(n_tok):
    rows = n_tok // GRID_W
    row = jnp.repeat(jnp.arange(rows), GRID_W).astype(F32)
    col = jnp.tile(jnp.arange(GRID_W), rows).astype(F32)
    half = QK // 2
    inv = ROPE_THETA ** (-jnp.arange(0, half, 2, dtype=F32) / half)
    ang = jnp.concatenate([row[:, None] * inv, col[:, None] * inv], axis=-1)
    cos = jnp.repeat(jnp.cos(ang), 2, axis=-1)
    sin = jnp.repeat(jnp.sin(ang), 2, axis=-1)
    sign = jnp.tile(jnp.array([-1.0, 1.0], F32), QK // 2)
    return jnp.tile(cos, (1, 2)), jnp.tile(sin * sign, (1, 2))


def kernel(x_prompt, x_sample, cache_attn_k, cache_attn_v, state_rwkv, state_s5, c, c_ctx, w_mod, b_mod, norm1, norm2, w_in, w_out, att_lambda, att_subln, rwkv_mu, rwkv_w0, rwkv_w_up, rwkv_a0, rwkv_a_up, rwkv_g_up, rwkv_k_k, rwkv_k_a, rwkv_r_k, rwkv_ln_g, rwkv_ln_b, s5_lam_re, s5_lam_im, s5_log_step, s5_b_re, s5_b_im, s5_c_re, s5_c_im, s5_d, s5_w_glu, moe_router, moe_bias, moe_w_gate_up, moe_w_down, shared_w_gate_up, shared_w_down, final_norm):
    params = dict(w_mod=w_mod, b_mod=b_mod, norm1=norm1, norm2=norm2, w_in=w_in, w_out=w_out,
                  att_lambda=att_lambda, att_subln=att_subln,
                  rwkv_mu=rwkv_mu, rwkv_w0=rwkv_w0, rwkv_w_up=rwkv_w_up, rwkv_a0=rwkv_a0,
                  rwkv_a_up=rwkv_a_up, rwkv_g_up=rwkv_g_up, rwkv_k_k=rwkv_k_k, rwkv_k_a=rwkv_k_a,
                  rwkv_r_k=rwkv_r_k, rwkv_ln_g=rwkv_ln_g, rwkv_ln_b=rwkv_ln_b,
                  s5_lam_re=s5_lam_re, s5_lam_im=s5_lam_im, s5_log_step=s5_log_step,
                  s5_b_re=s5_b_re, s5_b_im=s5_b_im, s5_c_re=s5_c_re, s5_c_im=s5_c_im,
                  s5_d=s5_d, s5_w_glu=s5_w_glu,
                  moe_router=moe_router, moe_bias=moe_bias, moe_w_gate_up=moe_w_gate_up,
                  moe_w_down=moe_w_down, shared_w_gate_up=shared_w_gate_up,
                  shared_w_down=shared_w_down)
    bp, lp_len, _ = x_prompt.shape
    bs, ls_len, _ = x_sample.shape
    segs = ((bp, lp_len, 0, 0), (bs, ls_len, 1, 1))
    xs = [x_prompt.reshape(bp * lp_len, D), x_sample.reshape(bs * ls_len, D)]
    cpad = jnp.zeros((16, D), F32).at[0].set(c_ctx).at[1:1 + bs].set(c)
    new_k, new_v, new_r, new_s = [], [], [], []
    n_slots = ((bp * lp_len + bs * ls_len) * TOPK // MOE_BM + NE) * MOE_BM
    x_sorted = jnp.zeros((n_slots, D // 2), U32)
    for l in range(DEPTH):
        lp = {name: arr[l] for name, arr in params.items() if name not in _WHOLE}
        whole = {name: params[name] for name in _WHOLE}
        lam_init = 0.8 - 0.6 * math.exp(-0.3 * l)
        ctxs = (None, (cache_attn_k[:, l], cache_attn_v[:, l], state_rwkv[:, l], state_s5[:, l]))
        xs, caches, x_sorted = _layer(xs, segs, cpad, lp, whole, l, lam_init, ctxs, final_norm,
                                      l == DEPTH - 1, x_sorted)
        ck, cv, cr, cs = caches[0]
        new_k.append(ck)
        new_v.append(cv)
        new_r.append(cr)
        new_s.append(cs)
    return (xs[0].reshape(bp, lp_len, D), xs[1].reshape(bs, ls_len, D),
            jnp.stack(new_k, axis=1), jnp.stack(new_v, axis=1),
            jnp.stack(new_r, axis=1), jnp.stack(new_s, axis=1))


_WHOLE = ('w_mod', 'moe_w_gate_up', 'moe_w_down')


def _layer(xs, segs, cpad, lp, whole, layer, lam_init, ctxs, final_norm, final, x_sorted):
    hh = jnp.arange(RW) // QK
    segmat = (hh[:, None] == hh[None, :]).astype(F32)
    perm = jnp.concatenate([jnp.arange(0, 3072), jnp.arange(3072 + RW_PROJ, PROJ_W),
                            jnp.arange(3072, 3072 + RW_PROJ)])
    mods6 = _modulation(cpad, whole['w_mod'], lp['b_mod'], layer).reshape(16 * 6, 1, D)
    w_in_bf = lp['w_in'][:, perm].astype(BF16)
    w_out_bf = lp['w_out'].astype(BF16)
    wb, wc, ab = _s5_params(lp)
    rt = lp['moe_router'].T
    rt_hi = rt.astype(BF16)
    router_t2 = jnp.stack([rt_hi, (rt - rt_hi.astype(F32)).astype(BF16)], axis=0)
    x1s, h2ps, tws, tis, caches = [], [], [], [], []
    for si, seg in enumerate(segs):
        B, L = seg[0], seg[1]
        proj = _in_proj(xs[si], lp['norm1'], mods6, w_in_bf, seg)
        proj3 = proj.reshape(L, B, PROJ_STRIDE)
        if ctxs[si] is None:
            o_att = _attention(proj, lp['att_lambda'], lp['att_subln'], lam_init, seg)
            s0_rwkv = None
            x0 = jnp.zeros((2, 2, B, S5S), F32)
        else:
            ck, cv, s0_rwkv, s0_s5 = ctxs[si]
            ctx = (ck.reshape(B, -1, ATT_W), cv.reshape(B, -1, ATT_W))
            o_att = _attention(proj, lp['att_lambda'], lp['att_subln'], lam_init, seg, ctx,
                               _rope_tables(L))
            x0 = s0_s5.astype(F32).reshape(B, 2, 2, S5S).transpose(1, 2, 0, 3)
        o_sum, shared, misc, s_fin = _rwkv_mix(proj, lp, segmat, seg, s0_rwkv)
        y_s5, xf = _s5_scan(proj3, wb, wc, ab, x0, seg)
        o_cat = _mix_post(o_att, o_sum, shared, misc, y_s5, proj, segmat, lp, seg)
        x1, h2p, tw, ti = _out_proj(o_cat, xs[si], w_out_bf, mods6, lp['norm2'],
                                    router_t2, lp['moe_bias'], seg)
        x1s.append(x1)
        h2ps.append(h2p)
        tws.append(tw)
        tis.append(ti)
        caches.append((proj3[:, :, COL_K:COL_K + ATT_W].transpose(1, 0, 2).reshape(B, L, HEADS, 2, QK),
                       proj3[:, :, COL_V:COL_V + ATT_W].transpose(1, 0, 2).reshape(B, L, HEADS, 2 * QK),
                       s_fin,
                       xf.transpose(2, 0, 1, 3).reshape(B, 2, 2, S5G, S5N)))
    h2p_all = jnp.concatenate(h2ps, axis=0)
    top_w = jnp.concatenate(tws, axis=1).T
    slot_of, block_e, n_used = _routing_tables(jnp.concatenate(tis, axis=1).T)
    x_sorted = _moe_dispatch(slot_of, h2p_all, x_sorted)
    y_slots = _moe_experts(block_e, n_used, x_sorted, whole['moe_w_gate_up'], whole['moe_w_down'], layer)
    w_sgu_bf = lp['shared_w_gate_up'].astype(BF16)
    w_sdn_bf = lp['shared_w_down'].astype(BF16)
    outs = []
    row0 = 0
    for si, seg in enumerate(segs):
        outs.append(_combine(slot_of, top_w, y_slots, h2p_all, x1s[si], mods6, w_sgu_bf, w_sdn_bf,
                             final_norm, seg, row0, final))
        row0 += seg[0] * seg[1]
    return outs, caches, x_sorted
```

```python
import functools
import math

import jax
import jax.numpy as jnp
from jax import lax
from jax.experimental import pallas as pl
from jax.experimental.pallas import tpu as pltpu

F32 = jnp.float32
BF16 = jnp.bfloat16
U32 = jnp.uint32

D = 2048
DEPTH = 2
EPS = 1e-6
GRID_W = 64
ROPE_THETA = 10000.0
HEADS = 8
QK = 64
ATT_W = 1024
RW = 512
RW_PROJ = 1792
S5W = 512
S5G = 32
S5N = 64
S5C = 16
S5S = S5G * S5N
S5_PARTS = 4
S5_PW = S5W // S5_PARTS
S5_PS = S5S // S5_PARTS
GN_EPS = 64e-5
NE = 64
TOPK = 8
EDIM = 512
ROUTE_SCALE = 2.5
PROJ_W = 5376
PROJ_STRIDE = 7168
COL_Q, COL_K, COL_V, COL_U, COL_Z = 0, 1024, 2048, 3072, 3584
PROJ_TN = 1792
MOE_BM = 512
MOE_SPLIT = 2
ATT_TQS = 128
ATT_TK = 256
VMEM_LIMIT = 56 * 1024 * 1024


def _cp(sem):
    return pltpu.CompilerParams(dimension_semantics=sem, vmem_limit_bytes=VMEM_LIMIT)


def _sigmoid(x):
    return 1.0 / (1.0 + jnp.exp(-x))


def _dot(a, b):
    return jnp.dot(a, b, preferred_element_type=F32)


def _dot_nt(a, b):
    return lax.dot_general(a, b, (((1,), (1,)), ((), ())), preferred_element_type=F32)


def _split_bf16(x):
    hi = x.astype(BF16)
    return hi, (x - hi.astype(F32)).astype(BF16)


def _dot_hi(a, b):
    a_hi, a_lo = _split_bf16(a)
    b_hi, b_lo = _split_bf16(b)
    return _dot(a_hi, b_hi) + (_dot(a_hi, b_lo) + _dot(a_lo, b_hi))


def _segsum(x, seg_bf):
    x_hi, x_lo = _split_bf16(x)
    return _dot(x_hi, seg_bf) + _dot(x_lo, seg_bf)


def _bf16_bits(x):
    b = lax.bitcast_convert_type(x, U32)
    return b + jnp.uint32(0x7FFF) + ((b >> 16) & jnp.uint32(1))


def _pack_bf16_pair(lo, hi):
    return (_bf16_bits(lo) >> 16) | (_bf16_bits(hi) & jnp.uint32(0xFFFF0000))


def _unpack_bf16_pair(p):
    lo = lax.bitcast_convert_type(p << 16, F32)
    hi = lax.bitcast_convert_type(p & jnp.uint32(0xFFFF0000), F32)
    return lo, hi


def _mod_kernel(c_ref, w_ref, b_ref, o_ref):
    c = c_ref[...]
    s = c * _sigmoid(c)
    o_ref[...] = _dot(s.astype(BF16), w_ref[0].astype(BF16)) + b_ref[...]


def _modulation(cpad, w_mod_all, b_mod, layer):
    tn = 1024
    return pl.pallas_call(
        _mod_kernel,
        grid=(6 * D // tn,),
        in_specs=[pl.BlockSpec((16, D), lambda j: (0, 0)),
                  pl.BlockSpec((1, D, tn), lambda j: (layer, 0, j)),
                  pl.BlockSpec((1, tn), lambda j: (0, j))],
        out_specs=pl.BlockSpec((16, tn), lambda j: (0, j)),
        out_shape=jax.ShapeDtypeStruct((16, 6 * D), F32),
        compiler_params=_cp(("arbitrary",)),
        name="modulation",
    )(cpad, w_mod_all, b_mod.reshape(1, 6 * D))


def _in_kernel(x_ref, g_ref, sc_ref, sh_ref, w_ref, o_ref, h_ref):
    @pl.when(pl.program_id(2) == 0)
    def _():
        x = x_ref[...]
        y = x * lax.rsqrt(jnp.mean(x * x, axis=-1, keepdims=True) + EPS)
        h = (y * g_ref[...]) * (1.0 + sc_ref[0]) + sh_ref[0]
        h_ref[...] = h.astype(BF16)

    o_ref[...] = _dot(h_ref[...], w_ref[...])


def _in_proj(x, norm_g, mods6, w_in_bf, seg):
    B, L, mod_base, mod_stride = seg
    tm = min(L, 512)
    nt = L // tm
    ncol = PROJ_W // PROJ_TN
    nstride = PROJ_STRIDE // PROJ_TN

    def mrow(j):
        return lambda b, t, c: ((mod_base + b * mod_stride) * 6 + j, 0, 0)

    return pl.pallas_call(
        _in_kernel,
        grid=(B, nt, ncol),
        in_specs=[pl.BlockSpec((tm, D), lambda b, t, c: (b * nt + t, 0)),
                  pl.BlockSpec((1, D), lambda b, t, c: (0, 0)),
                  pl.BlockSpec((1, 1, D), mrow(1)),
                  pl.BlockSpec((1, 1, D), mrow(0)),
                  pl.BlockSpec((D, PROJ_TN), lambda b, t, c: (0, c))],
        out_specs=pl.BlockSpec((tm, PROJ_TN), lambda b, t, c: (t, b * nstride + c)),
        out_shape=jax.ShapeDtypeStruct((L, B * PROJ_STRIDE), F32),
        scratch_shapes=[pltpu.VMEM((tm, D), BF16)],
        compiler_params=_cp(("arbitrary", "arbitrary", "arbitrary")),
        name="in_proj",
    )(x, norm_g.reshape(1, D), mods6, mods6, w_in_bf)


def _rope(x, c, s):
    lane = lax.broadcasted_iota(jnp.int32, x.shape, 1)
    nxt = pltpu.roll(x, 127, 1)
    prv = pltpu.roll(x, 1, 1)
    swapped = jnp.where((lane & 1) == 0, nxt, prv)
    return x * c + swapped * s


def _attn_kernel(*refs, lam_init, n_ctx, rope, hb):
    if rope:
        (q_ref, k_ref, v_ref, ck_ref, cv_ref, cq_ref, sq_ref, ckk_ref, skk_ref,
         lam_ref, g_ref, o_ref, kall_ref, vall_ref) = refs
    else:
        q_ref, k_ref, v_ref, lam_ref, g_ref, o_ref, kall_ref, vall_ref = refs

    @pl.when(pl.program_id(2) == 0)
    def _():
        for hh in range(hb):
            cols = slice(hh * 128, (hh + 1) * 128)
            k = k_ref[:, cols]
            if rope:
                k = _rope(k, ckk_ref[...], skk_ref[...])
                kall_ref[hh, 0:n_ctx, :] = ck_ref[0, :, cols].astype(BF16)
                vall_ref[hh, 0:n_ctx, 0:128] = cv_ref[0, :, cols].astype(BF16)
            kall_ref[hh, n_ctx:, :] = k.astype(BF16)
            vall_ref[hh, n_ctx:, 0:128] = v_ref[:, cols].astype(BF16)
            vall_ref[hh, :, 128:256] = jnp.ones((vall_ref.shape[1], 128), BF16)

    lv = lam_ref[...]
    lam = (jnp.exp(jnp.sum(lv[0:1] * lv[1:2], axis=-1, keepdims=True))
           - jnp.exp(jnp.sum(lv[2:3] * lv[3:4], axis=-1, keepdims=True)) + lam_init)
    n_kt = kall_ref.shape[1] // ATT_TK
    lane = lax.broadcasted_iota(jnp.int32, (ATT_TQS, 128), 1)
    for hh in range(hb):
        cols = slice(hh * 128, (hh + 1) * 128)
        q = q_ref[:, cols]
        if rope:
            q = _rope(q, cq_ref[...], sq_ref[...])
        q = q * (QK ** -0.5 * math.log2(math.e))
        for qs in range(q.shape[0] // ATT_TQS):
            rows = slice(qs * ATT_TQS, (qs + 1) * ATT_TQS)
            outs = []
            for m in range(2):
                qm = jnp.where((lane < QK) == (m == 0), q[rows], 0.0).astype(BF16)
                macc = _dot_nt(qm, kall_ref[hh, 0:ATT_TK, :])
                for kt in range(1, n_kt):
                    macc = jnp.maximum(macc, _dot_nt(qm, kall_ref[hh, kt * ATT_TK:(kt + 1) * ATT_TK, :]))
                mx = jnp.max(macc, axis=-1, keepdims=True)
                acc = jnp.zeros((ATT_TQS, 256), F32)
                for kt in range(n_kt):
                    keys = slice(kt * ATT_TK, (kt + 1) * ATT_TK)
                    e = jnp.exp2(_dot_nt(qm, kall_ref[hh, keys, :]) - mx).astype(BF16)
                    acc = acc + _dot(e, vall_ref[hh, keys, :])
                outs.append(acc[:, 0:128] / acc[:, 128:129])
            o = outs[0] - lam * outs[1]
            o = o * lax.rsqrt(jnp.mean(o * o, axis=-1, keepdims=True) + EPS) * g_ref[...]
            o_ref[rows, cols] = o * (1.0 - lam_init)


def _attention(proj, att_lambda, subln, lam_init, seg, ctx=None, tables=None):
    B, L = seg[0], seg[1]
    tq = min(L, 512)
    nq = L // tq
    rope = ctx is not None
    n_ctx = ctx[0].shape[1] if rope else 0
    hb = 1 if rope else HEADS
    bw = 128 * hb
    cs = PROJ_STRIDE // bw
    specs = [pl.BlockSpec((tq, bw), lambda b, h, t: (t, b * cs + COL_Q // bw + h)),
             pl.BlockSpec((L, bw), lambda b, h, t: (0, b * cs + COL_K // bw + h)),
             pl.BlockSpec((L, bw), lambda b, h, t: (0, b * cs + COL_V // bw + h))]
    args = [proj, proj, proj]
    if rope:
        cos_t, sin_t = tables
        specs += [pl.BlockSpec((1, n_ctx, bw), lambda b, h, t: (b, 0, h)),
                  pl.BlockSpec((1, n_ctx, bw), lambda b, h, t: (b, 0, h)),
                  pl.BlockSpec((tq, 128), lambda b, h, t: (t, 0)),
                  pl.BlockSpec((tq, 128), lambda b, h, t: (t, 0)),
                  pl.BlockSpec((L, 128), lambda b, h, t: (0, 0)),
                  pl.BlockSpec((L, 128), lambda b, h, t: (0, 0))]
        args += [ctx[0], ctx[1], cos_t, sin_t, cos_t, sin_t]
    specs += [pl.BlockSpec((4, QK), lambda b, h, t: (0, 0)),
              pl.BlockSpec((1, 128), lambda b, h, t: (0, 0))]
    args += [att_lambda, subln.reshape(1, 128)]
    return pl.pallas_call(
        functools.partial(_attn_kernel, lam_init=lam_init, n_ctx=n_ctx, rope=rope, hb=hb),
        grid=(B, HEADS // hb, nq),
        in_specs=specs,
        out_specs=pl.BlockSpec((tq, bw), lambda b, h, t: (b * nq + t, h)),
        out_shape=jax.ShapeDtypeStruct((B * L, ATT_W), F32),
        scratch_shapes=[pltpu.VMEM((hb, n_ctx + L, 128), BF16), pltpu.VMEM((hb, n_ctx + L, 256), BF16)],
        compiler_params=_cp(("arbitrary", "arbitrary", "arbitrary")),
        name="diff_attention",
    )(*args)


def _rwkv_pre_kernel(z_ref, zp_ref, zn_ref, mu_ref, seg_ref, wup_ref, aup_ref, gup_ref,
                     w0_ref, a0_ref, kk_ref, ka_ref, sh_ref, pd_ref, ms_ref, *, tm):
    t = pl.program_id(1)
    nt = pl.num_programs(1)
    z = z_ref[...]
    row = lax.broadcasted_iota(jnp.int32, z.shape, 0)
    prev_row = jnp.where(t > 0, zp_ref[7:8, :], 0.0)
    next_row = jnp.where(t < nt - 1, zn_ref[0:1, :], 0.0)
    zp = jnp.where(row == 0, prev_row, pltpu.roll(z, 1, 0))
    zn = jnp.where(row == tm - 1, next_row, pltpu.roll(z, tm - 1, 0))
    zs = z + mu_ref[...] * (0.5 * (zp + zn) - z)
    r = zs[:, 0:RW]
    k = zs[:, RW:2 * RW]
    v = zs[:, 2 * RW:3 * RW]
    wa = zs[:, 3 * RW:3 * RW + 128]
    gl = zs[:, 3 * RW + 128:3 * RW + 256]
    lane = lax.broadcasted_iota(jnp.int32, wa.shape, 1)
    wa = jnp.where(lane < 64, jnp.tanh(wa), wa)
    g = _dot_hi(_sigmoid(gl), gup_ref[...])
    kk = k * kk_ref[...]
    kk = kk * lax.rsqrt(_segsum(kk * kk, seg_ref[...]) + EPS)
    sh_ref[0] = r
    sh_ref[1] = v
    sh_ref[2] = kk
    kb = jnp.zeros_like(k)
    for d in range(2):
        xw = w0_ref[d] + _dot_hi(wa, wup_ref[d])
        w = jnp.exp(-math.exp(-0.5) * _sigmoid(xw))
        a = _sigmoid(a0_ref[d] + _dot_hi(wa, aup_ref[d]))
        kd = k * (1.0 + (a - 1.0) * ka_ref[...])
        pd_ref[0, d] = w
        pd_ref[1, d] = kd
        pd_ref[2, d] = kk * a
        kb = kb + kd
    ms_ref[0] = g
    ms_ref[1] = kb


def _rwkv_pre(proj, lp, segmat, seg):
    B, L = seg[0], seg[1]
    tm = min(L, 256)
    nt = L // tm
    zs = PROJ_STRIDE // RW_PROJ
    zc = COL_Z // RW_PROJ
    nb8 = L // 8
    zeros64 = jnp.zeros((2, 64, RW), F32)
    wup = jnp.concatenate([lp['rwkv_w_up'], zeros64], axis=1)
    aup = jnp.concatenate([zeros64, lp['rwkv_a_up']], axis=1)
    c2 = lambda b, t: (0, 0)
    c3 = lambda b, t: (0, 0, 0)
    return pl.pallas_call(
        functools.partial(_rwkv_pre_kernel, tm=tm),
        grid=(B, nt),
        in_specs=[pl.BlockSpec((tm, RW_PROJ), lambda b, t: (t, b * zs + zc)),
                  pl.BlockSpec((8, RW_PROJ), lambda b, t: (jnp.maximum(t * (tm // 8) - 1, 0), b * zs + zc)),
                  pl.BlockSpec((8, RW_PROJ), lambda b, t: (jnp.minimum((t + 1) * (tm // 8), nb8 - 1), b * zs + zc)),
                  pl.BlockSpec((1, RW_PROJ), c2),
                  pl.BlockSpec((RW, RW), c2),
                  pl.BlockSpec((2, 128, RW), c3),
                  pl.BlockSpec((2, 128, RW), c3),
                  pl.BlockSpec((128, RW), c2),
                  pl.BlockSpec((2, 1, RW), c3),
                  pl.BlockSpec((2, 1, RW), c3),
                  pl.BlockSpec((1, RW), c2),
                  pl.BlockSpec((1, RW), c2)],
        out_specs=[pl.BlockSpec((3, tm, RW), lambda b, t: (0, t, b)),
                   pl.BlockSpec((3, 2, tm, RW), lambda b, t: (0, 0, t, b)),
                   pl.BlockSpec((2, tm, RW), lambda b, t: (0, t, b))],
        out_shape=[jax.ShapeDtypeStruct((3, L, B * RW), F32),
                   jax.ShapeDtypeStruct((3, 2, L, B * RW), F32),
                   jax.ShapeDtypeStruct((2, L, B * RW), F32)],
        compiler_params=_cp(("arbitrary", "arbitrary")),
        name="rwkv_pre",
    )(proj, proj, proj, lp['rwkv_mu'].reshape(1, RW_PROJ), segmat, wup, aup, lp['rwkv_g_up'],
      lp['rwkv_w0'].reshape(2, 1, RW), lp['rwkv_a0'].reshape(2, 1, RW),
      lp['rwkv_k_k'].reshape(1, RW), lp['rwkv_k_a'].reshape(1, RW))


def _wkv_kernel(xa_ref, xb_ref, s0_ref, yf_ref, yr_ref, sf_ref, st_ref, *, tc):
    c = pl.program_id(1)

    @pl.when(c == 0)
    def _():
        st_ref[...] = s0_ref[...]

    fwd = lax.broadcasted_iota(jnp.int32, (QK, 128), 1) < 64

    def step(t, carry):
        tr = tc - 1 - t

        def tile(q):
            return jnp.where(fwd, xa_ref[q, t], xb_ref[q, tr])

        r_t = tile(0)
        v_t = tile(1)
        kk_t = tile(2)
        w_t = tile(3)
        kd_t = tile(4)
        b_t = tile(5)
        wr = w_t * r_t
        kr = jnp.sum(kd_t * r_t, axis=0, keepdims=True)
        br = jnp.sum(b_t * r_t, axis=0, keepdims=True)
        for g in range(QK // 8):
            ys = []
            for j in range(8):
                vi = g * 8 + j
                s = st_ref[vi]
                sa = jnp.sum(s * kk_t, axis=0, keepdims=True)
                y0 = jnp.sum(s * wr, axis=0, keepdims=True)
                vv = v_t[vi:vi + 1]
                st_ref[vi] = s * w_t + (vv * kd_t - sa * b_t)
                ys.append(y0 + vv * kr - sa * br)
            ytile = jnp.concatenate(ys, axis=0)
            yf_ref[t, g * 8:(g + 1) * 8, :] = ytile
            yr_ref[tr, g * 8:(g + 1) * 8, :] = ytile
        return carry

    lax.fori_loop(0, tc, step, 0)

    @pl.when(c == pl.num_programs(1) - 1)
    def _():
        sf_ref[...] = st_ref[...]


def _wkv_scan(xs, s0):
    _, L, _, lanes = xs.shape
    tc = 16
    nc = L // tc
    return pl.pallas_call(
        functools.partial(_wkv_kernel, tc=tc),
        grid=(lanes // 128, nc),
        in_specs=[pl.BlockSpec((6, tc, QK, 128), lambda g, c: (0, c, 0, g)),
                  pl.BlockSpec((6, tc, QK, 128), lambda g, c: (0, nc - 1 - c, 0, g)),
                  pl.BlockSpec((QK, QK, 128), lambda g, c: (0, 0, g))],
        out_specs=[pl.BlockSpec((tc, QK, 128), lambda g, c: (c, 0, g)),
                   pl.BlockSpec((tc, QK, 128), lambda g, c: (nc - 1 - c, 0, g)),
                   pl.BlockSpec((QK, QK, 128), lambda g, c: (0, 0, g))],
        out_shape=[jax.ShapeDtypeStruct((L, QK, lanes), F32),
                   jax.ShapeDtypeStruct((L, QK, lanes), F32),
                   jax.ShapeDtypeStruct((QK, QK, lanes), F32)],
        scratch_shapes=[pltpu.VMEM((QK, QK, 128), F32)],
        compiler_params=_cp(("arbitrary", "arbitrary")),
        name="wkv7_scan",
    )(xs, xs, s0)


def _rwkv_mix(proj, lp, segmat, seg, s0_bdhvk):
    B, L = seg[0], seg[1]
    G = B // 8
    shared, perdir, misc = _rwkv_pre(proj, lp, segmat, seg)
    sh = shared.reshape(3, L, G, 8, HEADS, QK)
    pd = perdir.reshape(3, 2, L, G, 8, HEADS, QK)
    full = jnp.concatenate([jnp.stack([sh, sh], axis=1), pd], axis=0)
    xs = full.transpose(0, 2, 6, 3, 1, 4, 5).reshape(6, L, QK, G * 128)
    if s0_bdhvk is None:
        s0 = jnp.zeros((QK, QK, G * 128), F32)
    else:
        s0 = (s0_bdhvk.astype(F32).reshape(G, 8, 2, HEADS, QK, QK)
              .transpose(4, 5, 0, 2, 1, 3).reshape(QK, QK, G * 128))
    yf, yr, sf = _wkv_scan(xs, s0)
    y = (yf.reshape(L, QK, G, 2, 8, HEADS)[:, :, :, 0] + yr.reshape(L, QK, G, 2, 8, HEADS)[:, :, :, 1])
    o_sum = y.transpose(0, 2, 3, 4, 1).reshape(L, B * RW)
    s_fin = (sf.reshape(QK, QK, G, 2, 8, HEADS).transpose(2, 4, 3, 5, 0, 1)
             .reshape(B, 2, HEADS, QK, QK))
    return o_sum, shared, misc, s_fin


def _s5_kernel(u_ref, wb_ref, wc_ref, ab_ref, x0_ref, y_ref, xf_ref, bx_ref, st_ref, *, tc, nb):
    d = pl.program_id(0)
    c = pl.program_id(1)

    @pl.when(c == 0)
    def _():
        st_ref[...] = x0_ref[0]

    u = u_ref[...].reshape(tc * nb, S5W).astype(BF16)
    for j in range(S5_PARTS):
        bj = _dot(u[:, j * S5_PW:(j + 1) * S5_PW], wb_ref[0, j])
        bx_ref[:, j * S5_PS:(j + 1) * S5_PS] = bj[:, 0:S5_PS]
        bx_ref[:, S5S + j * S5_PS:S5S + (j + 1) * S5_PS] = bj[:, S5_PS:2 * S5_PS]
    ar = jnp.broadcast_to(ab_ref[0, 0:1, :], (nb, S5S))
    ai = jnp.broadcast_to(ab_ref[0, 1:2, :], (nb, S5S))

    def step(i, carry):
        tt = jnp.where(d == 0, i, tc - 1 - i)
        rows = pl.ds(pl.multiple_of(tt * nb, nb), nb)
        xr = st_ref[0]
        xi = st_ref[1]
        nr = ar * xr - ai * xi + bx_ref[rows, 0:S5S]
        ni = ar * xi + ai * xr + bx_ref[rows, S5S:2 * S5S]
        st_ref[0] = nr
        st_ref[1] = ni
        bx_ref[rows, 0:S5S] = nr
        bx_ref[rows, S5S:2 * S5S] = ni
        return carry

    lax.fori_loop(0, tc, step, 0)
    ys = []
    for j in range(S5_PARTS):
        xr = bx_ref[:, j * S5_PS:(j + 1) * S5_PS].astype(BF16)
        xi = bx_ref[:, S5S + j * S5_PS:S5S + (j + 1) * S5_PS].astype(BF16)
        ys.append(_dot(xr, wc_ref[j, 0:S5_PS, :]) + _dot(xi, wc_ref[j, S5_PS:2 * S5_PS, :]))
    y_ref[0] = jnp.concatenate(ys, axis=-1).reshape(tc, nb, S5W)

    @pl.when(c == pl.num_programs(1) - 1)
    def _():
        xf_ref[0] = st_ref[...]


def _s5_scan(proj3, wb, wc, ab, x0, seg):
    B, L = seg[0], seg[1]
    tc = 64 if B <= 8 else 16
    nc = L // tc
    tmap = lambda d, c: jnp.where(d == 0, c, nc - 1 - c)
    return pl.pallas_call(
        functools.partial(_s5_kernel, tc=tc, nb=B),
        grid=(2, nc),
        in_specs=[pl.BlockSpec((tc, B, S5W), lambda d, c: (tmap(d, c), 0, COL_U // S5W)),
                  pl.BlockSpec((1, S5_PARTS, S5_PW, 2 * S5_PS), lambda d, c: (d, 0, 0, 0)),
                  pl.BlockSpec((S5_PARTS, 2 * S5_PS, S5_PW), lambda d, c: (0, 0, 0)),
                  pl.BlockSpec((1, 2, S5S), lambda d, c: (d, 0, 0)),
                  pl.BlockSpec((1, 2, B, S5S), lambda d, c: (d, 0, 0, 0))],
        out_specs=[pl.BlockSpec((1, tc, B, S5W), lambda d, c: (d, tmap(d, c), 0, 0)),
                   pl.BlockSpec((1, 2, B, S5S), lambda d, c: (d, 0, 0, 0))],
        out_shape=[jax.ShapeDtypeStruct((2, L, B, S5W), F32),
                   jax.ShapeDtypeStruct((2, 2, B, S5S), F32)],
        scratch_shapes=[pltpu.VMEM((tc * B, 2 * S5S), F32), pltpu.VMEM((2, B, S5S), F32)],
        compiler_params=_cp(("arbitrary", "arbitrary")),
        name="s5_scan",
    )(proj3, wb, wc, ab, x0)


def _s5_params(lp):
    lam_re, lam_im = lp['s5_lam_re'], lp['s5_lam_im']
    dt = jnp.exp(lp['s5_log_step'])[:, :, None]
    mag = jnp.exp(lam_re * dt)
    ab_re, ab_im = mag * jnp.cos(lam_im * dt), mag * jnp.sin(lam_im * dt)
    den = lam_re * lam_re + lam_im * lam_im
    f_re = ((ab_re - 1.0) * lam_re + ab_im * lam_im) / den
    f_im = (ab_im * lam_re - (ab_re - 1.0) * lam_im) / den
    b_re, b_im = lp['s5_b_re'], lp['s5_b_im']
    wre = f_re[..., None] * b_re[None] - f_im[..., None] * b_im[None]
    wim = f_re[..., None] * b_im[None] + f_im[..., None] * b_re[None]
    eye = jnp.eye(S5G, dtype=F32)

    def block_in(w):
        return jnp.einsum('dgnc,gh->dgchn', w, eye).reshape(2, S5W, S5S)

    wb_re, wb_im = block_in(wre), block_in(wim)
    wb = jnp.stack([jnp.concatenate([wb_re[:, j * S5_PW:(j + 1) * S5_PW, j * S5_PS:(j + 1) * S5_PS],
                                     wb_im[:, j * S5_PW:(j + 1) * S5_PW, j * S5_PS:(j + 1) * S5_PS]], axis=-1)
                    for j in range(S5_PARTS)], axis=1).astype(BF16)

    def block_out(cm):
        return jnp.einsum('gcn,gh->gnhc', cm, eye).reshape(S5S, S5W)

    wc_re, wc_im = block_out(lp['s5_c_re']), -block_out(lp['s5_c_im'])
    wc = jnp.stack([jnp.concatenate([wc_re[j * S5_PS:(j + 1) * S5_PS, j * S5_PW:(j + 1) * S5_PW],
                                     wc_im[j * S5_PS:(j + 1) * S5_PS, j * S5_PW:(j + 1) * S5_PW]], axis=0)
                    for j in range(S5_PARTS)], axis=0).astype(BF16)
    ab = jnp.stack([ab_re.reshape(2, S5S), ab_im.reshape(2, S5S)], axis=1)
    return wb, wc, ab


def _mix_kernel(oatt_ref, osum_ref, sh_ref, ms_ref, y_ref, u_ref, seg_ref, lng_ref, lnb_ref,
                rk_ref, d_ref, wglu_ref, o_ref):
    segm = seg_ref[...]
    o = osum_ref[...]
    mean = _segsum(o, segm) * (1.0 / QK)
    oc = o - mean
    var = _segsum(oc * oc, segm) * (1.0 / QK)
    o_n = oc * lax.rsqrt(var + GN_EPS) * lng_ref[...] + lnb_ref[...]
    r = sh_ref[0]
    v = sh_ref[1]
    bonus = _segsum(r * 0.5 * ms_ref[1] * rk_ref[...], segm) * v
    rw = (o_n + bonus) * ms_ref[0]
    u = u_ref[...]
    y = d_ref[...] * u + y_ref[0] + y_ref[1]
    hg = 0.5 * y * (1.0 + jnp.tanh(math.sqrt(2.0 / math.pi) * (y + 0.044715 * (y * y * y))))
    s5 = hg * _sigmoid(_dot(hg.astype(BF16), wglu_ref[...]))
    o_ref[:, 0:ATT_W] = oatt_ref[...].astype(BF16)
    o_ref[:, ATT_W:ATT_W + RW] = rw.astype(BF16)
    o_ref[:, ATT_W + RW:D] = s5.astype(BF16)


def _mix_post(o_att, o_sum, shared, misc, y_s5, proj, segmat, lp, seg):
    B, L = seg[0], seg[1]
    tm = min(L, 256)
    nt = L // tm
    us = PROJ_STRIDE // S5W
    c2 = lambda b, t: (0, 0)
    return pl.pallas_call(
        _mix_kernel,
        grid=(B, nt),
        in_specs=[pl.BlockSpec((tm, ATT_W), lambda b, t: (b * nt + t, 0)),
                  pl.BlockSpec((tm, RW), lambda b, t: (t, b)),
                  pl.BlockSpec((3, tm, RW), lambda b, t: (0, t, b)),
                  pl.BlockSpec((2, tm, RW), lambda b, t: (0, t, b)),
                  pl.BlockSpec((2, tm, S5W), lambda b, t: (0, t, b)),
                  pl.BlockSpec((tm, S5W), lambda b, t: (t, b * us + COL_U // S5W)),
                  pl.BlockSpec((RW, RW), c2),
                  pl.BlockSpec((1, RW), c2),
                  pl.BlockSpec((1, RW), c2),
                  pl.BlockSpec((1, RW), c2),
                  pl.BlockSpec((1, S5W), c2),
                  pl.BlockSpec((S5W, S5W), c2)],
        out_specs=pl.BlockSpec((tm, D), lambda b, t: (b * nt + t, 0)),
        out_shape=jax.ShapeDtypeStruct((B * L, D), BF16),
        compiler_params=_cp(("arbitrary", "arbitrary")),
        name="mix_post",
    )(o_att, o_sum, shared, misc, y_s5.reshape(2, L, B * S5W), proj, segmat,
      lp['rwkv_ln_g'].reshape(1, RW), lp['rwkv_ln_b'].reshape(1, RW), lp['rwkv_r_k'].reshape(1, RW),
      lp['s5_d'].reshape(1, S5W), lp['s5_w_glu'].astype(BF16))


def _out_kernel(oc_ref, x_ref, w_ref, g1_ref, sc_ref, sh_ref, n2_ref, rt_ref, rb_ref,
                x1_ref, hp_ref, tw_ref, ti_ref):
    x1 = x_ref[...] + g1_ref[0] * _dot(oc_ref[...], w_ref[...])
    x1_ref[...] = x1
    y = x1 * lax.rsqrt(jnp.mean(x1 * x1, axis=-1, keepdims=True) + EPS)
    h2 = (y * n2_ref[...]) * (1.0 + sc_ref[0]) + sh_ref[0]
    hp_ref[...] = _pack_bf16_pair(h2[:, 0:D // 2], h2[:, D // 2:D])
    h_hi = h2.astype(BF16)
    h_lo = (h2 - h_hi.astype(F32)).astype(BF16)
    r_hi = rt_ref[0]
    r_lo = rt_ref[1]
    logits = _dot_nt(r_hi, h_hi) + (_dot_nt(r_hi, h_lo) + _dot_nt(r_lo, h_hi))
    scores = _sigmoid(logits)
    sel = scores + rb_ref[...]
    row = lax.broadcasted_iota(jnp.int32, sel.shape, 0)
    tws, tis = [], []
    for k in range(TOPK):
        m = jnp.max(sel, axis=0, keepdims=True)
        idx = jnp.min(jnp.where(sel == m, row, NE), axis=0, keepdims=True)
        hit = row == idx
        tws.append(jnp.sum(jnp.where(hit, scores, 0.0), axis=0, keepdims=True))
        tis.append(idx)
        sel = jnp.where(hit, -jnp.inf, sel)
    tw = jnp.concatenate(tws, axis=0)
    tw_ref[...] = tw / jnp.sum(tw, axis=0, keepdims=True) * ROUTE_SCALE
    ti_ref[...] = jnp.concatenate(tis, axis=0)


def _out_proj(o_cat, x, w_out_bf, mods6, norm2, router_t2, moe_bias, seg):
    B, L, mod_base, mod_stride = seg
    tm = min(L, 256)
    nt = L // tm

    def mrow(j):
        return lambda i: ((mod_base + (i // nt) * mod_stride) * 6 + j, 0, 0)

    c2 = lambda i: (0, 0)
    n = B * L
    return pl.pallas_call(
        _out_kernel,
        grid=(n // tm,),
        in_specs=[pl.BlockSpec((tm, D), lambda i: (i, 0)),
                  pl.BlockSpec((tm, D), lambda i: (i, 0)),
                  pl.BlockSpec((D, D), c2),
                  pl.BlockSpec((1, 1, D), mrow(2)),
                  pl.BlockSpec((1, 1, D), mrow(4)),
                  pl.BlockSpec((1, 1, D), mrow(3)),
                  pl.BlockSpec((1, D), c2),
                  pl.BlockSpec((2, NE, D), lambda i: (0, 0, 0)),
                  pl.BlockSpec((NE, 1), c2)],
        out_specs=[pl.BlockSpec((tm, D), lambda i: (i, 0)),
                   pl.BlockSpec((tm, D // 2), lambda i: (i, 0)),
                   pl.BlockSpec((TOPK, tm), lambda i: (0, i)),
                   pl.BlockSpec((TOPK, tm), lambda i: (0, i))],
        out_shape=[jax.ShapeDtypeStruct((n, D), F32),
                   jax.ShapeDtypeStruct((n, D // 2), U32),
                   jax.ShapeDtypeStruct((TOPK, n), F32),
                   jax.ShapeDtypeStruct((TOPK, n), jnp.int32)],
        compiler_params=_cp(("arbitrary",)),
        name="out_proj",
    )(o_cat, x, w_out_bf, mods6, mods6, mods6, norm2.reshape(1, D), router_t2, moe_bias.reshape(NE, 1))


DISP_TM = 128


def _dispatch_copy(hp_ref, xs_ref, sem, r, dst_row):
    return pltpu.make_async_copy(hp_ref.at[pl.ds(r, 1)], xs_ref.at[pl.ds(dst_row, 1)], sem.at[0])


def _dispatch_kernel(idx_ref, hp_ref, xs_in_ref, xs_ref, sem):
    del xs_in_ref
    for r in range(DISP_TM):
        for k in range(TOPK):
            _dispatch_copy(hp_ref, xs_ref, sem, r, idx_ref[0, 0, r * TOPK + k]).start(priority=k % 2)
    for r in range(DISP_TM):
        for k in range(TOPK):
            _dispatch_copy(hp_ref, xs_ref, sem, r, 0).wait()


def _moe_dispatch(slot_of, h2p, xs_init):
    n = slot_of.shape[0]
    tm = DISP_TM
    idx3 = slot_of.reshape(n // tm, 1, tm * TOPK)
    return pl.pallas_call(
        _dispatch_kernel,
        grid=(n // tm,),
        in_specs=[pl.BlockSpec((1, 1, tm * TOPK), lambda i: (i, 0, 0), memory_space=pltpu.SMEM),
                  pl.BlockSpec((tm, D // 2), lambda i: (i, 0)),
                  pl.BlockSpec(memory_space=pl.ANY)],
        out_specs=pl.BlockSpec(memory_space=pl.ANY),
        out_shape=jax.ShapeDtypeStruct(xs_init.shape, U32),
        scratch_shapes=[pltpu.SemaphoreType.DMA((1,))],
        input_output_aliases={2: 0},
        compiler_params=_cp(("arbitrary",)),
        name="moe_dispatch",
    )(idx3, h2p, xs_init)


def _moe_kernel(be_ref, nu_ref, x_ref, wgu_ref, wdn_ref, o_ref, wgu_bf, wdn_bf):
    i = pl.program_id(0)

    @pl.when(i < nu_ref[0])
    def _():
        @pl.when(jnp.logical_or(i == 0, be_ref[i] != be_ref[jnp.maximum(i - 1, 0)]))
        def _():
            wgu_bf[...] = wgu_ref[0, 0].astype(BF16)
            wdn_bf[...] = wdn_ref[0, 0].astype(BF16)

        for part in range(MOE_SPLIT):
            rows = slice(part * (MOE_BM // MOE_SPLIT), (part + 1) * (MOE_BM // MOE_SPLIT))
            lo, hi = _unpack_bf16_pair(x_ref[rows, :])
            h = (_dot(lo.astype(BF16), wgu_bf[0:D // 2, :])
                 + _dot(hi.astype(BF16), wgu_bf[D // 2:D, :]))
            gte = h[:, 0:EDIM]
            act = (gte * _sigmoid(gte)) * h[:, EDIM:2 * EDIM]
            y = _dot(act.astype(BF16), wdn_bf[...])
            o_ref[rows, :] = _pack_bf16_pair(y[:, 0:D // 2], y[:, D // 2:D])

    @pl.when(i >= nu_ref[0])
    def _():
        o_ref[...] = jnp.zeros(o_ref.shape, U32)


def _moe_experts(block_e, n_used, x_sorted, w_gu_all, w_dn_all, layer):
    n_slots = x_sorted.shape[0]
    nb = n_slots // MOE_BM
    grid_spec = pltpu.PrefetchScalarGridSpec(
        num_scalar_prefetch=2,
        grid=(nb,),
        in_specs=[pl.BlockSpec((MOE_BM, D // 2), lambda i, be, nu: (i, 0)),
                  pl.BlockSpec((1, 1, D, 2 * EDIM), lambda i, be, nu: (layer, be[i], 0, 0)),
                  pl.BlockSpec((1, 1, EDIM, D), lambda i, be, nu: (layer, be[i], 0, 0))],
        out_specs=pl.BlockSpec((MOE_BM, D // 2), lambda i, be, nu: (i, 0)),
        scratch_shapes=[pltpu.VMEM((D, 2 * EDIM), BF16),
                        pltpu.VMEM((EDIM, D), BF16)],
    )
    return pl.pallas_call(
        _moe_kernel,
        grid_spec=grid_spec,
        out_shape=jax.ShapeDtypeStruct((n_slots, D // 2), U32),
        compiler_params=_cp(("arbitrary",)),
        name="moe_experts",
    )(block_e, n_used, x_sorted, w_gu_all, w_dn_all)


COMB_TM = 128


def _comb_gather(idx_ref, y_hbm, ybuf, sem, slot):
    for r in range(COMB_TM):
        for k in range(TOPK):
            pltpu.make_async_copy(y_hbm.at[pl.ds(idx_ref[0, 0, r * TOPK + k], 1)],
                                  ybuf.at[slot, k, pl.ds(r, 1)], sem.at[slot]).start(priority=k % 2)


def _comb_wait(y_hbm, ybuf, sem, slot):
    for r in range(COMB_TM):
        for k in range(TOPK):
            pltpu.make_async_copy(y_hbm.at[pl.ds(0, 1)], ybuf.at[slot, k, pl.ds(r, 1)],
                                  sem.at[slot]).wait()


def _comb_kernel(idxc_ref, idxn_ref, y_hbm, hp_ref, tw_ref, x1_ref, g2_ref, wgu_ref, wdn_ref, fn_ref,
                 o_ref, ybuf, sem, *, final):
    i = pl.program_id(0)
    nb = pl.num_programs(0)
    slot = i % 2

    @pl.when(i == 0)
    def _():
        _comb_gather(idxc_ref, y_hbm, ybuf, sem, 0)

    _comb_gather(idxn_ref, y_hbm, ybuf, sem, 1 - slot)

    lo, hi = _unpack_bf16_pair(hp_ref[...])
    h = _dot(lo.astype(BF16), wgu_ref[0:D // 2, :]) + _dot(hi.astype(BF16), wgu_ref[D // 2:D, :])
    gte = h[:, 0:EDIM]
    act = (gte * _sigmoid(gte)) * h[:, EDIM:2 * EDIM]
    shared = _dot(act.astype(BF16), wdn_ref[...])

    _comb_wait(y_hbm, ybuf, sem, slot)
    rlo = jnp.zeros((COMB_TM, D // 2), F32)
    rhi = jnp.zeros((COMB_TM, D // 2), F32)
    tw = tw_ref[...]
    for k in range(TOPK):
        a, b = _unpack_bf16_pair(ybuf[slot, k])
        wk = tw[:, k:k + 1]
        rlo = rlo + wk * a
        rhi = rhi + wk * b
    g2 = g2_ref[0]
    x1 = x1_ref[...]
    out_lo = x1[:, 0:D // 2] + g2[:, 0:D // 2] * (rlo + shared[:, 0:D // 2])
    out_hi = x1[:, D // 2:D] + g2[:, D // 2:D] * (rhi + shared[:, D // 2:D])
    if final:
        ms = (jnp.sum(out_lo * out_lo, axis=-1, keepdims=True)
              + jnp.sum(out_hi * out_hi, axis=-1, keepdims=True)) * (1.0 / D)
        inv = lax.rsqrt(ms + EPS)
        fn = fn_ref[...]
        out_lo = out_lo * inv * fn[:, 0:D // 2]
        out_hi = out_hi * inv * fn[:, D // 2:D]
    o_ref[:, 0:D // 2] = out_lo
    o_ref[:, D // 2:D] = out_hi

    @pl.when(i == nb - 1)
    def _():
        _comb_wait(y_hbm, ybuf, sem, 1 - slot)


def _combine(slot_of, top_w, y_slots, h2p, x1, mods6, w_sgu_bf, w_sdn_bf, final_norm, seg, row0, final):
    B, L, mod_base, mod_stride = seg
    n = B * L
    tm = COMB_TM
    nb = n // tm
    nt = L // tm
    blk0 = row0 // tm
    idx3 = slot_of.reshape(-1, 1, tm * TOPK)
    c2 = lambda i: (0, 0)
    return pl.pallas_call(
        functools.partial(_comb_kernel, final=final),
        grid=(nb,),
        in_specs=[pl.BlockSpec((1, 1, tm * TOPK), lambda i: (blk0 + i, 0, 0), memory_space=pltpu.SMEM),
                  pl.BlockSpec((1, 1, tm * TOPK), lambda i: (blk0 + jnp.minimum(i + 1, nb - 1), 0, 0),
                               memory_space=pltpu.SMEM),
                  pl.BlockSpec(memory_space=pl.ANY),
                  pl.BlockSpec((tm, D // 2), lambda i: (blk0 + i, 0)),
                  pl.BlockSpec((tm, TOPK), lambda i: (blk0 + i, 0)),
                  pl.BlockSpec((tm, D), lambda i: (i, 0)),
                  pl.BlockSpec((1, 1, D), lambda i: ((mod_base + (i // nt) * mod_stride) * 6 + 5, 0, 0)),
                  pl.BlockSpec((D, 2 * EDIM), c2),
                  pl.BlockSpec((EDIM, D), c2),
                  pl.BlockSpec((1, D), c2)],
        out_specs=pl.BlockSpec((tm, D), lambda i: (i, 0)),
        out_shape=jax.ShapeDtypeStruct((n, D), F32),
        scratch_shapes=[pltpu.VMEM((2, TOPK, tm, D // 2), U32), pltpu.SemaphoreType.DMA((2,))],
        compiler_params=_cp(("arbitrary",)),
        name="moe_combine",
    )(idx3, idx3, y_slots, h2p, top_w, x1, mods6, w_sgu_bf, w_sdn_bf, final_norm.reshape(1, D))


def _routing_tables(top_i):
    n = top_i.shape[0]
    onehot = (top_i[:, :, None] == jnp.arange(NE, dtype=jnp.int32)[None, None, :])
    mask = jnp.any(onehot, axis=1).astype(jnp.int32)
    counts = jnp.sum(mask, axis=0)
    rank = jnp.cumsum(mask, axis=0) - mask
    padded = (counts + MOE_BM - 1) // MOE_BM * MOE_BM
    pad_end = jnp.cumsum(padded)
    pad_start = pad_end - padded
    slot_all = pad_start[None, :] + rank
    slot_of = jnp.take_along_axis(slot_all, top_i, axis=1).astype(jnp.int32)
    n_blocks = n * TOPK // MOE_BM + NE
    starts = jnp.arange(n_blocks, dtype=jnp.int32) * MOE_BM
    block_e = jnp.minimum(jnp.sum((pad_end[None, :] <= starts[:, None]).astype(jnp.int32), axis=1),
                          NE - 1).astype(jnp.int32)
    n_used = (pad_end[NE - 1:NE] // MOE_BM).astype(jnp.int32)
    return slot_of, block_e, n_used


def _rope_tables(n_tok):
    rows = n_tok // GRID_W
    row = jnp.repeat(jnp.arange(rows), GRID_W).astype(F32)
    col = jnp.tile(jnp.arange(GRID_W), rows).astype(F32)
    half = QK // 2
    inv = ROPE_THETA ** (-jnp.arange(0, half, 2, dtype=F32) / half)
    ang = jnp.concatenate([row[:, None] * inv, col[:, None] * inv], axis=-1)
    cos = jnp.repeat(jnp.cos(ang), 2, axis=-1)
    sin = jnp.repeat(jnp.sin(ang), 2, axis=-1)
    sign = jnp.tile(jnp.array([-1.0, 1.0], F32), QK // 2)
    return jnp.tile(cos, (1, 2)), jnp.tile(sin * sign, (1, 2))


def kernel(x_prompt, x_sample, cache_attn_k, cache_attn_v, state_rwkv, state_s5, c, c_ctx, w_mod, b_mod, norm1, norm2, w_in, w_out, att_lambda, att_subln, rwkv_mu, rwkv_w0, rwkv_w_up, rwkv_a0, rwkv_a_up, rwkv_g_up, rwkv_k_k, rwkv_k_a, rwkv_r_k, rwkv_ln_g, rwkv_ln_b, s5_lam_re, s5_lam_im, s5_log_step, s5_b_re, s5_b_im, s5_c_re, s5_c_im, s5_d, s5_w_glu, moe_router, moe_bias, moe_w_gate_up, moe_w_down, shared_w_gate_up, shared_w_down, final_norm):
    params = dict(w_mod=w_mod, b_mod=b_mod, norm1=norm1, norm2=norm2, w_in=w_in, w_out=w_out,
                  att_lambda=att_lambda, att_subln=att_subln,
                  rwkv_mu=rwkv_mu, rwkv_w0=rwkv_w0, rwkv_w_up=rwkv_w_up, rwkv_a0=rwkv_a0,
                  rwkv_a_up=rwkv_a_up, rwkv_g_up=rwkv_g_up, rwkv_k_k=rwkv_k_k, rwkv_k_a=rwkv_k_a,
                  rwkv_r_k=rwkv_r_k, rwkv_ln_g=rwkv_ln_g, rwkv_ln_b=rwkv_ln_b,
                  s5_lam_re=s5_lam_re, s5_lam_im=s5_lam_im, s5_log_step=s5_log_step,
                  s5_b_re=s5_b_re, s5_b_im=s5_b_im, s5_c_re=s5_c_re, s5_c_im=s5_c_im,
                  s5_d=s5_d, s5_w_glu=s5_w_glu,
                  moe_router=moe_router, moe_bias=moe_bias, moe_w_gate_up=moe_w_gate_up,
                  moe_w_down=moe_w_down, shared_w_gate_up=shared_w_gate_up,
                  shared_w_down=shared_w_down)
    bp, lp_len, _ = x_prompt.shape
    bs, ls_len, _ = x_sample.shape
    segs = ((bp, lp_len, 0, 0), (bs, ls_len, 1, 1))
    xs = [x_prompt.reshape(bp * lp_len, D), x_sample.reshape(bs * ls_len, D)]
    cpad = jnp.zeros((16, D), F32).at[0].set(c_ctx).at[1:1 + bs].set(c)
    new_k, new_v, new_r, new_s = [], [], [], []
    n_slots = ((bp * lp_len + bs * ls_len) * TOPK // MOE_BM + NE) * MOE_BM
    x_sorted = jnp.zeros((n_slots, D // 2), U32)
    for l in range(DEPTH):
        lp = {name: arr[l] for name, arr in params.items() if name not in _WHOLE}
        whole = {name: params[name] for name in _WHOLE}
        lam_init = 0.8 - 0.6 * math.exp(-0.3 * l)
        ctxs = (None, (cache_attn_k[:, l], cache_attn_v[:, l], state_rwkv[:, l], state_s5[:, l]))
        xs, caches, x_sorted = _layer(xs, segs, cpad, lp, whole, l, lam_init, ctxs, final_norm,
                                      l == DEPTH - 1, x_sorted)
        ck, cv, cr, cs = caches[0]
        new_k.append(ck)
        new_v.append(cv)
        new_r.append(cr)
        new_s.append(cs)
    return (xs[0].reshape(bp, lp_len, D), xs[1].reshape(bs, ls_len, D),
            jnp.stack(new_k, axis=1), jnp.stack(new_v, axis=1),
            jnp.stack(new_r, axis=1), jnp.stack(new_s, axis=1))


_WHOLE = ('w_mod', 'moe_w_gate_up', 'moe_w_down')


def _layer(xs, segs, cpad, lp, whole, layer, lam_init, ctxs, final_norm, final, x_sorted):
    hh = jnp.arange(RW) // QK
    segmat = (hh[:, None] == hh[None, :]).astype(BF16)
    perm = jnp.concatenate([jnp.arange(0, 3072), jnp.arange(3072 + RW_PROJ, PROJ_W),
                            jnp.arange(3072, 3072 + RW_PROJ)])
    mods6 = _modulation(cpad, whole['w_mod'], lp['b_mod'], layer).reshape(16 * 6, 1, D)
    w_in_bf = lp['w_in'][:, perm].astype(BF16)
    w_out_bf = lp['w_out'].astype(BF16)
    wb, wc, ab = _s5_params(lp)
    rt = lp['moe_router'].T
    rt_hi = rt.astype(BF16)
    router_t2 = jnp.stack([rt_hi, (rt - rt_hi.astype(F32)).astype(BF16)], axis=0)
    x1s, h2ps, tws, tis, caches = [], [], [], [], []
    for si, seg in enumerate(segs):
        B, L = seg[0], seg[1]
        proj = _in_proj(xs[si], lp['norm1'], mods6, w_in_bf, seg)
        proj3 = proj.reshape(L, B, PROJ_STRIDE)
        if ctxs[si] is None:
            o_att = _attention(proj, lp['att_lambda'], lp['att_subln'], lam_init, seg)
            s0_rwkv = None
            x0 = jnp.zeros((2, 2, B, S5S), F32)
        else:
            ck, cv, s0_rwkv, s0_s5 = ctxs[si]
            ctx = (ck.reshape(B, -1, ATT_W), cv.reshape(B, -1, ATT_W))
            o_att = _attention(proj, lp['att_lambda'], lp['att_subln'], lam_init, seg, ctx,
                               _rope_tables(L))
            x0 = s0_s5.astype(F32).reshape(B, 2, 2, S5S).transpose(1, 2, 0, 3)
        o_sum, shared, misc, s_fin = _rwkv_mix(proj, lp, segmat, seg, s0_rwkv)
        y_s5, xf = _s5_scan(proj3, wb, wc, ab, x0, seg)
        o_cat = _mix_post(o_att, o_sum, shared, misc, y_s5, proj, segmat, lp, seg)
        x1, h2p, tw, ti = _out_proj(o_cat, xs[si], w_out_bf, mods6, lp['norm2'],
                                    router_t2, lp['moe_bias'], seg)
        x1s.append(x1)
        h2ps.append(h2p)
        tws.append(tw)
        tis.append(ti)
        caches.append((proj3[:, :, COL_K:COL_K + ATT_W].transpose(1, 0, 2).reshape(B, L, HEADS, 2, QK),
                       proj3[:, :, COL_V:COL_V + ATT_W].transpose(1, 0, 2).reshape(B, L, HEADS, 2 * QK),
                       s_fin,
                       xf.transpose(2, 0, 1, 3).reshape(B, 2, 2, S5G, S5N)))
    h2p_all = jnp.concatenate(h2ps, axis=0)
    top_w = jnp.concatenate(tws, axis=1).T
    slot_of, block_e, n_used = _routing_tables(jnp.concatenate(tis, axis=1).T)
    x_sorted = _moe_dispatch(slot_of, h2p_all, x_sorted)
    y_slots = _moe_experts(block_e, n_used, x_sorted, whole['moe_w_gate_up'], whole['moe_w_down'], layer)
    w_sgu_bf = lp['shared_w_gate_up'].astype(BF16)
    w_sdn_bf = lp['shared_w_down'].astype(BF16)
    outs = []
    row0 = 0
    for si, seg in enumerate(segs):
        outs.append(_combine(slot_of, top_w, y_slots, h2p_all, x1s[si], mods6, w_sgu_bf, w_sdn_bf,
                             final_norm, seg, row0, final))
        row0 += seg[0] * seg[1]
    return outs, caches, x_sorted
```

```python
import functools
import math

import jax
import jax.numpy as jnp
from jax import lax
from jax.experimental import pallas as pl
from jax.experimental.pallas import tpu as pltpu

F32 = jnp.float32
BF16 = jnp.bfloat16
U32 = jnp.uint32

D = 2048
DEPTH = 2
EPS = 1e-6
GRID_W = 64
ROPE_THETA = 10000.0
HEADS = 8
QK = 64
ATT_W = 1024
RW = 512
RW_PROJ = 1792
S5W = 512
S5G = 32
S5N = 64
S5C = 16
S5S = S5G * S5N
S5_PARTS = 4
S5_PW = S5W // S5_PARTS
S5_PS = S5S // S5_PARTS
GN_EPS = 64e-5
NE = 64
TOPK = 8
EDIM = 512
ROUTE_SCALE = 2.5
PROJ_W = 5376
PROJ_STRIDE = 7168
COL_Q, COL_K, COL_V, COL_U, COL_Z = 0, 1024, 2048, 3072, 3584
PROJ_TN = 1792
MOE_BM = 512
MOE_SPLIT = 2
ATT_TQS = 128
ATT_TK = 256
VMEM_LIMIT = 56 * 1024 * 1024


def _cp(sem):
    return pltpu.CompilerParams(dimension_semantics=sem, vmem_limit_bytes=VMEM_LIMIT)


def _sigmoid(x):
    return 1.0 / (1.0 + jnp.exp(-x))


def _dot(a, b):
    return jnp.dot(a, b, preferred_element_type=F32)


def _dot_nt(a, b):
    return lax.dot_general(a, b, (((1,), (1,)), ((), ())), preferred_element_type=F32)


def _split_bf16(x):
    hi = x.astype(BF16)
    return hi, (x - hi.astype(F32)).astype(BF16)


def _dot_hi(a, b):
    a_hi, a_lo = _split_bf16(a)
    b_hi, b_lo = _split_bf16(b)
    return _dot(a_hi, b_hi) + (_dot(a_hi, b_lo) + _dot(a_lo, b_hi))


def _segsum(x, seg_bf):
    x_hi, x_lo = _split_bf16(x)
    return _dot(x_hi, seg_bf) + _dot(x_lo, seg_bf)


def _bf16_bits(x):
    b = lax.bitcast_convert_type(x, U32)
    return b + jnp.uint32(0x7FFF) + ((b >> 16) & jnp.uint32(1))


def _pack_bf16_pair(lo, hi):
    return (_bf16_bits(lo) >> 16) | (_bf16_bits(hi) & jnp.uint32(0xFFFF0000))


def _unpack_bf16_pair(p):
    lo = lax.bitcast_convert_type(p << 16, F32)
    hi = lax.bitcast_convert_type(p & jnp.uint32(0xFFFF0000), F32)
    return lo, hi


def _mod_kernel(c_ref, w_ref, b_ref, o_ref):
    c = c_ref[...]
    s = c * _sigmoid(c)
    o_ref[...] = _dot(s.astype(BF16), w_ref[0].astype(BF16)) + b_ref[...]


def _modulation(cpad, w_mod_all, b_mod, layer):
    tn = 1024
    return pl.pallas_call(
        _mod_kernel,
        grid=(6 * D // tn,),
        in_specs=[pl.BlockSpec((16, D), lambda j: (0, 0)),
                  pl.BlockSpec((1, D, tn), lambda j: (layer, 0, j)),
                  pl.BlockSpec((1, tn), lambda j: (0, j))],
        out_specs=pl.BlockSpec((16, tn), lambda j: (0, j)),
        out_shape=jax.ShapeDtypeStruct((16, 6 * D), F32),
        compiler_params=_cp(("arbitrary",)),
        name="modulation",
    )(cpad, w_mod_all, b_mod.reshape(1, 6 * D))


def _in_kernel(x_ref, g_ref, sc_ref, sh_ref, w_ref, o_ref):
    x = x_ref[...]
    y = x * lax.rsqrt(jnp.mean(x * x, axis=-1, keepdims=True) + EPS)
    h = (y * g_ref[...]) * (1.0 + sc_ref[0]) + sh_ref[0]
    o_ref[...] = _dot(h.astype(BF16), w_ref[...])


def _in_proj(x, norm_g, mods6, w_in_bf, seg):
    B, L, mod_base, mod_stride = seg
    tm = min(L, 512)
    nt = L // tm
    ncol = PROJ_W // PROJ_TN
    nstride = PROJ_STRIDE // PROJ_TN

    def mrow(j):
        return lambda c, i: ((mod_base + (i // nt) * mod_stride) * 6 + j, 0, 0)

    return pl.pallas_call(
        _in_kernel,
        grid=(ncol, B * nt),
        in_specs=[pl.BlockSpec((tm, D), lambda c, i: (i, 0)),
                  pl.BlockSpec((1, D), lambda c, i: (0, 0)),
                  pl.BlockSpec((1, 1, D), mrow(1)),
                  pl.BlockSpec((1, 1, D), mrow(0)),
                  pl.BlockSpec((D, PROJ_TN), lambda c, i: (0, c))],
        out_specs=pl.BlockSpec((tm, PROJ_TN), lambda c, i: (i % nt, (i // nt) * nstride + c)),
        out_shape=jax.ShapeDtypeStruct((L, B * PROJ_STRIDE), F32),
        compiler_params=_cp(("arbitrary", "arbitrary")),
        name="in_proj",
    )(x, norm_g.reshape(1, D), mods6, mods6, w_in_bf)


def _rope(x, c, s):
    lane = lax.broadcasted_iota(jnp.int32, x.shape, 1)
    nxt = pltpu.roll(x, 127, 1)
    prv = pltpu.roll(x, 1, 1)
    swapped = jnp.where((lane & 1) == 0, nxt, prv)
    return x * c + swapped * s


def _attn_kernel(*refs, lam_init, n_ctx, rope, hb):
    if rope:
        (q_ref, k_ref, v_ref, ck_ref, cv_ref, cq_ref, sq_ref, ckk_ref, skk_ref,
         lam_ref, g_ref, o_ref, kall_ref, vall_ref) = refs
    else:
        q_ref, k_ref, v_ref, lam_ref, g_ref, o_ref, kall_ref, vall_ref = refs

    @pl.when(pl.program_id(2) == 0)
    def _():
        for hh in range(hb):
            cols = slice(hh * 128, (hh + 1) * 128)
            k = k_ref[:, cols]
            if rope:
                k = _rope(k, ckk_ref[...], skk_ref[...])
                kall_ref[hh, 0:n_ctx, :] = ck_ref[0, :, cols].astype(BF16)
                vall_ref[hh, 0:n_ctx, 0:128] = cv_ref[0, :, cols].astype(BF16)
            kall_ref[hh, n_ctx:, :] = k.astype(BF16)
            vall_ref[hh, n_ctx:, 0:128] = v_ref[:, cols].astype(BF16)
            vall_ref[hh, :, 128:256] = jnp.ones((vall_ref.shape[1], 128), BF16)

    lv = lam_ref[...]
    lam = (jnp.exp(jnp.sum(lv[0:1] * lv[1:2], axis=-1, keepdims=True))
           - jnp.exp(jnp.sum(lv[2:3] * lv[3:4], axis=-1, keepdims=True)) + lam_init)
    n_kt = kall_ref.shape[1] // ATT_TK
    lane = lax.broadcasted_iota(jnp.int32, (ATT_TQS, 128), 1)
    for hh in range(hb):
        cols = slice(hh * 128, (hh + 1) * 128)
        q = q_ref[:, cols]
        if rope:
            q = _rope(q, cq_ref[...], sq_ref[...])
        q = q * (QK ** -0.5 * math.log2(math.e))
        for qs in range(q.shape[0] // ATT_TQS):
            rows = slice(qs * ATT_TQS, (qs + 1) * ATT_TQS)
            outs = []
            for m in range(2):
                qm = jnp.where((lane < QK) == (m == 0), q[rows], 0.0).astype(BF16)
                macc = _dot_nt(qm, kall_ref[hh, 0:ATT_TK, :])
                for kt in range(1, n_kt):
                    macc = jnp.maximum(macc, _dot_nt(qm, kall_ref[hh, kt * ATT_TK:(kt + 1) * ATT_TK, :]))
                mx = jnp.max(macc, axis=-1, keepdims=True)
                acc = jnp.zeros((ATT_TQS, 256), F32)
                for kt in range(n_kt):
                    keys = slice(kt * ATT_TK, (kt + 1) * ATT_TK)
                    e = jnp.exp2(_dot_nt(qm, kall_ref[hh, keys, :]) - mx).astype(BF16)
                    acc = acc + _dot(e, vall_ref[hh, keys, :])
                outs.append(acc[:, 0:128] / acc[:, 128:129])
            o = outs[0] - lam * outs[1]
            o = o * lax.rsqrt(jnp.mean(o * o, axis=-1, keepdims=True) + EPS) * g_ref[...]
            o_ref[rows, cols] = o * (1.0 - lam_init)


def _attention(proj, att_lambda, subln, lam_init, seg, ctx=None, tables=None):
    B, L = seg[0], seg[1]
    tq = min(L, 512)
    nq = L // tq
    rope = ctx is not None
    n_ctx = ctx[0].shape[1] if rope else 0
    hb = 1 if rope else HEADS
    bw = 128 * hb
    cs = PROJ_STRIDE // bw
    specs = [pl.BlockSpec((tq, bw), lambda b, h, t: (t, b * cs + COL_Q // bw + h)),
             pl.BlockSpec((L, bw), lambda b, h, t: (0, b * cs + COL_K // bw + h)),
             pl.BlockSpec((L, bw), lambda b, h, t: (0, b * cs + COL_V // bw + h))]
    args = [proj, proj, proj]
    if rope:
        cos_t, sin_t = tables
        specs += [pl.BlockSpec((1, n_ctx, bw), lambda b, h, t: (b, 0, h)),
                  pl.BlockSpec((1, n_ctx, bw), lambda b, h, t: (b, 0, h)),
                  pl.BlockSpec((tq, 128), lambda b, h, t: (t, 0)),
                  pl.BlockSpec((tq, 128), lambda b, h, t: (t, 0)),
                  pl.BlockSpec((L, 128), lambda b, h, t: (0, 0)),
                  pl.BlockSpec((L, 128), lambda b, h, t: (0, 0))]
        args += [ctx[0], ctx[1], cos_t, sin_t, cos_t, sin_t]
    specs += [pl.BlockSpec((4, QK), lambda b, h, t: (0, 0)),
              pl.BlockSpec((1, 128), lambda b, h, t: (0, 0))]
    args += [att_lambda, subln.reshape(1, 128)]
    return pl.pallas_call(
        functools.partial(_attn_kernel, lam_init=lam_init, n_ctx=n_ctx, rope=rope, hb=hb),
        grid=(B, HEADS // hb, nq),
        in_specs=specs,
        out_specs=pl.BlockSpec((tq, bw), lambda b, h, t: (b * nq + t, h)),
        out_shape=jax.ShapeDtypeStruct((B * L, ATT_W), F32),
        scratch_shapes=[pltpu.VMEM((hb, n_ctx + L, 128), BF16), pltpu.VMEM((hb, n_ctx + L, 256), BF16)],
        compiler_params=_cp(("arbitrary", "arbitrary", "arbitrary")),
        name="diff_attention",
    )(*args)


def _rwkv_pre_kernel(z_ref, zp_ref, zn_ref, mu_ref, seg_ref, wup_ref, aup_ref, gup_ref,
                     w0_ref, a0_ref, kk_ref, ka_ref, sh_ref, pd_ref, ms_ref, *, tm):
    t = pl.program_id(1)
    nt = pl.num_programs(1)
    z = z_ref[...]
    row = lax.broadcasted_iota(jnp.int32, z.shape, 0)
    prev_row = jnp.where(t > 0, zp_ref[7:8, :], 0.0)
    next_row = jnp.where(t < nt - 1, zn_ref[0:1, :], 0.0)
    zp = jnp.where(row == 0, prev_row, pltpu.roll(z, 1, 0))
    zn = jnp.where(row == tm - 1, next_row, pltpu.roll(z, tm - 1, 0))
    zs = z + mu_ref[...] * (0.5 * (zp + zn) - z)
    r = zs[:, 0:RW]
    k = zs[:, RW:2 * RW]
    v = zs[:, 2 * RW:3 * RW]
    wa = zs[:, 3 * RW:3 * RW + 128]
    gl = zs[:, 3 * RW + 128:3 * RW + 256]
    lane = lax.broadcasted_iota(jnp.int32, wa.shape, 1)
    wa = jnp.where(lane < 64, jnp.tanh(wa), wa)
    g = _dot_hi(_sigmoid(gl), gup_ref[...])
    kk = k * kk_ref[...]
    kk = kk * lax.rsqrt(_segsum(kk * kk, seg_ref[...]) + EPS)
    sh_ref[0] = r
    sh_ref[1] = v
    sh_ref[2] = kk
    kb = jnp.zeros_like(k)
    for d in range(2):
        xw = w0_ref[d] + _dot_hi(wa, wup_ref[d])
        w = jnp.exp(-math.exp(-0.5) * _sigmoid(xw))
        a = _sigmoid(a0_ref[d] + _dot_hi(wa, aup_ref[d]))
        kd = k * (1.0 + (a - 1.0) * ka_ref[...])
        pd_ref[0, d] = w
        pd_ref[1, d] = kd
        pd_ref[2, d] = kk * a
        kb = kb + kd
    ms_ref[0] = g
    ms_ref[1] = kb


def _rwkv_pre(proj, lp, segmat, seg):
    B, L = seg[0], seg[1]
    tm = min(L, 256)
    nt = L // tm
    zs = PROJ_STRIDE // RW_PROJ
    zc = COL_Z // RW_PROJ
    nb8 = L // 8
    zeros64 = jnp.zeros((2, 64, RW), F32)
    wup = jnp.concatenate([lp['rwkv_w_up'], zeros64], axis=1)
    aup = jnp.concatenate([zeros64, lp['rwkv_a_up']], axis=1)
    c2 = lambda b, t: (0, 0)
    c3 = lambda b, t: (0, 0, 0)
    return pl.pallas_call(
        functools.partial(_rwkv_pre_kernel, tm=tm),
        grid=(B, nt),
        in_specs=[pl.BlockSpec((tm, RW_PROJ), lambda b, t: (t, b * zs + zc)),
                  pl.BlockSpec((8, RW_PROJ), lambda b, t: (jnp.maximum(t * (tm // 8) - 1, 0), b * zs + zc)),
                  pl.BlockSpec((8, RW_PROJ), lambda b, t: (jnp.minimum((t + 1) * (tm // 8), nb8 - 1), b * zs + zc)),
                  pl.BlockSpec((1, RW_PROJ), c2),
                  pl.BlockSpec((RW, RW), c2),
                  pl.BlockSpec((2, 128, RW), c3),
                  pl.BlockSpec((2, 128, RW), c3),
                  pl.BlockSpec((128, RW), c2),
                  pl.BlockSpec((2, 1, RW), c3),
                  pl.BlockSpec((2, 1, RW), c3),
                  pl.BlockSpec((1, RW), c2),
                  pl.BlockSpec((1, RW), c2)],
        out_specs=[pl.BlockSpec((3, tm, RW), lambda b, t: (0, t, b)),
                   pl.BlockSpec((3, 2, tm, RW), lambda b, t: (0, 0, t, b)),
                   pl.BlockSpec((2, tm, RW), lambda b, t: (0, t, b))],
        out_shape=[jax.ShapeDtypeStruct((3, L, B * RW), F32),
                   jax.ShapeDtypeStruct((3, 2, L, B * RW), F32),
                   jax.ShapeDtypeStruct((2, L, B * RW), F32)],
        compiler_params=_cp(("arbitrary", "arbitrary")),
        name="rwkv_pre",
    )(proj, proj, proj, lp['rwkv_mu'].reshape(1, RW_PROJ), segmat, wup, aup, lp['rwkv_g_up'],
      lp['rwkv_w0'].reshape(2, 1, RW), lp['rwkv_a0'].reshape(2, 1, RW),
      lp['rwkv_k_k'].reshape(1, RW), lp['rwkv_k_a'].reshape(1, RW))


def _wkv_kernel(xa_ref, xb_ref, s0_ref, yf_ref, yr_ref, sf_ref, st_ref, *, tc):
    c = pl.program_id(1)

    @pl.when(c == 0)
    def _():
        st_ref[...] = s0_ref[...]

    fwd = lax.broadcasted_iota(jnp.int32, (QK, 128), 1) < 64

    def step(t, carry):
        tr = tc - 1 - t

        def tile(q):
            return jnp.where(fwd, xa_ref[q, t], xb_ref[q, tr])

        r_t = tile(0)
        v_t = tile(1)
        kk_t = tile(2)
        w_t = tile(3)
        kd_t = tile(4)
        b_t = tile(5)
        wr = w_t * r_t
        kr = jnp.sum(kd_t * r_t, axis=0, keepdims=True)
        br = jnp.sum(b_t * r_t, axis=0, keepdims=True)
        ytiles = []
        for g in range(QK // 8):
            ys = []
            for j in range(8):
                vi = g * 8 + j
                s = st_ref[vi]
                sa = jnp.sum(s * kk_t, axis=0, keepdims=True)
                y0 = jnp.sum(s * wr, axis=0, keepdims=True)
                vv = v_t[vi:vi + 1]
                st_ref[vi] = s * w_t + (vv * kd_t - sa * b_t)
                ys.append(y0 + vv * kr - sa * br)
            ytiles.append(jnp.concatenate(ys, axis=0))
        y_t = jnp.transpose(jnp.concatenate(ytiles, axis=0))
        yf_ref[t, 0] = y_t[0:64]
        yr_ref[tr, 0] = y_t[64:128]
        return carry

    lax.fori_loop(0, tc, step, 0)

    @pl.when(c == pl.num_programs(1) - 1)
    def _():
        sf_ref[...] = st_ref[...]


def _wkv_scan(xs, s0):
    _, L, _, lanes = xs.shape
    tc = 16
    nc = L // tc
    return pl.pallas_call(
        functools.partial(_wkv_kernel, tc=tc),
        grid=(lanes // 128, nc),
        in_specs=[pl.BlockSpec((6, tc, QK, 128), lambda g, c: (0, c, 0, g)),
                  pl.BlockSpec((6, tc, QK, 128), lambda g, c: (0, nc - 1 - c, 0, g)),
                  pl.BlockSpec((QK, QK, 128), lambda g, c: (0, 0, g))],
        out_specs=[pl.BlockSpec((tc, 1, 64, QK), lambda g, c: (c, g, 0, 0)),
                   pl.BlockSpec((tc, 1, 64, QK), lambda g, c: (nc - 1 - c, g, 0, 0)),
                   pl.BlockSpec((QK, QK, 128), lambda g, c: (0, 0, g))],
        out_shape=[jax.ShapeDtypeStruct((L, lanes // 128, 64, QK), F32),
                   jax.ShapeDtypeStruct((L, lanes // 128, 64, QK), F32),
                   jax.ShapeDtypeStruct((QK, QK, lanes), F32)],
        scratch_shapes=[pltpu.VMEM((QK, QK, 128), F32)],
        compiler_params=_cp(("arbitrary", "arbitrary")),
        name="wkv7_scan",
    )(xs, xs, s0)


def _rwkv_mix(proj, lp, segmat, seg, s0_bdhvk):
    B, L = seg[0], seg[1]
    G = B // 8
    shared, perdir, misc = _rwkv_pre(proj, lp, segmat, seg)
    sh = shared.reshape(3, L, G, 8, HEADS, QK)
    pd = perdir.reshape(3, 2, L, G, 8, HEADS, QK)
    full = jnp.concatenate([jnp.stack([sh, sh], axis=1), pd], axis=0)
    xs = full.transpose(0, 2, 6, 3, 1, 4, 5).reshape(6, L, QK, G * 128)
    if s0_bdhvk is None:
        s0 = jnp.zeros((QK, QK, G * 128), F32)
    else:
        s0 = (s0_bdhvk.astype(F32).reshape(G, 8, 2, HEADS, QK, QK)
              .transpose(4, 5, 0, 2, 1, 3).reshape(QK, QK, G * 128))
    yf, yr, sf = _wkv_scan(xs, s0)
    o_sum = (yf + yr).reshape(L, B * RW)
    s_fin = (sf.reshape(QK, QK, G, 2, 8, HEADS).transpose(2, 4, 3, 5, 0, 1)
             .reshape(B, 2, HEADS, QK, QK))
    return o_sum, shared, misc, s_fin


def _s5_kernel(u_ref, wb_ref, wc_ref, ab_ref, x0_ref, y_ref, xf_ref, bx_ref, st_ref, *, tc, nb):
    d = pl.program_id(0)
    c = pl.program_id(1)

    @pl.when(c == 0)
    def _():
        st_ref[...] = x0_ref[0]

    u = u_ref[...].reshape(tc * nb, S5W).astype(BF16)
    for j in range(S5_PARTS):
        bj = _dot(u[:, j * S5_PW:(j + 1) * S5_PW], wb_ref[0, j])
        bx_ref[:, j * S5_PS:(j + 1) * S5_PS] = bj[:, 0:S5_PS]
        bx_ref[:, S5S + j * S5_PS:S5S + (j + 1) * S5_PS] = bj[:, S5_PS:2 * S5_PS]
    ar = jnp.broadcast_to(ab_ref[0, 0:1, :], (nb, S5S))
    ai = jnp.broadcast_to(ab_ref[0, 1:2, :], (nb, S5S))

    def step(i, carry):
        tt = jnp.where(d == 0, i, tc - 1 - i)
        rows = pl.ds(pl.multiple_of(tt * nb, nb), nb)
        xr = st_ref[0]
        xi = st_ref[1]
        nr = ar * xr - ai * xi + bx_ref[rows, 0:S5S]
        ni = ar * xi + ai * xr + bx_ref[rows, S5S:2 * S5S]
        st_ref[0] = nr
        st_ref[1] = ni
        bx_ref[rows, 0:S5S] = nr
        bx_ref[rows, S5S:2 * S5S] = ni
        return carry

    lax.fori_loop(0, tc, step, 0)
    ys = []
    for j in range(S5_PARTS):
        xr = bx_ref[:, j * S5_PS:(j + 1) * S5_PS].astype(BF16)
        xi = bx_ref[:, S5S + j * S5_PS:S5S + (j + 1) * S5_PS].astype(BF16)
        ys.append(_dot(xr, wc_ref[j, 0:S5_PS, :]) + _dot(xi, wc_ref[j, S5_PS:2 * S5_PS, :]))
    y_ref[0] = jnp.concatenate(ys, axis=-1).reshape(tc, nb, S5W)

    @pl.when(c == pl.num_programs(1) - 1)
    def _():
        xf_ref[0] = st_ref[...]


def _s5_scan(proj3, wb, wc, ab, x0, seg):
    B, L = seg[0], seg[1]
    tc = 64 if B <= 8 else 16
    nc = L // tc
    tmap = lambda d, c: jnp.where(d == 0, c, nc - 1 - c)
    return pl.pallas_call(
        functools.partial(_s5_kernel, tc=tc, nb=B),
        grid=(2, nc),
        in_specs=[pl.BlockSpec((tc, B, S5W), lambda d, c: (tmap(d, c), 0, COL_U // S5W)),
                  pl.BlockSpec((1, S5_PARTS, S5_PW, 2 * S5_PS), lambda d, c: (d, 0, 0, 0)),
                  pl.BlockSpec((S5_PARTS, 2 * S5_PS, S5_PW), lambda d, c: (0, 0, 0)),
                  pl.BlockSpec((1, 2, S5S), lambda d, c: (d, 0, 0)),
                  pl.BlockSpec((1, 2, B, S5S), lambda d, c: (d, 0, 0, 0))],
        out_specs=[pl.BlockSpec((1, tc, B, S5W), lambda d, c: (d, tmap(d, c), 0, 0)),
                   pl.BlockSpec((1, 2, B, S5S), lambda d, c: (d, 0, 0, 0))],
        out_shape=[jax.ShapeDtypeStruct((2, L, B, S5W), F32),
                   jax.ShapeDtypeStruct((2, 2, B, S5S), F32)],
        scratch_shapes=[pltpu.VMEM((tc * B, 2 * S5S), F32), pltpu.VMEM((2, B, S5S), F32)],
        compiler_params=_cp(("arbitrary", "arbitrary")),
        name="s5_scan",
    )(proj3, wb, wc, ab, x0)


def _s5_params(lp):
    lam_re, lam_im = lp['s5_lam_re'], lp['s5_lam_im']
    dt = jnp.exp(lp['s5_log_step'])[:, :, None]
    mag = jnp.exp(lam_re * dt)
    ab_re, ab_im = mag * jnp.cos(lam_im * dt), mag * jnp.sin(lam_im * dt)
    den = lam_re * lam_re + lam_im * lam_im
    f_re = ((ab_re - 1.0) * lam_re + ab_im * lam_im) / den
    f_im = (ab_im * lam_re - (ab_re - 1.0) * lam_im) / den
    b_re, b_im = lp['s5_b_re'], lp['s5_b_im']
    wre = f_re[..., None] * b_re[None] - f_im[..., None] * b_im[None]
    wim = f_re[..., None] * b_im[None] + f_im[..., None] * b_re[None]
    eye = jnp.eye(S5G, dtype=F32)

    def block_in(w):
        return jnp.einsum('dgnc,gh->dgchn', w, eye).reshape(2, S5W, S5S)

    wb_re, wb_im = block_in(wre), block_in(wim)
    wb = jnp.stack([jnp.concatenate([wb_re[:, j * S5_PW:(j + 1) * S5_PW, j * S5_PS:(j + 1) * S5_PS],
                                     wb_im[:, j * S5_PW:(j + 1) * S5_PW, j * S5_PS:(j + 1) * S5_PS]], axis=-1)
                    for j in range(S5_PARTS)], axis=1).astype(BF16)

    def block_out(cm):
        return jnp.einsum('gcn,gh->gnhc', cm, eye).reshape(S5S, S5W)

    wc_re, wc_im = block_out(lp['s5_c_re']), -block_out(lp['s5_c_im'])
    wc = jnp.stack([jnp.concatenate([wc_re[j * S5_PS:(j + 1) * S5_PS, j * S5_PW:(j + 1) * S5_PW],
                                     wc_im[j * S5_PS:(j + 1) * S5_PS, j * S5_PW:(j + 1) * S5_PW]], axis=0)
                    for j in range(S5_PARTS)], axis=0).astype(BF16)
    ab = jnp.stack([ab_re.reshape(2, S5S), ab_im.reshape(2, S5S)], axis=1)
    return wb, wc, ab


def _mix_kernel(oatt_ref, osum_ref, sh_ref, ms_ref, y_ref, u_ref, seg_ref, lng_ref, lnb_ref,
                rk_ref, d_ref, wglu_ref, o_ref):
    segm = seg_ref[...]
    o = osum_ref[...]
    mean = _segsum(o, segm) * (1.0 / QK)
    oc = o - mean
    var = _segsum(oc * oc, segm) * (1.0 / QK)
    o_n = oc * lax.rsqrt(var + GN_EPS) * lng_ref[...] + lnb_ref[...]
    r = sh_ref[0]
    v = sh_ref[1]
    bonus = _segsum(r * 0.5 * ms_ref[1] * rk_ref[...], segm) * v
    rw = (o_n + bonus) * ms_ref[0]
    u = u_ref[...]
    y = d_ref[...] * u + y_ref[0] + y_ref[1]
    hg = 0.5 * y * (1.0 + jnp.tanh(math.sqrt(2.0 / math.pi) * (y + 0.044715 * (y * y * y))))
    s5 = hg * _sigmoid(_dot(hg.astype(BF16), wglu_ref[...]))
    o_ref[:, 0:ATT_W] = oatt_ref[...].astype(BF16)
    o_ref[:, ATT_W:ATT_W + RW] = rw.astype(BF16)
    o_ref[:, ATT_W + RW:D] = s5.astype(BF16)


def _mix_post(o_att, o_sum, shared, misc, y_s5, proj, segmat, lp, seg):
    B, L = seg[0], seg[1]
    tm = min(L, 256)
    nt = L // tm
    us = PROJ_STRIDE // S5W
    c2 = lambda b, t: (0, 0)
    return pl.pallas_call(
        _mix_kernel,
        grid=(B, nt),
        in_specs=[pl.BlockSpec((tm, ATT_W), lambda b, t: (b * nt + t, 0)),
                  pl.BlockSpec((tm, RW), lambda b, t: (t, b)),
                  pl.BlockSpec((3, tm, RW), lambda b, t: (0, t, b)),
                  pl.BlockSpec((2, tm, RW), lambda b, t: (0, t, b)),
                  pl.BlockSpec((2, tm, S5W), lambda b, t: (0, t, b)),
                  pl.BlockSpec((tm, S5W), lambda b, t: (t, b * us + COL_U // S5W)),
                  pl.BlockSpec((RW, RW), c2),
                  pl.BlockSpec((1, RW), c2),
                  pl.BlockSpec((1, RW), c2),
                  pl.BlockSpec((1, RW), c2),
                  pl.BlockSpec((1, S5W), c2),
                  pl.BlockSpec((S5W, S5W), c2)],
        out_specs=pl.BlockSpec((tm, D), lambda b, t: (b * nt + t, 0)),
        out_shape=jax.ShapeDtypeStruct((B * L, D), BF16),
        compiler_params=_cp(("arbitrary", "arbitrary")),
        name="mix_post",
    )(o_att, o_sum, shared, misc, y_s5.reshape(2, L, B * S5W), proj, segmat,
      lp['rwkv_ln_g'].reshape(1, RW), lp['rwkv_ln_b'].reshape(1, RW), lp['rwkv_r_k'].reshape(1, RW),
      lp['s5_d'].reshape(1, S5W), lp['s5_w_glu'].astype(BF16))


def _out_kernel(oc_ref, x_ref, w_ref, g1_ref, sc_ref, sh_ref, n2_ref, rt_ref, rb_ref,
                x1_ref, hp_ref, tw_ref, ti_ref):
    x1 = x_ref[...] + g1_ref[0] * _dot(oc_ref[...], w_ref[...])
    x1_ref[...] = x1
    y = x1 * lax.rsqrt(jnp.mean(x1 * x1, axis=-1, keepdims=True) + EPS)
    h2 = (y * n2_ref[...]) * (1.0 + sc_ref[0]) + sh_ref[0]
    hp_ref[...] = _pack_bf16_pair(h2[:, 0:D // 2], h2[:, D // 2:D])
    h_hi = h2.astype(BF16)
    h_lo = (h2 - h_hi.astype(F32)).astype(BF16)
    r_hi = rt_ref[0]
    r_lo = rt_ref[1]
    logits = _dot_nt(r_hi, h_hi) + (_dot_nt(r_hi, h_lo) + _dot_nt(r_lo, h_hi))
    scores = _sigmoid(logits)
    sel = scores + rb_ref[...]
    row = lax.broadcasted_iota(jnp.int32, sel.shape, 0)
    tws, tis = [], []
    for k in range(TOPK):
        m = jnp.max(sel, axis=0, keepdims=True)
        idx = jnp.min(jnp.where(sel == m, row, NE), axis=0, keepdims=True)
        hit = row == idx
        tws.append(jnp.sum(jnp.where(hit, scores, 0.0), axis=0, keepdims=True))
        tis.append(idx)
        sel = jnp.where(hit, -jnp.inf, sel)
    tw = jnp.concatenate(tws, axis=0)
    tw_ref[...] = tw / jnp.sum(tw, axis=0, keepdims=True) * ROUTE_SCALE
    ti_ref[...] = jnp.concatenate(tis, axis=0)


def _out_proj(o_cat, x, w_out_bf, mods6, norm2, router_t2, moe_bias, seg):
    B, L, mod_base, mod_stride = seg
    tm = min(L, 256)
    nt = L // tm

    def mrow(j):
        return lambda i: ((mod_base + (i // nt) * mod_stride) * 6 + j, 0, 0)

    c2 = lambda i: (0, 0)
    n = B * L
    return pl.pallas_call(
        _out_kernel,
        grid=(n // tm,),
        in_specs=[pl.BlockSpec((tm, D), lambda i: (i, 0)),
                  pl.BlockSpec((tm, D), lambda i: (i, 0)),
                  pl.BlockSpec((D, D), c2),
                  pl.BlockSpec((1, 1, D), mrow(2)),
                  pl.BlockSpec((1, 1, D), mrow(4)),
                  pl.BlockSpec((1, 1, D), mrow(3)),
                  pl.BlockSpec((1, D), c2),
                  pl.BlockSpec((2, NE, D), lambda i: (0, 0, 0)),
                  pl.BlockSpec((NE, 1), c2)],
        out_specs=[pl.BlockSpec((tm, D), lambda i: (i, 0)),
                   pl.BlockSpec((tm, D // 2), lambda i: (i, 0)),
                   pl.BlockSpec((TOPK, tm), lambda i: (0, i)),
                   pl.BlockSpec((TOPK, tm), lambda i: (0, i))],
        out_shape=[jax.ShapeDtypeStruct((n, D), F32),
                   jax.ShapeDtypeStruct((n, D // 2), U32),
                   jax.ShapeDtypeStruct((TOPK, n), F32),
                   jax.ShapeDtypeStruct((TOPK, n), jnp.int32)],
        compiler_params=_cp(("arbitrary",)),
        name="out_proj",
    )(o_cat, x, w_out_bf, mods6, mods6, mods6, norm2.reshape(1, D), router_t2, moe_bias.reshape(NE, 1))


DISP_TM = 128


def _dispatch_copy(hp_ref, xs_ref, sem, r, dst_row):
    return pltpu.make_async_copy(hp_ref.at[pl.ds(r, 1)], xs_ref.at[pl.ds(dst_row, 1)], sem.at[0])


def _dispatch_kernel(idx_ref, hp_ref, xs_in_ref, xs_ref, sem):
    del xs_in_ref
    for r in range(DISP_TM):
        for k in range(TOPK):
            _dispatch_copy(hp_ref, xs_ref, sem, r, idx_ref[0, 0, r * TOPK + k]).start(priority=k % 2)
    for r in range(DISP_TM):
        for k in range(TOPK):
            _dispatch_copy(hp_ref, xs_ref, sem, r, 0).wait()


def _moe_dispatch(slot_of, h2p, xs_init):
    n = slot_of.shape[0]
    tm = DISP_TM
    idx3 = slot_of.reshape(n // tm, 1, tm * TOPK)
    return pl.pallas_call(
        _dispatch_kernel,
        grid=(n // tm,),
        in_specs=[pl.BlockSpec((1, 1, tm * TOPK), lambda i: (i, 0, 0), memory_space=pltpu.SMEM),
                  pl.BlockSpec((tm, D // 2), lambda i: (i, 0)),
                  pl.BlockSpec(memory_space=pl.ANY)],
        out_specs=pl.BlockSpec(memory_space=pl.ANY),
        out_shape=jax.ShapeDtypeStruct(xs_init.shape, U32),
        scratch_shapes=[pltpu.SemaphoreType.DMA((1,))],
        input_output_aliases={2: 0},
        compiler_params=_cp(("arbitrary",)),
        name="moe_dispatch",
    )(idx3, h2p, xs_init)


def _moe_kernel(be_ref, nu_ref, x_ref, wgu_ref, wdn_ref, o_ref, wgu_bf, wdn_bf):
    i = pl.program_id(0)

    @pl.when(i < nu_ref[0])
    def _():
        @pl.when(jnp.logical_or(i == 0, be_ref[i] != be_ref[jnp.maximum(i - 1, 0)]))
        def _():
            wgu_bf[...] = wgu_ref[0, 0].astype(BF16)
            wdn_bf[...] = wdn_ref[0, 0].astype(BF16)

        for part in range(MOE_SPLIT):
            rows = slice(part * (MOE_BM // MOE_SPLIT), (part + 1) * (MOE_BM // MOE_SPLIT))
            lo, hi = _unpack_bf16_pair(x_ref[rows, :])
            h = (_dot(lo.astype(BF16), wgu_bf[0:D // 2, :])
                 + _dot(hi.astype(BF16), wgu_bf[D // 2:D, :]))
            gte = h[:, 0:EDIM]
            act = (gte * _sigmoid(gte)) * h[:, EDIM:2 * EDIM]
            y = _dot(act.astype(BF16), wdn_bf[...])
            o_ref[rows, :] = _pack_bf16_pair(y[:, 0:D // 2], y[:, D // 2:D])

    @pl.when(i >= nu_ref[0])
    def _():
        o_ref[...] = jnp.zeros(o_ref.shape, U32)


def _moe_experts(block_e, n_used, x_sorted, w_gu_all, w_dn_all, layer):
    n_slots = x_sorted.shape[0]
    nb = n_slots // MOE_BM
    grid_spec = pltpu.PrefetchScalarGridSpec(
        num_scalar_prefetch=2,
        grid=(nb,),
        in_specs=[pl.BlockSpec((MOE_BM, D // 2), lambda i, be, nu: (i, 0)),
                  pl.BlockSpec((1, 1, D, 2 * EDIM), lambda i, be, nu: (layer, be[i], 0, 0)),
                  pl.BlockSpec((1, 1, EDIM, D), lambda i, be, nu: (layer, be[i], 0, 0))],
        out_specs=pl.BlockSpec((MOE_BM, D // 2), lambda i, be, nu: (i, 0)),
        scratch_shapes=[pltpu.VMEM((D, 2 * EDIM), BF16),
                        pltpu.VMEM((EDIM, D), BF16)],
    )
    return pl.pallas_call(
        _moe_kernel,
        grid_spec=grid_spec,
        out_shape=jax.ShapeDtypeStruct((n_slots, D // 2), U32),
        compiler_params=_cp(("arbitrary",)),
        name="moe_experts",
    )(block_e, n_used, x_sorted, w_gu_all, w_dn_all)


COMB_TM = 128


def _comb_gather(idx_ref, y_hbm, ybuf, sem, slot):
    for r in range(COMB_TM):
        for k in range(TOPK):
            pltpu.make_async_copy(y_hbm.at[pl.ds(idx_ref[0, 0, r * TOPK + k], 1)],
                                  ybuf.at[slot, k, pl.ds(r, 1)], sem.at[slot]).start(priority=k % 2)


def _comb_wait(y_hbm, ybuf, sem, slot):
    for r in range(COMB_TM):
        for k in range(TOPK):
            pltpu.make_async_copy(y_hbm.at[pl.ds(0, 1)], ybuf.at[slot, k, pl.ds(r, 1)],
                                  sem.at[slot]).wait()


def _comb_kernel(idxc_ref, idxn_ref, y_hbm, hp_ref, tw_ref, x1_ref, g2_ref, wgu_ref, wdn_ref, fn_ref,
                 o_ref, ybuf, sem, *, final):
    i = pl.program_id(0)
    nb = pl.num_programs(0)
    slot = i % 2

    @pl.when(i == 0)
    def _():
        _comb_gather(idxc_ref, y_hbm, ybuf, sem, 0)

    _comb_gather(idxn_ref, y_hbm, ybuf, sem, 1 - slot)

    lo, hi = _unpack_bf16_pair(hp_ref[...])
    h = _dot(lo.astype(BF16), wgu_ref[0:D // 2, :]) + _dot(hi.astype(BF16), wgu_ref[D // 2:D, :])
    gte = h[:, 0:EDIM]
    act = (gte * _sigmoid(gte)) * h[:, EDIM:2 * EDIM]
    shared = _dot(act.astype(BF16), wdn_ref[...])

    _comb_wait(y_hbm, ybuf, sem, slot)
    rlo = jnp.zeros((COMB_TM, D // 2), F32)
    rhi = jnp.zeros((COMB_TM, D // 2), F32)
    tw = tw_ref[...]
    for k in range(TOPK):
        a, b = _unpack_bf16_pair(ybuf[slot, k])
        wk = tw[:, k:k + 1]
        rlo = rlo + wk * a
        rhi = rhi + wk * b
    g2 = g2_ref[0]
    x1 = x1_ref[...]
    out_lo = x1[:, 0:D // 2] + g2[:, 0:D // 2] * (rlo + shared[:, 0:D // 2])
    out_hi = x1[:, D // 2:D] + g2[:, D // 2:D] * (rhi + shared[:, D // 2:D])
    if final:
        ms = (jnp.sum(out_lo * out_lo, axis=-1, keepdims=True)
              + jnp.sum(out_hi * out_hi, axis=-1, keepdims=True)) * (1.0 / D)
        inv = lax.rsqrt(ms + EPS)
        fn = fn_ref[...]
        out_lo = out_lo * inv * fn[:, 0:D // 2]
        out_hi = out_hi * inv * fn[:, D // 2:D]
    o_ref[:, 0:D // 2] = out_lo
    o_ref[:, D // 2:D] = out_hi

    @pl.when(i == nb - 1)
    def _():
        _comb_wait(y_hbm, ybuf, sem, 1 - slot)


def _combine(slot_of, top_w, y_slots, h2p, x1, mods6, w_sgu_bf, w_sdn_bf, final_norm, seg, row0, final):
    B, L, mod_base, mod_stride = seg
    n = B * L
    tm = COMB_TM
    nb = n // tm
    nt = L // tm
    blk0 = row0 // tm
    idx3 = slot_of.reshape(-1, 1, tm * TOPK)
    c2 = lambda i: (0, 0)
    return pl.pallas_call(
        functools.partial(_comb_kernel, final=final),
        grid=(nb,),
        in_specs=[pl.BlockSpec((1, 1, tm * TOPK), lambda i: (blk0 + i, 0, 0), memory_space=pltpu.SMEM),
                  pl.BlockSpec((1, 1, tm * TOPK), lambda i: (blk0 + jnp.minimum(i + 1, nb - 1), 0, 0),
                               memory_space=pltpu.SMEM),
                  pl.BlockSpec(memory_space=pl.ANY),
                  pl.BlockSpec((tm, D // 2), lambda i: (blk0 + i, 0)),
                  pl.BlockSpec((tm, TOPK), lambda i: (blk0 + i, 0)),
                  pl.BlockSpec((tm, D), lambda i: (i, 0)),
                  pl.BlockSpec((1, 1, D), lambda i: ((mod_base + (i // nt) * mod_stride) * 6 + 5, 0, 0)),
                  pl.BlockSpec((D, 2 * EDIM), c2),
                  pl.BlockSpec((EDIM, D), c2),
                  pl.BlockSpec((1, D), c2)],
        out_specs=pl.BlockSpec((tm, D), lambda i: (i, 0)),
        out_shape=jax.ShapeDtypeStruct((n, D), F32),
        scratch_shapes=[pltpu.VMEM((2, TOPK, tm, D // 2), U32), pltpu.SemaphoreType.DMA((2,))],
        compiler_params=_cp(("arbitrary",)),
        name="moe_combine",
    )(idx3, idx3, y_slots, h2p, top_w, x1, mods6, w_sgu_bf, w_sdn_bf, final_norm.reshape(1, D))


def _routing_tables(top_i):
    n = top_i.shape[0]
    onehot = (top_i[:, :, None] == jnp.arange(NE, dtype=jnp.int32)[None, None, :])
    mask = jnp.any(onehot, axis=1).astype(jnp.int32)
    counts = jnp.sum(mask, axis=0)
    rank = jnp.cumsum(mask, axis=0) - mask
    padded = (counts + MOE_BM - 1) // MOE_BM * MOE_BM
    pad_end = jnp.cumsum(padded)
    pad_start = pad_end - padded
    slot_all = pad_start[None, :] + rank
    slot_of = jnp.take_along_axis(slot_all, top_i, axis=1).astype(jnp.int32)
    n_blocks = n * TOPK // MOE_BM + NE
    starts = jnp.arange(n_blocks, dtype=jnp.int32) * MOE_BM
    block_e = jnp.minimum(jnp.sum((pad_end[None, :] <= starts[:, None]).astype(jnp.int32), axis=1),
                          NE - 1).astype(jnp.int32)
    n_used = (pad_end[NE - 1:NE] // MOE_BM).astype(jnp.int32)
    return slot_of, block_e, n_used


def _rope_tables(n_tok):
    rows = n_tok // GRID_W
    row = jnp.repeat(jnp.arange(rows), GRID_W).astype(F32)
    col = jnp.tile(jnp.arange(GRID_W), rows).astype(F32)
    half = QK // 2
    inv = ROPE_THETA ** (-jnp.arange(0, half, 2, dtype=F32) / half)
    ang = jnp.concatenate([row[:, None] * inv, col[:, None] * inv], axis=-1)
    cos = jnp.repeat(jnp.cos(ang), 2, axis=-1)
    sin = jnp.repeat(jnp.sin(ang), 2, axis=-1)
    sign = jnp.tile(jnp.array([-1.0, 1.0], F32), QK // 2)
    return jnp.tile(cos, (1, 2)), jnp.tile(sin * sign, (1, 2))


def kernel(x_prompt, x_sample, cache_attn_k, cache_attn_v, state_rwkv, state_s5, c, c_ctx, w_mod, b_mod, norm1, norm2, w_in, w_out, att_lambda, att_subln, rwkv_mu, rwkv_w0, rwkv_w_up, rwkv_a0, rwkv_a_up, rwkv_g_up, rwkv_k_k, rwkv_k_a, rwkv_r_k, rwkv_ln_g, rwkv_ln_b, s5_lam_re, s5_lam_im, s5_log_step, s5_b_re, s5_b_im, s5_c_re, s5_c_im, s5_d, s5_w_glu, moe_router, moe_bias, moe_w_gate_up, moe_w_down, shared_w_gate_up, shared_w_down, final_norm):
    params = dict(w_mod=w_mod, b_mod=b_mod, norm1=norm1, norm2=norm2, w_in=w_in, w_out=w_out,
                  att_lambda=att_lambda, att_subln=att_subln,
                  rwkv_mu=rwkv_mu, rwkv_w0=rwkv_w0, rwkv_w_up=rwkv_w_up, rwkv_a0=rwkv_a0,
                  rwkv_a_up=rwkv_a_up, rwkv_g_up=rwkv_g_up, rwkv_k_k=rwkv_k_k, rwkv_k_a=rwkv_k_a,
                  rwkv_r_k=rwkv_r_k, rwkv_ln_g=rwkv_ln_g, rwkv_ln_b=rwkv_ln_b,
                  s5_lam_re=s5_lam_re, s5_lam_im=s5_lam_im, s5_log_step=s5_log_step,
                  s5_b_re=s5_b_re, s5_b_im=s5_b_im, s5_c_re=s5_c_re, s5_c_im=s5_c_im,
                  s5_d=s5_d, s5_w_glu=s5_w_glu,
                  moe_router=moe_router, moe_bias=moe_bias, moe_w_gate_up=moe_w_gate_up,
                  moe_w_down=moe_w_down, shared_w_gate_up=shared_w_gate_up,
                  shared_w_down=shared_w_down)
    bp, lp_len, _ = x_prompt.shape
    bs, ls_len, _ = x_sample.shape
    segs = ((bp, lp_len, 0, 0), (bs, ls_len, 1, 1))
    xs = [x_prompt.reshape(bp * lp_len, D), x_sample.reshape(bs * ls_len, D)]
    cpad = jnp.zeros((16, D), F32).at[0].set(c_ctx).at[1:1 + bs].set(c)
    new_k, new_v, new_r, new_s = [], [], [], []
    n_slots = ((bp * lp_len + bs * ls_len) * TOPK // MOE_BM + NE) * MOE_BM
    x_sorted = jnp.zeros((n_slots, D // 2), U32)
    for l in range(DEPTH):
        lp = {name: arr[l] for name, arr in params.items() if name not in _WHOLE}
        whole = {name: params[name] for name in _WHOLE}
        lam_init = 0.8 - 0.6 * math.exp(-0.3 * l)
        ctxs = (None, (cache_attn_k[:, l], cache_attn_v[:, l], state_rwkv[:, l], state_s5[:, l]))
        xs, caches, x_sorted = _layer(xs, segs, cpad, lp, whole, l, lam_init, ctxs, final_norm,
                                      l == DEPTH - 1, x_sorted)
        ck, cv, cr, cs = caches[0]
        new_k.append(ck)
        new_v.append(cv)
        new_r.append(cr)
        new_s.append(cs)
    return (xs[0].reshape(bp, lp_len, D), xs[1].reshape(bs, ls_len, D),
            jnp.stack(new_k, axis=1), jnp.stack(new_v, axis=1),
            jnp.stack(new_r, axis=1), jnp.stack(new_s, axis=1))


_WHOLE = ('w_mod', 'moe_w_gate_up', 'moe_w_down')


def _layer(xs, segs, cpad, lp, whole, layer, lam_init, ctxs, final_norm, final, x_sorted):
    hh = jnp.arange(RW) // QK
    segmat = (hh[:, None] == hh[None, :]).astype(BF16)
    perm = jnp.concatenate([jnp.arange(0, 3072), jnp.arange(3072 + RW_PROJ, PROJ_W),
                            jnp.arange(3072, 3072 + RW_PROJ)])
    mods6 = _modulation(cpad, whole['w_mod'], lp['b_mod'], layer).reshape(16 * 6, 1, D)
    w_in_bf = lp['w_in'][:, perm].astype(BF16)
    w_out_bf = lp['w_out'].astype(BF16)
    wb, wc, ab = _s5_params(lp)
    rt = lp['moe_router'].T
    rt_hi = rt.astype(BF16)
    router_t2 = jnp.stack([rt_hi, (rt - rt_hi.astype(F32)).astype(BF16)], axis=0)
    x1s, h2ps, tws, tis, caches = [], [], [], [], []
    for si, seg in enumerate(segs):
        B, L = seg[0], seg[1]
        proj = _in_proj(xs[si], lp['norm1'], mods6, w_in_bf, seg)
        proj3 = proj.reshape(L, B, PROJ_STRIDE)
        if ctxs[si] is None:
            o_att = _attention(proj, lp['att_lambda'], lp['att_subln'], lam_init, seg)
            s0_rwkv = None
            x0 = jnp.zeros((2, 2, B, S5S), F32)
        else:
            ck, cv, s0_rwkv, s0_s5 = ctxs[si]
            ctx = (ck.reshape(B, -1, ATT_W), cv.reshape(B, -1, ATT_W))
            o_att = _attention(proj, lp['att_lambda'], lp['att_subln'], lam_init, seg, ctx,
                               _rope_tables(L))
            x0 = s0_s5.astype(F32).reshape(B, 2, 2, S5S).transpose(1, 2, 0, 3)
        o_sum, shared, misc, s_fin = _rwkv_mix(proj, lp, segmat, seg, s0_rwkv)
        y_s5, xf = _s5_scan(proj3, wb, wc, ab, x0, seg)
        o_cat = _mix_post(o_att, o_sum, shared, misc, y_s5, proj, segmat, lp, seg)
        x1, h2p, tw, ti = _out_proj(o_cat, xs[si], w_out_bf, mods6, lp['norm2'],
                                    router_t2, lp['moe_bias'], seg)
        x1s.append(x1)
        h2ps.append(h2p)
        tws.append(tw)
        tis.append(ti)
        caches.append((proj3[:, :, COL_K:COL_K + ATT_W].transpose(1, 0, 2).reshape(B, L, HEADS, 2, QK),
                       proj3[:, :, COL_V:COL_V + ATT_W].transpose(1, 0, 2).reshape(B, L, HEADS, 2 * QK),
                       s_fin,
                       xf.transpose(2, 0, 1, 3).reshape(B, 2, 2, S5G, S5N)))
    h2p_all = jnp.concatenate(h2ps, axis=0)
    top_w = jnp.concatenate(tws, axis=1).T
    slot_of, block_e, n_used = _routing_tables(jnp.concatenate(tis, axis=1).T)
    x_sorted = _moe_dispatch(slot_of, h2p_all, x_sorted)
    y_slots = _moe_experts(block_e, n_used, x_sorted, whole['moe_w_gate_up'], whole['moe_w_down'], layer)
    w_sgu_bf = lp['shared_w_gate_up'].astype(BF16)
    w_sdn_bf = lp['shared_w_down'].astype(BF16)
    outs = []
    row0 = 0
    for si, seg in enumerate(segs):
        outs.append(_combine(slot_of, top_w, y_slots, h2p_all, x1s[si], mods6, w_sgu_bf, w_sdn_bf,
                             final_norm, seg, row0, final))
        row0 += seg[0] * seg[1]
    return outs, caches, x_sorted
```

```python
import functools
import math

import jax
import jax.numpy as jnp
from jax import lax
from jax.experimental import pallas as pl
from jax.experimental.pallas import tpu as pltpu

F32 = jnp.float32
BF16 = jnp.bfloat16
U32 = jnp.uint32

D = 2048
DEPTH = 2
EPS = 1e-6
GRID_W = 64
ROPE_THETA = 10000.0
HEADS = 8
QK = 64
ATT_W = 1024
RW = 512
RW_PROJ = 1792
S5W = 512
S5G = 32
S5N = 64
S5C = 16
S5S = S5G * S5N
S5_PARTS = 4
S5_PW = S5W // S5_PARTS
S5_PS = S5S // S5_PARTS
GN_EPS = 64e-5
NE = 64
TOPK = 8
EDIM = 512
ROUTE_SCALE = 2.5
PROJ_W = 5376
PROJ_STRIDE = 7168
COL_Q, COL_K, COL_V, COL_U, COL_Z = 0, 1024, 2048, 3072, 3584
PROJ_TN = 1792
MOE_BM = 512
MOE_SPLIT = 2
ATT_TQS = 128
ATT_TK = 256
VMEM_LIMIT = 56 * 1024 * 1024


def _cp(sem):
    return pltpu.CompilerParams(dimension_semantics=sem, vmem_limit_bytes=VMEM_LIMIT)


def _sigmoid(x):
    return 1.0 / (1.0 + jnp.exp(-x))


def _dot(a, b):
    return jnp.dot(a, b, preferred_element_type=F32)


def _dot_nt(a, b):
    return lax.dot_general(a, b, (((1,), (1,)), ((), ())), preferred_element_type=F32)


def _split_bf16(x):
    hi = x.astype(BF16)
    return hi, (x - hi.astype(F32)).astype(BF16)


def _dot_hi(a, b):
    a_hi, a_lo = _split_bf16(a)
    b_hi, b_lo = _split_bf16(b)
    return _dot(a_hi, b_hi) + (_dot(a_hi, b_lo) + _dot(a_lo, b_hi))


def _segsum(x, seg_bf):
    x_hi, x_lo = _split_bf16(x)
    return _dot(x_hi, seg_bf) + _dot(x_lo, seg_bf)


def _bf16_bits(x):
    b = lax.bitcast_convert_type(x, U32)
    return b + jnp.uint32(0x7FFF) + ((b >> 16) & jnp.uint32(1))


def _pack_bf16_pair(lo, hi):
    return (_bf16_bits(lo) >> 16) | (_bf16_bits(hi) & jnp.uint32(0xFFFF0000))


def _unpack_bf16_pair(p):
    lo = lax.bitcast_convert_type(p << 16, F32)
    hi = lax.bitcast_convert_type(p & jnp.uint32(0xFFFF0000), F32)
    return lo, hi


def _mod_kernel(c_ref, w_ref, b_ref, o_ref):
    c = c_ref[...]
    s = c * _sigmoid(c)
    o_ref[...] = _dot(s.astype(BF16), w_ref[0].astype(BF16)) + b_ref[...]


def _modulation(cpad, w_mod_all, b_mod, layer):
    tn = 1024
    return pl.pallas_call(
        _mod_kernel,
        grid=(6 * D // tn,),
        in_specs=[pl.BlockSpec((16, D), lambda j: (0, 0)),
                  pl.BlockSpec((1, D, tn), lambda j: (layer, 0, j)),
                  pl.BlockSpec((1, tn), lambda j: (0, j))],
        out_specs=pl.BlockSpec((16, tn), lambda j: (0, j)),
        out_shape=jax.ShapeDtypeStruct((16, 6 * D), F32),
        compiler_params=_cp(("arbitrary",)),
        name="modulation",
    )(cpad, w_mod_all, b_mod.reshape(1, 6 * D))


def _in_kernel(x_ref, g_ref, sc_ref, sh_ref, w_ref, o_ref):
    x = x_ref[...]
    y = x * lax.rsqrt(jnp.mean(x * x, axis=-1, keepdims=True) + EPS)
    h = (y * g_ref[...]) * (1.0 + sc_ref[0]) + sh_ref[0]
    o_ref[...] = _dot(h.astype(BF16), w_ref[...])


def _in_proj(x, norm_g, mods6, w_in_bf, seg):
    B, L, mod_base, mod_stride = seg
    tm = min(L, 512)
    nt = L // tm
    ncol = PROJ_W // PROJ_TN
    nstride = PROJ_STRIDE // PROJ_TN

    def mrow(j):
        return lambda c, i: ((mod_base + (i // nt) * mod_stride) * 6 + j, 0, 0)

    return pl.pallas_call(
        _in_kernel,
        grid=(ncol, B * nt),
        in_specs=[pl.BlockSpec((tm, D), lambda c, i: (i, 0)),
                  pl.BlockSpec((1, D), lambda c, i: (0, 0)),
                  pl.BlockSpec((1, 1, D), mrow(1)),
                  pl.BlockSpec((1, 1, D), mrow(0)),
                  pl.BlockSpec((D, PROJ_TN), lambda c, i: (0, c))],
        out_specs=pl.BlockSpec((tm, PROJ_TN), lambda c, i: (i % nt, (i // nt) * nstride + c)),
        out_shape=jax.ShapeDtypeStruct((L, B * PROJ_STRIDE), F32),
        compiler_params=_cp(("arbitrary", "arbitrary")),
        name="in_proj",
    )(x, norm_g.reshape(1, D), mods6, mods6, w_in_bf)


def _rope(x, c, s):
    lane = lax.broadcasted_iota(jnp.int32, x.shape, 1)
    nxt = pltpu.roll(x, 127, 1)
    prv = pltpu.roll(x, 1, 1)
    swapped = jnp.where((lane & 1) == 0, nxt, prv)
    return x * c + swapped * s


def _attn_kernel(*refs, lam_init, n_ctx, rope, hb):
    if rope:
        (q_ref, k_ref, v_ref, ck_ref, cv_ref, cq_ref, sq_ref, ckk_ref, skk_ref,
         lam_ref, g_ref, o_ref, kall_ref, vall_ref) = refs
    else:
        q_ref, k_ref, v_ref, lam_ref, g_ref, o_ref, kall_ref, vall_ref = refs

    @pl.when(pl.program_id(2) == 0)
    def _():
        for hh in range(hb):
            cols = slice(hh * 128, (hh + 1) * 128)
            k = k_ref[:, cols]
            if rope:
                k = _rope(k, ckk_ref[...], skk_ref[...])
                kall_ref[hh, 0:n_ctx, :] = ck_ref[0, :, cols].astype(BF16)
                vall_ref[hh, 0:n_ctx, 0:128] = cv_ref[0, :, cols].astype(BF16)
            kall_ref[hh, n_ctx:, :] = k.astype(BF16)
            vall_ref[hh, n_ctx:, 0:128] = v_ref[:, cols].astype(BF16)
            vall_ref[hh, :, 128:256] = jnp.ones((vall_ref.shape[1], 128), BF16)

    lv = lam_ref[...]
    lam = (jnp.exp(jnp.sum(lv[0:1] * lv[1:2], axis=-1, keepdims=True))
           - jnp.exp(jnp.sum(lv[2:3] * lv[3:4], axis=-1, keepdims=True)) + lam_init)
    n_kt = kall_ref.shape[1] // ATT_TK
    lane = lax.broadcasted_iota(jnp.int32, (ATT_TQS, 128), 1)
    for hh in range(hb):
        cols = slice(hh * 128, (hh + 1) * 128)
        q = q_ref[:, cols]
        if rope:
            q = _rope(q, cq_ref[...], sq_ref[...])
        q = q * (QK ** -0.5 * math.log2(math.e))
        for qs in range(q.shape[0] // ATT_TQS):
            rows = slice(qs * ATT_TQS, (qs + 1) * ATT_TQS)
            outs = []
            for m in range(2):
                qm = jnp.where((lane < QK) == (m == 0), q[rows], 0.0).astype(BF16)
                macc = _dot_nt(qm, kall_ref[hh, 0:ATT_TK, :])
                for kt in range(1, n_kt):
                    macc = jnp.maximum(macc, _dot_nt(qm, kall_ref[hh, kt * ATT_TK:(kt + 1) * ATT_TK, :]))
                mx = jnp.max(macc, axis=-1, keepdims=True)
                acc = jnp.zeros((ATT_TQS, 256), F32)
                for kt in range(n_kt):
                    keys = slice(kt * ATT_TK, (kt + 1) * ATT_TK)
                    e = jnp.exp2(_dot_nt(qm, kall_ref[hh, keys, :]) - mx).astype(BF16)
                    acc = acc + _dot(e, vall_ref[hh, keys, :])
                outs.append(acc[:, 0:128] / acc[:, 128:129])
            o = outs[0] - lam * outs[1]
            o = o * lax.rsqrt(jnp.mean(o * o, axis=-1, keepdims=True) + EPS) * g_ref[...]
            o_ref[rows, cols] = o * (1.0 - lam_init)


def _attention(proj, att_lambda, subln, lam_init, seg, ctx=None, tables=None):
    B, L = seg[0], seg[1]
    tq = min(L, 512)
    nq = L // tq
    rope = ctx is not None
    n_ctx = ctx[0].shape[1] if rope else 0
    hb = 1 if rope else HEADS
    bw = 128 * hb
    cs = PROJ_STRIDE // bw
    specs = [pl.BlockSpec((tq, bw), lambda b, h, t: (t, b * cs + COL_Q // bw + h)),
             pl.BlockSpec((L, bw), lambda b, h, t: (0, b * cs + COL_K // bw + h)),
             pl.BlockSpec((L, bw), lambda b, h, t: (0, b * cs + COL_V // bw + h))]
    args = [proj, proj, proj]
    if rope:
        cos_t, sin_t = tables
        specs += [pl.BlockSpec((1, n_ctx, bw), lambda b, h, t: (b, 0, h)),
                  pl.BlockSpec((1, n_ctx, bw), lambda b, h, t: (b, 0, h)),
                  pl.BlockSpec((tq, 128), lambda b, h, t: (t, 0)),
                  pl.BlockSpec((tq, 128), lambda b, h, t: (t, 0)),
                  pl.BlockSpec((L, 128), lambda b, h, t: (0, 0)),
                  pl.BlockSpec((L, 128), lambda b, h, t: (0, 0))]
        args += [ctx[0], ctx[1], cos_t, sin_t, cos_t, sin_t]
    specs += [pl.BlockSpec((4, QK), lambda b, h, t: (0, 0)),
              pl.BlockSpec((1, 128), lambda b, h, t: (0, 0))]
    args += [att_lambda, subln.reshape(1, 128)]
    return pl.pallas_call(
        functools.partial(_attn_kernel, lam_init=lam_init, n_ctx=n_ctx, rope=rope, hb=hb),
        grid=(B, HEADS // hb, nq),
        in_specs=specs,
        out_specs=pl.BlockSpec((tq, bw), lambda b, h, t: (b * nq + t, h)),
        out_shape=jax.ShapeDtypeStruct((B * L, ATT_W), F32),
        scratch_shapes=[pltpu.VMEM((hb, n_ctx + L, 128), BF16), pltpu.VMEM((hb, n_ctx + L, 256), BF16)],
        compiler_params=_cp(("arbitrary", "arbitrary", "arbitrary")),
        name="diff_attention",
    )(*args)


def _rwkv_pre_kernel(z_ref, zp_ref, zn_ref, mu_ref, seg_ref, wup_ref, aup_ref, gup_ref,
                     w0_ref, a0_ref, kk_ref, ka_ref, sh_ref, pd_ref, ms_ref, *, tm):
    t = pl.program_id(1)
    nt = pl.num_programs(1)
    z = z_ref[...]
    row = lax.broadcasted_iota(jnp.int32, z.shape, 0)
    prev_row = jnp.where(t > 0, zp_ref[7:8, :], 0.0)
    next_row = jnp.where(t < nt - 1, zn_ref[0:1, :], 0.0)
    zp = jnp.where(row == 0, prev_row, pltpu.roll(z, 1, 0))
    zn = jnp.where(row == tm - 1, next_row, pltpu.roll(z, tm - 1, 0))
    zs = z + mu_ref[...] * (0.5 * (zp + zn) - z)
    r = zs[:, 0:RW]
    k = zs[:, RW:2 * RW]
    v = zs[:, 2 * RW:3 * RW]
    wa = zs[:, 3 * RW:3 * RW + 128]
    gl = zs[:, 3 * RW + 128:3 * RW + 256]
    lane = lax.broadcasted_iota(jnp.int32, wa.shape, 1)
    wa = jnp.where(lane < 64, jnp.tanh(wa), wa)
    g = _dot_hi(_sigmoid(gl), gup_ref[...])
    kk = k * kk_ref[...]
    kk = kk * lax.rsqrt(_segsum(kk * kk, seg_ref[...]) + EPS)
    sh_ref[0] = r
    sh_ref[1] = v
    sh_ref[2] = kk
    kb = jnp.zeros_like(k)
    for d in range(2):
        xw = w0_ref[d] + _dot_hi(wa, wup_ref[d])
        w = jnp.exp(-math.exp(-0.5) * _sigmoid(xw))
        a = _sigmoid(a0_ref[d] + _dot_hi(wa, aup_ref[d]))
        kd = k * (1.0 + (a - 1.0) * ka_ref[...])
        pd_ref[0, d] = w
        pd_ref[1, d] = kd
        pd_ref[2, d] = kk * a
        kb = kb + kd
    ms_ref[0] = g
    ms_ref[1] = kb


def _rwkv_pre(proj, lp, segmat, seg):
    B, L = seg[0], seg[1]
    tm = min(L, 256)
    nt = L // tm
    zs = PROJ_STRIDE // RW_PROJ
    zc = COL_Z // RW_PROJ
    nb8 = L // 8
    zeros64 = jnp.zeros((2, 64, RW), F32)
    wup = jnp.concatenate([lp['rwkv_w_up'], zeros64], axis=1)
    aup = jnp.concatenate([zeros64, lp['rwkv_a_up']], axis=1)
    c2 = lambda b, t: (0, 0)
    c3 = lambda b, t: (0, 0, 0)
    return pl.pallas_call(
        functools.partial(_rwkv_pre_kernel, tm=tm),
        grid=(B, nt),
        in_specs=[pl.BlockSpec((tm, RW_PROJ), lambda b, t: (t, b * zs + zc)),
                  pl.BlockSpec((8, RW_PROJ), lambda b, t: (jnp.maximum(t * (tm // 8) - 1, 0), b * zs + zc)),
                  pl.BlockSpec((8, RW_PROJ), lambda b, t: (jnp.minimum((t + 1) * (tm // 8), nb8 - 1), b * zs + zc)),
                  pl.BlockSpec((1, RW_PROJ), c2),
                  pl.BlockSpec((RW, RW), c2),
                  pl.BlockSpec((2, 128, RW), c3),
                  pl.BlockSpec((2, 128, RW), c3),
                  pl.BlockSpec((128, RW), c2),
                  pl.BlockSpec((2, 1, RW), c3),
                  pl.BlockSpec((2, 1, RW), c3),
                  pl.BlockSpec((1, RW), c2),
                  pl.BlockSpec((1, RW), c2)],
        out_specs=[pl.BlockSpec((3, tm, RW), lambda b, t: (0, t, b)),
                   pl.BlockSpec((3, 2, tm, RW), lambda b, t: (0, 0, t, b)),
                   pl.BlockSpec((2, tm, RW), lambda b, t: (0, t, b))],
        out_shape=[jax.ShapeDtypeStruct((3, L, B * RW), F32),
                   jax.ShapeDtypeStruct((3, 2, L, B * RW), F32),
                   jax.ShapeDtypeStruct((2, L, B * RW), F32)],
        compiler_params=_cp(("arbitrary", "arbitrary")),
        name="rwkv_pre",
    )(proj, proj, proj, lp['rwkv_mu'].reshape(1, RW_PROJ), segmat, wup, aup, lp['rwkv_g_up'],
      lp['rwkv_w0'].reshape(2, 1, RW), lp['rwkv_a0'].reshape(2, 1, RW),
      lp['rwkv_k_k'].reshape(1, RW), lp['rwkv_k_a'].reshape(1, RW))


def _wkv_kernel(xa_ref, xb_ref, s0_ref, yf_ref, yr_ref, sf_ref, st_ref, *, tc):
    c = pl.program_id(1)

    @pl.when(c == 0)
    def _():
        st_ref[...] = s0_ref[...]

    fwd = lax.broadcasted_iota(jnp.int32, (QK, 128), 1) < 64

    def step(t, carry):
        tr = tc - 1 - t

        def tile(q):
            return jnp.where(fwd, xa_ref[q, t, 0], xb_ref[q, tr, 0])

        r_t = tile(0)
        v_t = tile(1)
        kk_t = tile(2)
        w_t = tile(3)
        kd_t = tile(4)
        b_t = tile(5)
        wr = w_t * r_t
        kr = jnp.sum(kd_t * r_t, axis=0, keepdims=True)
        br = jnp.sum(b_t * r_t, axis=0, keepdims=True)
        ytiles = []
        for g in range(QK // 8):
            ys = []
            for j in range(8):
                vi = g * 8 + j
                s = st_ref[vi]
                sa = jnp.sum(s * kk_t, axis=0, keepdims=True)
                y0 = jnp.sum(s * wr, axis=0, keepdims=True)
                vv = v_t[vi:vi + 1]
                st_ref[vi] = s * w_t + (vv * kd_t - sa * b_t)
                ys.append(y0 + vv * kr - sa * br)
            ytiles.append(jnp.concatenate(ys, axis=0))
        y_t = jnp.transpose(jnp.concatenate(ytiles, axis=0))
        yf_ref[t, 0] = y_t[0:64]
        yr_ref[tr, 0] = y_t[64:128]
        return carry

    lax.fori_loop(0, tc, step, 0)

    @pl.when(c == pl.num_programs(1) - 1)
    def _():
        sf_ref[...] = st_ref[...]


def _wkv_scan(xs, s0):
    _, L, G, _, _ = xs.shape
    lanes = G * 128
    tc = 16
    nc = L // tc
    return pl.pallas_call(
        functools.partial(_wkv_kernel, tc=tc),
        grid=(G, nc),
        in_specs=[pl.BlockSpec((6, tc, 1, QK, 128), lambda g, c: (0, c, g, 0, 0)),
                  pl.BlockSpec((6, tc, 1, QK, 128), lambda g, c: (0, nc - 1 - c, g, 0, 0)),
                  pl.BlockSpec((QK, QK, 128), lambda g, c: (0, 0, g))],
        out_specs=[pl.BlockSpec((tc, 1, 64, QK), lambda g, c: (c, g, 0, 0)),
                   pl.BlockSpec((tc, 1, 64, QK), lambda g, c: (nc - 1 - c, g, 0, 0)),
                   pl.BlockSpec((QK, QK, 128), lambda g, c: (0, 0, g))],
        out_shape=[jax.ShapeDtypeStruct((L, lanes // 128, 64, QK), F32),
                   jax.ShapeDtypeStruct((L, lanes // 128, 64, QK), F32),
                   jax.ShapeDtypeStruct((QK, QK, lanes), F32)],
        scratch_shapes=[pltpu.VMEM((QK, QK, 128), F32)],
        compiler_params=_cp(("arbitrary", "arbitrary")),
        name="wkv7_scan",
    )(xs, xs, s0)


def _rwkv_mix(proj, lp, segmat, seg, s0_bdhvk):
    B, L = seg[0], seg[1]
    G = B // 8
    shared, perdir, misc = _rwkv_pre(proj, lp, segmat, seg)
    sh = jnp.broadcast_to(shared.reshape(3, L, G, 1, 8 * HEADS, QK), (3, L, G, 2, 8 * HEADS, QK))
    pd = jnp.moveaxis(perdir.reshape(3, 2, L, G, 8 * HEADS, QK), 1, 3)
    xs = jnp.swapaxes(jnp.concatenate([sh, pd], axis=0).reshape(6, L, G, 128, QK), -1, -2)
    if s0_bdhvk is None:
        s0 = jnp.zeros((QK, QK, G * 128), F32)
    else:
        s0 = (s0_bdhvk.astype(F32).reshape(G, 8, 2, HEADS, QK, QK)
              .transpose(4, 5, 0, 2, 1, 3).reshape(QK, QK, G * 128))
    yf, yr, sf = _wkv_scan(xs, s0)
    o_sum = (yf + yr).reshape(L, B * RW)
    s_fin = (sf.reshape(QK, QK, G, 2, 8, HEADS).transpose(2, 4, 3, 5, 0, 1)
             .reshape(B, 2, HEADS, QK, QK))
    return o_sum, shared, misc, s_fin


def _s5_kernel(u_ref, wb_ref, wc_ref, ab_ref, x0_ref, y_ref, xf_ref, bx_ref, st_ref, *, tc, nb):
    d = pl.program_id(0)
    c = pl.program_id(1)

    @pl.when(c == 0)
    def _():
        st_ref[...] = x0_ref[0]

    u = u_ref[...].reshape(tc * nb, S5W).astype(BF16)
    for j in range(S5_PARTS):
        bj = _dot(u[:, j * S5_PW:(j + 1) * S5_PW], wb_ref[0, j])
        bx_ref[:, j * S5_PS:(j + 1) * S5_PS] = bj[:, 0:S5_PS]
        bx_ref[:, S5S + j * S5_PS:S5S + (j + 1) * S5_PS] = bj[:, S5_PS:2 * S5_PS]
    ar = jnp.broadcast_to(ab_ref[0, 0:1, :], (nb, S5S))
    ai = jnp.broadcast_to(ab_ref[0, 1:2, :], (nb, S5S))

    def step(i, carry):
        tt = jnp.where(d == 0, i, tc - 1 - i)
        rows = pl.ds(pl.multiple_of(tt * nb, nb), nb)
        xr = st_ref[0]
        xi = st_ref[1]
        nr = ar * xr - ai * xi + bx_ref[rows, 0:S5S]
        ni = ar * xi + ai * xr + bx_ref[rows, S5S:2 * S5S]
        st_ref[0] = nr
        st_ref[1] = ni
        bx_ref[rows, 0:S5S] = nr
        bx_ref[rows, S5S:2 * S5S] = ni
        return carry

    lax.fori_loop(0, tc, step, 0)
    ys = []
    for j in range(S5_PARTS):
        xr = bx_ref[:, j * S5_PS:(j + 1) * S5_PS].astype(BF16)
        xi = bx_ref[:, S5S + j * S5_PS:S5S + (j + 1) * S5_PS].astype(BF16)
        ys.append(_dot(xr, wc_ref[j, 0:S5_PS, :]) + _dot(xi, wc_ref[j, S5_PS:2 * S5_PS, :]))
    y_ref[0] = jnp.concatenate(ys, axis=-1).reshape(tc, nb, S5W)

    @pl.when(c == pl.num_programs(1) - 1)
    def _():
        xf_ref[0] = st_ref[...]


def _s5_scan(proj3, wb, wc, ab, x0, seg):
    B, L = seg[0], seg[1]
    tc = 64 if B <= 8 else 16
    nc = L // tc
    tmap = lambda d, c: jnp.where(d == 0, c, nc - 1 - c)
    return pl.pallas_call(
        functools.partial(_s5_kernel, tc=tc, nb=B),
        grid=(2, nc),
        in_specs=[pl.BlockSpec((tc, B, S5W), lambda d, c: (tmap(d, c), 0, COL_U // S5W)),
                  pl.BlockSpec((1, S5_PARTS, S5_PW, 2 * S5_PS), lambda d, c: (d, 0, 0, 0)),
                  pl.BlockSpec((S5_PARTS, 2 * S5_PS, S5_PW), lambda d, c: (0, 0, 0)),
                  pl.BlockSpec((1, 2, S5S), lambda d, c: (d, 0, 0)),
                  pl.BlockSpec((1, 2, B, S5S), lambda d, c: (d, 0, 0, 0))],
        out_specs=[pl.BlockSpec((1, tc, B, S5W), lambda d, c: (d, tmap(d, c), 0, 0)),
                   pl.BlockSpec((1, 2, B, S5S), lambda d, c: (d, 0, 0, 0))],
        out_shape=[jax.ShapeDtypeStruct((2, L, B, S5W), F32),
                   jax.ShapeDtypeStruct((2, 2, B, S5S), F32)],
        scratch_shapes=[pltpu.VMEM((tc * B, 2 * S5S), F32), pltpu.VMEM((2, B, S5S), F32)],
        compiler_params=_cp(("arbitrary", "arbitrary")),
        name="s5_scan",
    )(proj3, wb, wc, ab, x0)


def _s5_params(lp):
    lam_re, lam_im = lp['s5_lam_re'], lp['s5_lam_im']
    dt = jnp.exp(lp['s5_log_step'])[:, :, None]
    mag = jnp.exp(lam_re * dt)
    ab_re, ab_im = mag * jnp.cos(lam_im * dt), mag * jnp.sin(lam_im * dt)
    den = lam_re * lam_re + lam_im * lam_im
    f_re = ((ab_re - 1.0) * lam_re + ab_im * lam_im) / den
    f_im = (ab_im * lam_re - (ab_re - 1.0) * lam_im) / den
    b_re, b_im = lp['s5_b_re'], lp['s5_b_im']
    wre = f_re[..., None] * b_re[None] - f_im[..., None] * b_im[None]
    wim = f_re[..., None] * b_im[None] + f_im[..., None] * b_re[None]
    eye = jnp.eye(S5G, dtype=F32)

    def block_in(w):
        return jnp.einsum('dgnc,gh->dgchn', w, eye).reshape(2, S5W, S5S)

    wb_re, wb_im = block_in(wre), block_in(wim)
    wb = jnp.stack([jnp.concatenate([wb_re[:, j * S5_PW:(j + 1) * S5_PW, j * S5_PS:(j + 1) * S5_PS],
                                     wb_im[:, j * S5_PW:(j + 1) * S5_PW, j * S5_PS:(j + 1) * S5_PS]], axis=-1)
                    for j in range(S5_PARTS)], axis=1).astype(BF16)

    def block_out(cm):
        return jnp.einsum('gcn,gh->gnhc', cm, eye).reshape(S5S, S5W)

    wc_re, wc_im = block_out(lp['s5_c_re']), -block_out(lp['s5_c_im'])
    wc = jnp.stack([jnp.concatenate([wc_re[j * S5_PS:(j + 1) * S5_PS, j * S5_PW:(j + 1) * S5_PW],
                                     wc_im[j * S5_PS:(j + 1) * S5_PS, j * S5_PW:(j + 1) * S5_PW]], axis=0)
                    for j in range(S5_PARTS)], axis=0).astype(BF16)
    ab = jnp.stack([ab_re.reshape(2, S5S), ab_im.reshape(2, S5S)], axis=1)
    return wb, wc, ab


def _mix_kernel(oatt_ref, osum_ref, sh_ref, ms_ref, y_ref, u_ref, seg_ref, lng_ref, lnb_ref,
                rk_ref, d_ref, wglu_ref, o_ref):
    segm = seg_ref[...]
    o = osum_ref[...]
    mean = _segsum(o, segm) * (1.0 / QK)
    oc = o - mean
    var = _segsum(oc * oc, segm) * (1.0 / QK)
    o_n = oc * lax.rsqrt(var + GN_EPS) * lng_ref[...] + lnb_ref[...]
    r = sh_ref[0]
    v = sh_ref[1]
    bonus = _segsum(r * 0.5 * ms_ref[1] * rk_ref[...], segm) * v
    rw = (o_n + bonus) * ms_ref[0]
    u = u_ref[...]
    y = d_ref[...] * u + y_ref[0] + y_ref[1]
    hg = 0.5 * y * (1.0 + jnp.tanh(math.sqrt(2.0 / math.pi) * (y + 0.044715 * (y * y * y))))
    s5 = hg * _sigmoid(_dot(hg.astype(BF16), wglu_ref[...]))
    o_ref[:, 0:ATT_W] = oatt_ref[...].astype(BF16)
    o_ref[:, ATT_W:ATT_W + RW] = rw.astype(BF16)
    o_ref[:, ATT_W + RW:D] = s5.astype(BF16)


def _mix_post(o_att, o_sum, shared, misc, y_s5, proj, segmat, lp, seg):
    B, L = seg[0], seg[1]
    tm = min(L, 256)
    nt = L // tm
    us = PROJ_STRIDE // S5W
    c2 = lambda b, t: (0, 0)
    return pl.pallas_call(
        _mix_kernel,
        grid=(B, nt),
        in_specs=[pl.BlockSpec((tm, ATT_W), lambda b, t: (b * nt + t, 0)),
                  pl.BlockSpec((tm, RW), lambda b, t: (t, b)),
                  pl.BlockSpec((3, tm, RW), lambda b, t: (0, t, b)),
                  pl.BlockSpec((2, tm, RW), lambda b, t: (0, t, b)),
                  pl.BlockSpec((2, tm, S5W), lambda b, t: (0, t, b)),
                  pl.BlockSpec((tm, S5W), lambda b, t: (t, b * us + COL_U // S5W)),
                  pl.BlockSpec((RW, RW), c2),
                  pl.BlockSpec((1, RW), c2),
                  pl.BlockSpec((1, RW), c2),
                  pl.BlockSpec((1, RW), c2),
                  pl.BlockSpec((1, S5W), c2),
                  pl.BlockSpec((S5W, S5W), c2)],
        out_specs=pl.BlockSpec((tm, D), lambda b, t: (b * nt + t, 0)),
        out_shape=jax.ShapeDtypeStruct((B * L, D), BF16),
        compiler_params=_cp(("arbitrary", "arbitrary")),
        name="mix_post",
    )(o_att, o_sum, shared, misc, y_s5.reshape(2, L, B * S5W), proj, segmat,
      lp['rwkv_ln_g'].reshape(1, RW), lp['rwkv_ln_b'].reshape(1, RW), lp['rwkv_r_k'].reshape(1, RW),
      lp['s5_d'].reshape(1, S5W), lp['s5_w_glu'].astype(BF16))


def _out_kernel(oc_ref, x_ref, w_ref, g1_ref, sc_ref, sh_ref, n2_ref, rt_ref, rb_ref,
                x1_ref, hp_ref, tw_ref, ti_ref):
    x1 = x_ref[...] + g1_ref[0] * _dot(oc_ref[...], w_ref[...])
    x1_ref[...] = x1
    y = x1 * lax.rsqrt(jnp.mean(x1 * x1, axis=-1, keepdims=True) + EPS)
    h2 = (y * n2_ref[...]) * (1.0 + sc_ref[0]) + sh_ref[0]
    hp_ref[...] = _pack_bf16_pair(h2[:, 0:D // 2], h2[:, D // 2:D])
    h_hi = h2.astype(BF16)
    h_lo = (h2 - h_hi.astype(F32)).astype(BF16)
    r_hi = rt_ref[0]
    r_lo = rt_ref[1]
    logits = _dot_nt(r_hi, h_hi) + (_dot_nt(r_hi, h_lo) + _dot_nt(r_lo, h_hi))
    scores = _sigmoid(logits)
    sel = scores + rb_ref[...]
    row = lax.broadcasted_iota(jnp.int32, sel.shape, 0)
    tws, tis = [], []
    for k in range(TOPK):
        m = jnp.max(sel, axis=0, keepdims=True)
        idx = jnp.min(jnp.where(sel == m, row, NE), axis=0, keepdims=True)
        hit = row == idx
        tws.append(jnp.sum(jnp.where(hit, scores, 0.0), axis=0, keepdims=True))
        tis.append(idx)
        sel = jnp.where(hit, -jnp.inf, sel)
    tw = jnp.concatenate(tws, axis=0)
    tw_ref[...] = tw / jnp.sum(tw, axis=0, keepdims=True) * ROUTE_SCALE
    ti_ref[...] = jnp.concatenate(tis, axis=0)


def _out_proj(o_cat, x, w_out_bf, mods6, norm2, router_t2, moe_bias, seg):
    B, L, mod_base, mod_stride = seg
    tm = min(L, 256)
    nt = L // tm

    def mrow(j):
        return lambda i: ((mod_base + (i // nt) * mod_stride) * 6 + j, 0, 0)

    c2 = lambda i: (0, 0)
    n = B * L
    return pl.pallas_call(
        _out_kernel,
        grid=(n // tm,),
        in_specs=[pl.BlockSpec((tm, D), lambda i: (i, 0)),
                  pl.BlockSpec((tm, D), lambda i: (i, 0)),
                  pl.BlockSpec((D, D), c2),
                  pl.BlockSpec((1, 1, D), mrow(2)),
                  pl.BlockSpec((1, 1, D), mrow(4)),
                  pl.BlockSpec((1, 1, D), mrow(3)),
                  pl.BlockSpec((1, D), c2),
                  pl.BlockSpec((2, NE, D), lambda i: (0, 0, 0)),
                  pl.BlockSpec((NE, 1), c2)],
        out_specs=[pl.BlockSpec((tm, D), lambda i: (i, 0)),
                   pl.BlockSpec((tm, D // 2), lambda i: (i, 0)),
                   pl.BlockSpec((TOPK, tm), lambda i: (0, i)),
                   pl.BlockSpec((TOPK, tm), lambda i: (0, i))],
        out_shape=[jax.ShapeDtypeStruct((n, D), F32),
                   jax.ShapeDtypeStruct((n, D // 2), U32),
                   jax.ShapeDtypeStruct((TOPK, n), F32),
                   jax.ShapeDtypeStruct((TOPK, n), jnp.int32)],
        compiler_params=_cp(("arbitrary",)),
        name="out_proj",
    )(o_cat, x, w_out_bf, mods6, mods6, mods6, norm2.reshape(1, D), router_t2, moe_bias.reshape(NE, 1))


DISP_TM = 128


def _dispatch_copy(hp_ref, xs_ref, sem, r, dst_row):
    return pltpu.make_async_copy(hp_ref.at[pl.ds(r, 1)], xs_ref.at[pl.ds(dst_row, 1)], sem.at[0])


def _dispatch_kernel(idx_ref, hp_ref, xs_in_ref, xs_ref, sem):
    del xs_in_ref
    for r in range(DISP_TM):
        for k in range(TOPK):
            _dispatch_copy(hp_ref, xs_ref, sem, r, idx_ref[0, 0, r * TOPK + k]).start(priority=k % 2)
    for r in range(DISP_TM):
        for k in range(TOPK):
            _dispatch_copy(hp_ref, xs_ref, sem, r, 0).wait()


def _moe_dispatch(slot_of, h2p, xs_init):
    n = slot_of.shape[0]
    tm = DISP_TM
    idx3 = slot_of.reshape(n // tm, 1, tm * TOPK)
    return pl.pallas_call(
        _dispatch_kernel,
        grid=(n // tm,),
        in_specs=[pl.BlockSpec((1, 1, tm * TOPK), lambda i: (i, 0, 0), memory_space=pltpu.SMEM),
                  pl.BlockSpec((tm, D // 2), lambda i: (i, 0)),
                  pl.BlockSpec(memory_space=pl.ANY)],
        out_specs=pl.BlockSpec(memory_space=pl.ANY),
        out_shape=jax.ShapeDtypeStruct(xs_init.shape, U32),
        scratch_shapes=[pltpu.SemaphoreType.DMA((1,))],
        input_output_aliases={2: 0},
        compiler_params=_cp(("arbitrary",)),
        name="moe_dispatch",
    )(idx3, h2p, xs_init)


def _moe_kernel(be_ref, nu_ref, x_ref, wgu_ref, wdn_ref, o_ref, wgu_bf, wdn_bf):
    i = pl.program_id(0)

    @pl.when(i < nu_ref[0])
    def _():
        @pl.when(jnp.logical_or(i == 0, be_ref[i] != be_ref[jnp.maximum(i - 1, 0)]))
        def _():
            wgu_bf[...] = wgu_ref[0, 0].astype(BF16)
            wdn_bf[...] = wdn_ref[0, 0].astype(BF16)

        for part in range(MOE_SPLIT):
            rows = slice(part * (MOE_BM // MOE_SPLIT), (part + 1) * (MOE_BM // MOE_SPLIT))
            lo, hi = _unpack_bf16_pair(x_ref[rows, :])
            h = (_dot(lo.astype(BF16), wgu_bf[0:D // 2, :])
                 + _dot(hi.astype(BF16), wgu_bf[D // 2:D, :]))
            gte = h[:, 0:EDIM]
            act = (gte * _sigmoid(gte)) * h[:, EDIM:2 * EDIM]
            y = _dot(act.astype(BF16), wdn_bf[...])
            o_ref[rows, :] = _pack_bf16_pair(y[:, 0:D // 2], y[:, D // 2:D])

    @pl.when(i >= nu_ref[0])
    def _():
        o_ref[...] = jnp.zeros(o_ref.shape, U32)


def _moe_experts(block_e, n_used, x_sorted, w_gu_all, w_dn_all, layer):
    n_slots = x_sorted.shape[0]
    nb = n_slots // MOE_BM
    grid_spec = pltpu.PrefetchScalarGridSpec(
        num_scalar_prefetch=2,
        grid=(nb,),
        in_specs=[pl.BlockSpec((MOE_BM, D // 2), lambda i, be, nu: (i, 0)),
                  pl.BlockSpec((1, 1, D, 2 * EDIM), lambda i, be, nu: (layer, be[i], 0, 0)),
                  pl.BlockSpec((1, 1, EDIM, D), lambda i, be, nu: (layer, be[i], 0, 0))],
        out_specs=pl.BlockSpec((MOE_BM, D // 2), lambda i, be, nu: (i, 0)),
        scratch_shapes=[pltpu.VMEM((D, 2 * EDIM), BF16),
                        pltpu.VMEM((EDIM, D), BF16)],
    )
    return pl.pallas_call(
        _moe_kernel,
        grid_spec=grid_spec,
        out_shape=jax.ShapeDtypeStruct((n_slots, D // 2), U32),
        compiler_params=_cp(("arbitrary",)),
        name="moe_experts",
    )(block_e, n_used, x_sorted, w_gu_all, w_dn_all)


COMB_TM = 128


def _comb_gather(idx_ref, y_hbm, ybuf, sem, slot):
    for r in range(COMB_TM):
        for k in range(TOPK):
            pltpu.make_async_copy(y_hbm.at[pl.ds(idx_ref[0, 0, r * TOPK + k], 1)],
                                  ybuf.at[slot, k, pl.ds(r, 1)], sem.at[slot]).start(priority=k % 2)


def _comb_wait(y_hbm, ybuf, sem, slot):
    for r in range(COMB_TM):
        for k in range(TOPK):
            pltpu.make_async_copy(y_hbm.at[pl.ds(0, 1)], ybuf.at[slot, k, pl.ds(r, 1)],
                                  sem.at[slot]).wait()


def _comb_kernel(idxc_ref, idxn_ref, y_hbm, hp_ref, tw_ref, x1_ref, g2_ref, wgu_ref, wdn_ref, fn_ref,
                 o_ref, ybuf, sem, *, final):
    i = pl.program_id(0)
    nb = pl.num_programs(0)
    slot = i % 2

    @pl.when(i == 0)
    def _():
        _comb_gather(idxc_ref, y_hbm, ybuf, sem, 0)

    _comb_gather(idxn_ref, y_hbm, ybuf, sem, 1 - slot)

    lo, hi = _unpack_bf16_pair(hp_ref[...])
    h = _dot(lo.astype(BF16), wgu_ref[0:D // 2, :]) + _dot(hi.astype(BF16), wgu_ref[D // 2:D, :])
    gte = h[:, 0:EDIM]
    act = (gte * _sigmoid(gte)) * h[:, EDIM:2 * EDIM]
    shared = _dot(act.astype(BF16), wdn_ref[...])

    _comb_wait(y_hbm, ybuf, sem, slot)
    rlo = jnp.zeros((COMB_TM, D // 2), F32)
    rhi = jnp.zeros((COMB_TM, D // 2), F32)
    tw = tw_ref[...]
    for k in range(TOPK):
        a, b = _unpack_bf16_pair(ybuf[slot, k])
        wk = tw[:, k:k + 1]
        rlo = rlo + wk * a
        rhi = rhi + wk * b
    g2 = g2_ref[0]
    x1 = x1_ref[...]
    out_lo = x1[:, 0:D // 2] + g2[:, 0:D // 2] * (rlo + shared[:, 0:D // 2])
    out_hi = x1[:, D // 2:D] + g2[:, D // 2:D] * (rhi + shared[:, D // 2:D])
    if final:
        ms = (jnp.sum(out_lo * out_lo, axis=-1, keepdims=True)
              + jnp.sum(out_hi * out_hi, axis=-1, keepdims=True)) * (1.0 / D)
        inv = lax.rsqrt(ms + EPS)
        fn = fn_ref[...]
        out_lo = out_lo * inv * fn[:, 0:D // 2]
        out_hi = out_hi * inv * fn[:, D // 2:D]
    o_ref[:, 0:D // 2] = out_lo
    o_ref[:, D // 2:D] = out_hi

    @pl.when(i == nb - 1)
    def _():
        _comb_wait(y_hbm, ybuf, sem, 1 - slot)


def _combine(slot_of, top_w, y_slots, h2p, x1, mods6, w_sgu_bf, w_sdn_bf, final_norm, seg, row0, final):
    B, L, mod_base, mod_stride = seg
    n = B * L
    tm = COMB_TM
    nb = n // tm
    nt = L // tm
    blk0 = row0 // tm
    idx3 = slot_of.reshape(-1, 1, tm * TOPK)
    c2 = lambda i: (0, 0)
    return pl.pallas_call(
        functools.partial(_comb_kernel, final=final),
        grid=(nb,),
        in_specs=[pl.BlockSpec((1, 1, tm * TOPK), lambda i: (blk0 + i, 0, 0), memory_space=pltpu.SMEM),
                  pl.BlockSpec((1, 1, tm * TOPK), lambda i: (blk0 + jnp.minimum(i + 1, nb - 1), 0, 0),
                               memory_space=pltpu.SMEM),
                  pl.BlockSpec(memory_space=pl.ANY),
                  pl.BlockSpec((tm, D // 2), lambda i: (blk0 + i, 0)),
                  pl.BlockSpec((tm, TOPK), lambda i: (blk0 + i, 0)),
                  pl.BlockSpec((tm, D), lambda i: (i, 0)),
                  pl.BlockSpec((1, 1, D), lambda i: ((mod_base + (i // nt) * mod_stride) * 6 + 5, 0, 0)),
                  pl.BlockSpec((D, 2 * EDIM), c2),
                  pl.BlockSpec((EDIM, D), c2),
                  pl.BlockSpec((1, D), c2)],
        out_specs=pl.BlockSpec((tm, D), lambda i: (i, 0)),
        out_shape=jax.ShapeDtypeStruct((n, D), F32),
        scratch_shapes=[pltpu.VMEM((2, TOPK, tm, D // 2), U32), pltpu.SemaphoreType.DMA((2,))],
        compiler_params=_cp(("arbitrary",)),
        name="moe_combine",
    )(idx3, idx3, y_slots, h2p, top_w, x1, mods6, w_sgu_bf, w_sdn_bf, final_norm.reshape(1, D))


def _routing_tables(top_i):
    n = top_i.shape[0]
    onehot = (top_i[:, :, None] == jnp.arange(NE, dtype=jnp.int32)[None, None, :])
    mask = jnp.any(onehot, axis=1).astype(jnp.int32)
    counts = jnp.sum(mask, axis=0)
    rank = jnp.cumsum(mask, axis=0) - mask
    padded = (counts + MOE_BM - 1) // MOE_BM * MOE_BM
    pad_end = jnp.cumsum(padded)
    pad_start = pad_end - padded
    slot_all = pad_start[None, :] + rank
    slot_of = jnp.take_along_axis(slot_all, top_i, axis=1).astype(jnp.int32)
    n_blocks = n * TOPK // MOE_BM + NE
    starts = jnp.arange(n_blocks, dtype=jnp.int32) * MOE_BM
    block_e = jnp.minimum(jnp.sum((pad_end[None, :] <= starts[:, None]).astype(jnp.int32), axis=1),
                          NE - 1).astype(jnp.int32)
    n_used = (pad_end[NE - 1:NE] // MOE_BM).astype(jnp.int32)
    return slot_of, block_e, n_used


def _rope_tables(n_tok):
    rows = n_tok // GRID_W
    row = jnp.repeat(jnp.arange(rows), GRID_W).astype(F32)
    col = jnp.tile(jnp.arange(GRID_W), rows).astype(F32)
    half = QK // 2
    inv = ROPE_THETA ** (-jnp.arange(0, half, 2, dtype=F32) / half)
    ang = jnp.concatenate([row[:, None] * inv, col[:, None] * inv], axis=-1)
    cos = jnp.repeat(jnp.cos(ang), 2, axis=-1)
    sin = jnp.repeat(jnp.sin(ang), 2, axis=-1)
    sign = jnp.tile(jnp.array([-1.0, 1.0], F32), QK // 2)
    return jnp.tile(cos, (1, 2)), jnp.tile(sin * sign, (1, 2))


def kernel(x_prompt, x_sample, cache_attn_k, cache_attn_v, state_rwkv, state_s5, c, c_ctx, w_mod, b_mod, norm1, norm2, w_in, w_out, att_lambda, att_subln, rwkv_mu, rwkv_w0, rwkv_w_up, rwkv_a0, rwkv_a_up, rwkv_g_up, rwkv_k_k, rwkv_k_a, rwkv_r_k, rwkv_ln_g, rwkv_ln_b, s5_lam_re, s5_lam_im, s5_log_step, s5_b_re, s5_b_im, s5_c_re, s5_c_im, s5_d, s5_w_glu, moe_router, moe_bias, moe_w_gate_up, moe_w_down, shared_w_gate_up, shared_w_down, final_norm):
    params = dict(w_mod=w_mod, b_mod=b_mod, norm1=norm1, norm2=norm2, w_in=w_in, w_out=w_out,
                  att_lambda=att_lambda, att_subln=att_subln,
                  rwkv_mu=rwkv_mu, rwkv_w0=rwkv_w0, rwkv_w_up=rwkv_w_up, rwkv_a0=rwkv_a0,
                  rwkv_a_up=rwkv_a_up, rwkv_g_up=rwkv_g_up, rwkv_k_k=rwkv_k_k, rwkv_k_a=rwkv_k_a,
                  rwkv_r_k=rwkv_r_k, rwkv_ln_g=rwkv_ln_g, rwkv_ln_b=rwkv_ln_b,
                  s5_lam_re=s5_lam_re, s5_lam_im=s5_lam_im, s5_log_step=s5_log_step,
                  s5_b_re=s5_b_re, s5_b_im=s5_b_im, s5_c_re=s5_c_re, s5_c_im=s5_c_im,
                  s5_d=s5_d, s5_w_glu=s5_w_glu,
                  moe_router=moe_router, moe_bias=moe_bias, moe_w_gate_up=moe_w_gate_up,
                  moe_w_down=moe_w_down, shared_w_gate_up=shared_w_gate_up,
                  shared_w_down=shared_w_down)
    bp, lp_len, _ = x_prompt.shape
    bs, ls_len, _ = x_sample.shape
    segs = ((bp, lp_len, 0, 0), (bs, ls_len, 1, 1))
    xs = [x_prompt.reshape(bp * lp_len, D), x_sample.reshape(bs * ls_len, D)]
    cpad = jnp.zeros((16, D), F32).at[0].set(c_ctx).at[1:1 + bs].set(c)
    new_k, new_v, new_r, new_s = [], [], [], []
    n_slots = ((bp * lp_len + bs * ls_len) * TOPK // MOE_BM + NE) * MOE_BM
    x_sorted = jnp.zeros((n_slots, D // 2), U32)
    for l in range(DEPTH):
        lp = {name: arr[l] for name, arr in params.items() if name not in _WHOLE}
        whole = {name: params[name] for name in _WHOLE}
        lam_init = 0.8 - 0.6 * math.exp(-0.3 * l)
        ctxs = (None, (cache_attn_k[:, l], cache_attn_v[:, l], state_rwkv[:, l], state_s5[:, l]))
        xs, caches, x_sorted = _layer(xs, segs, cpad, lp, whole, l, lam_init, ctxs, final_norm,
                                      l == DEPTH - 1, x_sorted)
        ck, cv, cr, cs = caches[0]
        new_k.append(ck)
        new_v.append(cv)
        new_r.append(cr)
        new_s.append(cs)
    return (xs[0].reshape(bp, lp_len, D), xs[1].reshape(bs, ls_len, D),
            jnp.stack(new_k, axis=1), jnp.stack(new_v, axis=1),
            jnp.stack(new_r, axis=1), jnp.stack(new_s, axis=1))


_WHOLE = ('w_mod', 'moe_w_gate_up', 'moe_w_down')


def _layer(xs, segs, cpad, lp, whole, layer, lam_init, ctxs, final_norm, final, x_sorted):
    hh = jnp.arange(RW) // QK
    segmat = (hh[:, None] == hh[None, :]).astype(BF16)
    perm = jnp.concatenate([jnp.arange(0, 3072), jnp.arange(3072 + RW_PROJ, PROJ_W),
                            jnp.arange(3072, 3072 + RW_PROJ)])
    mods6 = _modulation(cpad, whole['w_mod'], lp['b_mod'], layer).reshape(16 * 6, 1, D)
    w_in_bf = lp['w_in'][:, perm].astype(BF16)
    w_out_bf = lp['w_out'].astype(BF16)
    wb, wc, ab = _s5_params(lp)
    rt = lp['moe_router'].T
    rt_hi = rt.astype(BF16)
    router_t2 = jnp.stack([rt_hi, (rt - rt_hi.astype(F32)).astype(BF16)], axis=0)
    x1s, h2ps, tws, tis, caches = [], [], [], [], []
    for si, seg in enumerate(segs):
        B, L = seg[0], seg[1]
        proj = _in_proj(xs[si], lp['norm1'], mods6, w_in_bf, seg)
        proj3 = proj.reshape(L, B, PROJ_STRIDE)
        if ctxs[si] is None:
            o_att = _attention(proj, lp['att_lambda'], lp['att_subln'], lam_init, seg)
            s0_rwkv = None
            x0 = jnp.zeros((2, 2, B, S5S), F32)
        else:
            ck, cv, s0_rwkv, s0_s5 = ctxs[si]
            ctx = (ck.reshape(B, -1, ATT_W), cv.reshape(B, -1, ATT_W))
            o_att = _attention(proj, lp['att_lambda'], lp['att_subln'], lam_init, seg, ctx,
                               _rope_tables(L))
            x0 = s0_s5.astype(F32).reshape(B, 2, 2, S5S).transpose(1, 2, 0, 3)
        o_sum, shared, misc, s_fin = _rwkv_mix(proj, lp, segmat, seg, s0_rwkv)
        y_s5, xf = _s5_scan(proj3, wb, wc, ab, x0, seg)
        o_cat = _mix_post(o_att, o_sum, shared, misc, y_s5, proj, segmat, lp, seg)
        x1, h2p, tw, ti = _out_proj(o_cat, xs[si], w_out_bf, mods6, lp['norm2'],
                                    router_t2, lp['moe_bias'], seg)
        x1s.append(x1)
        h2ps.append(h2p)
        tws.append(tw)
        tis.append(ti)
        caches.append((proj3[:, :, COL_K:COL_K + ATT_W].transpose(1, 0, 2).reshape(B, L, HEADS, 2, QK),
                       proj3[:, :, COL_V:COL_V + ATT_W].transpose(1, 0, 2).reshape(B, L, HEADS, 2 * QK),
                       s_fin,
                       xf.transpose(2, 0, 1, 3).reshape(B, 2, 2, S5G, S5N)))
    h2p_all = jnp.concatenate(h2ps, axis=0)
    top_w = jnp.concatenate(tws, axis=1).T
    slot_of, block_e, n_used = _routing_tables(jnp.concatenate(tis, axis=1).T)
    x_sorted = _moe_dispatch(slot_of, h2p_all, x_sorted)
    y_slots = _moe_experts(block_e, n_used, x_sorted, whole['moe_w_gate_up'], whole['moe_w_down'], layer)
    w_sgu_bf = lp['shared_w_gate_up'].astype(BF16)
    w_sdn_bf = lp['shared_w_down'].astype(BF16)
    outs = []
    row0 = 0
    for si, seg in enumerate(segs):
        outs.append(_combine(slot_of, top_w, y_slots, h2p_all, x1s[si], mods6, w_sgu_bf, w_sdn_bf,
                             final_norm, seg, row0, final))
        row0 += seg[0] * seg[1]
    return outs, caches, x_sorted
```

```python
import functools
import math

import jax
import jax.numpy as jnp
from jax import lax
from jax.experimental import pallas as pl
from jax.experimental.pallas import tpu as pltpu

F32 = jnp.float32
BF16 = jnp.bfloat16
U32 = jnp.uint32

D = 2048
DEPTH = 2
EPS = 1e-6
GRID_W = 64
ROPE_THETA = 10000.0
HEADS = 8
QK = 64
ATT_W = 1024
RW = 512
RW_PROJ = 1792
S5W = 512
S5G = 32
S5N = 64
S5C = 16
S5S = S5G * S5N
S5_PARTS = 4
S5_PW = S5W // S5_PARTS
S5_PS = S5S // S5_PARTS
GN_EPS = 64e-5
NE = 64
TOPK = 8
EDIM = 512
ROUTE_SCALE = 2.5
PROJ_W = 5376
PROJ_STRIDE = 7168
COL_Q, COL_K, COL_V, COL_U, COL_Z = 0, 1024, 2048, 3072, 3584
PROJ_TN = 1792
MOE_BM = 512
MOE_SPLIT = 2
ATT_TQS = 128
ATT_TK = 256
VMEM_LIMIT = 56 * 1024 * 1024


def _cp(sem):
    return pltpu.CompilerParams(dimension_semantics=sem, vmem_limit_bytes=VMEM_LIMIT)


def _sigmoid(x):
    return 1.0 / (1.0 + jnp.exp(-x))


def _dot(a, b):
    return jnp.dot(a, b, preferred_element_type=F32)


def _dot_nt(a, b):
    return lax.dot_general(a, b, (((1,), (1,)), ((), ())), preferred_element_type=F32)


def _split_bf16(x):
    hi = x.astype(BF16)
    return hi, (x - hi.astype(F32)).astype(BF16)


def _dot_hi(a, b):
    a_hi, a_lo = _split_bf16(a)
    b_hi, b_lo = _split_bf16(b)
    return _dot(a_hi, b_hi) + (_dot(a_hi, b_lo) + _dot(a_lo, b_hi))


def _segsum(x, seg_bf):
    x_hi, x_lo = _split_bf16(x)
    return _dot(x_hi, seg_bf) + _dot(x_lo, seg_bf)


def _bf16_bits(x):
    b = lax.bitcast_convert_type(x, U32)
    return b + jnp.uint32(0x7FFF) + ((b >> 16) & jnp.uint32(1))


def _pack_bf16_pair(lo, hi):
    return (_bf16_bits(lo) >> 16) | (_bf16_bits(hi) & jnp.uint32(0xFFFF0000))


def _unpack_bf16_pair(p):
    lo = lax.bitcast_convert_type(p << 16, F32)
    hi = lax.bitcast_convert_type(p & jnp.uint32(0xFFFF0000), F32)
    return lo, hi


def _mod_kernel(c_ref, w_ref, b_ref, o_ref):
    c = c_ref[...]
    s = c * _sigmoid(c)
    o_ref[...] = _dot(s.astype(BF16), w_ref[0].astype(BF16)) + b_ref[...]


def _modulation(cpad, w_mod_all, b_mod, layer):
    tn = 1024
    return pl.pallas_call(
        _mod_kernel,
        grid=(6 * D // tn,),
        in_specs=[pl.BlockSpec((16, D), lambda j: (0, 0)),
                  pl.BlockSpec((1, D, tn), lambda j: (layer, 0, j)),
                  pl.BlockSpec((1, tn), lambda j: (0, j))],
        out_specs=pl.BlockSpec((16, tn), lambda j: (0, j)),
        out_shape=jax.ShapeDtypeStruct((16, 6 * D), F32),
        compiler_params=_cp(("arbitrary",)),
        name="modulation",
    )(cpad, w_mod_all, b_mod.reshape(1, 6 * D))


def _in_kernel(x_ref, g_ref, sc_ref, sh_ref, w_ref, o_ref):
    x = x_ref[...]
    y = x * lax.rsqrt(jnp.mean(x * x, axis=-1, keepdims=True) + EPS)
    h = (y * g_ref[...]) * (1.0 + sc_ref[0]) + sh_ref[0]
    o_ref[...] = _dot(h.astype(BF16), w_ref[...])


def _in_proj(x, norm_g, mods6, w_in_bf, seg):
    B, L, mod_base, mod_stride = seg
    tm = min(L, 512)
    nt = L // tm
    ncol = PROJ_W // PROJ_TN
    nstride = PROJ_STRIDE // PROJ_TN

    def mrow(j):
        return lambda c, i: ((mod_base + (i // nt) * mod_stride) * 6 + j, 0, 0)

    return pl.pallas_call(
        _in_kernel,
        grid=(ncol, B * nt),
        in_specs=[pl.BlockSpec((tm, D), lambda c, i: (i, 0)),
                  pl.BlockSpec((1, D), lambda c, i: (0, 0)),
                  pl.BlockSpec((1, 1, D), mrow(1)),
                  pl.BlockSpec((1, 1, D), mrow(0)),
                  pl.BlockSpec((D, PROJ_TN), lambda c, i: (0, c))],
        out_specs=pl.BlockSpec((tm, PROJ_TN), lambda c, i: (i % nt, (i // nt) * nstride + c)),
        out_shape=jax.ShapeDtypeStruct((L, B * PROJ_STRIDE), F32),
        compiler_params=_cp(("arbitrary", "arbitrary")),
        name="in_proj",
    )(x, norm_g.reshape(1, D), mods6, mods6, w_in_bf)


def _rope(x, c, s):
    lane = lax.broadcasted_iota(jnp.int32, x.shape, 1)
    nxt = pltpu.roll(x, 127, 1)
    prv = pltpu.roll(x, 1, 1)
    swapped = jnp.where((lane & 1) == 0, nxt, prv)
    return x * c + swapped * s


def _attn_kernel(*refs, lam_init, n_ctx, rope, hb):
    if rope:
        (q_ref, k_ref, v_ref, ck_ref, cv_ref, cq_ref, sq_ref, ckk_ref, skk_ref,
         lam_ref, g_ref, o_ref, kall_ref, vall_ref) = refs
    else:
        q_ref, k_ref, v_ref, lam_ref, g_ref, o_ref, kall_ref, vall_ref = refs

    @pl.when(pl.program_id(2) == 0)
    def _():
        for hh in range(hb):
            cols = slice(hh * 128, (hh + 1) * 128)
            k = k_ref[:, cols]
            if rope:
                k = _rope(k, ckk_ref[...], skk_ref[...])
                kall_ref[hh, 0:n_ctx, :] = ck_ref[0, :, cols].astype(BF16)
                vall_ref[hh, 0:n_ctx, 0:128] = cv_ref[0, :, cols].astype(BF16)
            kall_ref[hh, n_ctx:, :] = k.astype(BF16)
            vall_ref[hh, n_ctx:, 0:128] = v_ref[:, cols].astype(BF16)
            vall_ref[hh, :, 128:256] = jnp.ones((vall_ref.shape[1], 128), BF16)

    lv = lam_ref[...]
    lam = (jnp.exp(jnp.sum(lv[0:1] * lv[1:2], axis=-1, keepdims=True))
           - jnp.exp(jnp.sum(lv[2:3] * lv[3:4], axis=-1, keepdims=True)) + lam_init)
    n_kt = kall_ref.shape[1] // ATT_TK
    lane = lax.broadcasted_iota(jnp.int32, (ATT_TQS, 128), 1)
    for hh in range(hb):
        cols = slice(hh * 128, (hh + 1) * 128)
        q = q_ref[:, cols]
        if rope:
            q = _rope(q, cq_ref[...], sq_ref[...])
        q = q * (QK ** -0.5 * math.log2(math.e))
        for qs in range(q.shape[0] // ATT_TQS):
            rows = slice(qs * ATT_TQS, (qs + 1) * ATT_TQS)
            outs = []
            for m in range(2):
                qm = jnp.where((lane < QK) == (m == 0), q[rows], 0.0).astype(BF16)
                macc = _dot_nt(qm, kall_ref[hh, 0:ATT_TK, :])
                for kt in range(1, n_kt):
                    macc = jnp.maximum(macc, _dot_nt(qm, kall_ref[hh, kt * ATT_TK:(kt + 1) * ATT_TK, :]))
                mx = jnp.max(macc, axis=-1, keepdims=True)
                acc = jnp.zeros((ATT_TQS, 256), F32)
                for kt in range(n_kt):
                    keys = slice(kt * ATT_TK, (kt + 1) * ATT_TK)
                    e = jnp.exp2(_dot_nt(qm, kall_ref[hh, keys, :]) - mx).astype(BF16)
                    acc = acc + _dot(e, vall_ref[hh, keys, :])
                outs.append(acc[:, 0:128] / acc[:, 128:129])
            o = outs[0] - lam * outs[1]
            o = o * lax.rsqrt(jnp.mean(o * o, axis=-1, keepdims=True) + EPS) * g_ref[...]
            o_ref[rows, cols] = o * (1.0 - lam_init)


def _attention(proj, att_lambda, subln, lam_init, seg, ctx=None, tables=None):
    B, L = seg[0], seg[1]
    tq = min(L, 512)
    nq = L // tq
    rope = ctx is not None
    n_ctx = ctx[0].shape[1] if rope else 0
    hb = 1 if rope else HEADS
    bw = 128 * hb
    cs = PROJ_STRIDE // bw
    specs = [pl.BlockSpec((tq, bw), lambda b, h, t: (t, b * cs + COL_Q // bw + h)),
             pl.BlockSpec((L, bw), lambda b, h, t: (0, b * cs + COL_K // bw + h)),
             pl.BlockSpec((L, bw), lambda b, h, t: (0, b * cs + COL_V // bw + h))]
    args = [proj, proj, proj]
    if rope:
        cos_t, sin_t = tables
        specs += [pl.BlockSpec((1, n_ctx, bw), lambda b, h, t: (b, 0, h)),
                  pl.BlockSpec((1, n_ctx, bw), lambda b, h, t: (b, 0, h)),
                  pl.BlockSpec((tq, 128), lambda b, h, t: (t, 0)),
                  pl.BlockSpec((tq, 128), lambda b, h, t: (t, 0)),
                  pl.BlockSpec((L, 128), lambda b, h, t: (0, 0)),
                  pl.BlockSpec((L, 128), lambda b, h, t: (0, 0))]
        args += [ctx[0], ctx[1], cos_t, sin_t, cos_t, sin_t]
    specs += [pl.BlockSpec((4, QK), lambda b, h, t: (0, 0)),
              pl.BlockSpec((1, 128), lambda b, h, t: (0, 0))]
    args += [att_lambda, subln.reshape(1, 128)]
    return pl.pallas_call(
        functools.partial(_attn_kernel, lam_init=lam_init, n_ctx=n_ctx, rope=rope, hb=hb),
        grid=(B, HEADS // hb, nq),
        in_specs=specs,
        out_specs=pl.BlockSpec((tq, bw), lambda b, h, t: (b * nq + t, h)),
        out_shape=jax.ShapeDtypeStruct((B * L, ATT_W), F32),
        scratch_shapes=[pltpu.VMEM((hb, n_ctx + L, 128), BF16), pltpu.VMEM((hb, n_ctx + L, 256), BF16)],
        compiler_params=_cp(("arbitrary", "arbitrary", "arbitrary")),
        name="diff_attention",
    )(*args)


def _rwkv_pre_kernel(z_ref, zp_ref, zn_ref, mu_ref, seg_ref, wup_ref, aup_ref, gup_ref,
                     w0_ref, a0_ref, kk_ref, ka_ref, sh_ref, pd_ref, ms_ref, *, tm):
    t = pl.program_id(1)
    nt = pl.num_programs(1)
    z = z_ref[...]
    row = lax.broadcasted_iota(jnp.int32, z.shape, 0)
    prev_row = jnp.where(t > 0, zp_ref[7:8, :], 0.0)
    next_row = jnp.where(t < nt - 1, zn_ref[0:1, :], 0.0)
    zp = jnp.where(row == 0, prev_row, pltpu.roll(z, 1, 0))
    zn = jnp.where(row == tm - 1, next_row, pltpu.roll(z, tm - 1, 0))
    zs = z + mu_ref[...] * (0.5 * (zp + zn) - z)
    r = zs[:, 0:RW]
    k = zs[:, RW:2 * RW]
    v = zs[:, 2 * RW:3 * RW]
    wa = zs[:, 3 * RW:3 * RW + 128]
    gl = zs[:, 3 * RW + 128:3 * RW + 256]
    lane = lax.broadcasted_iota(jnp.int32, wa.shape, 1)
    wa = jnp.where(lane < 64, jnp.tanh(wa), wa)
    g = _dot_hi(_sigmoid(gl), gup_ref[...])
    kk = k * kk_ref[...]
    kk = kk * lax.rsqrt(_segsum(kk * kk, seg_ref[...]) + EPS)
    sh_ref[0] = r
    sh_ref[1] = v
    sh_ref[2] = kk
    kb = jnp.zeros_like(k)
    for d in range(2):
        xw = w0_ref[d] + _dot_hi(wa, wup_ref[d])
        w = jnp.exp(-math.exp(-0.5) * _sigmoid(xw))
        a = _sigmoid(a0_ref[d] + _dot_hi(wa, aup_ref[d]))
        kd = k * (1.0 + (a - 1.0) * ka_ref[...])
        pd_ref[0, d] = r
        pd_ref[1, d] = v
        pd_ref[2, d] = kk
        pd_ref[3, d] = w
        pd_ref[4, d] = kd
        pd_ref[5, d] = kk * a
        kb = kb + kd
    ms_ref[0] = g
    ms_ref[1] = kb


def _rwkv_pre(proj, lp, segmat, seg):
    B, L = seg[0], seg[1]
    tm = min(L, 256)
    nt = L // tm
    zs = PROJ_STRIDE // RW_PROJ
    zc = COL_Z // RW_PROJ
    nb8 = L // 8
    zeros64 = jnp.zeros((2, 64, RW), F32)
    wup = jnp.concatenate([lp['rwkv_w_up'], zeros64], axis=1)
    aup = jnp.concatenate([zeros64, lp['rwkv_a_up']], axis=1)
    c2 = lambda b, t: (0, 0)
    c3 = lambda b, t: (0, 0, 0)
    return pl.pallas_call(
        functools.partial(_rwkv_pre_kernel, tm=tm),
        grid=(B, nt),
        in_specs=[pl.BlockSpec((tm, RW_PROJ), lambda b, t: (t, b * zs + zc)),
                  pl.BlockSpec((8, RW_PROJ), lambda b, t: (jnp.maximum(t * (tm // 8) - 1, 0), b * zs + zc)),
                  pl.BlockSpec((8, RW_PROJ), lambda b, t: (jnp.minimum((t + 1) * (tm // 8), nb8 - 1), b * zs + zc)),
                  pl.BlockSpec((1, RW_PROJ), c2),
                  pl.BlockSpec((RW, RW), c2),
                  pl.BlockSpec((2, 128, RW), c3),
                  pl.BlockSpec((2, 128, RW), c3),
                  pl.BlockSpec((128, RW), c2),
                  pl.BlockSpec((2, 1, RW), c3),
                  pl.BlockSpec((2, 1, RW), c3),
                  pl.BlockSpec((1, RW), c2),
                  pl.BlockSpec((1, RW), c2)],
        out_specs=[pl.BlockSpec((3, tm, RW), lambda b, t: (0, t, b)),
                   pl.BlockSpec((6, 2, tm, RW), lambda b, t: (0, 0, t, b)),
                   pl.BlockSpec((2, tm, RW), lambda b, t: (0, t, b))],
        out_shape=[jax.ShapeDtypeStruct((3, L, B * RW), F32),
                   jax.ShapeDtypeStruct((6, 2, L, B * RW), F32),
                   jax.ShapeDtypeStruct((2, L, B * RW), F32)],
        compiler_params=_cp(("arbitrary", "arbitrary")),
        name="rwkv_pre",
    )(proj, proj, proj, lp['rwkv_mu'].reshape(1, RW_PROJ), segmat, wup, aup, lp['rwkv_g_up'],
      lp['rwkv_w0'].reshape(2, 1, RW), lp['rwkv_a0'].reshape(2, 1, RW),
      lp['rwkv_k_k'].reshape(1, RW), lp['rwkv_k_a'].reshape(1, RW))


def _wkv_kernel(xa_ref, xb_ref, s0_ref, yf_ref, yr_ref, sf_ref, st_ref, *, tc):
    c = pl.program_id(1)

    @pl.when(c == 0)
    def _():
        st_ref[...] = s0_ref[...]

    fwd = lax.broadcasted_iota(jnp.int32, (QK, 128), 1) < 64

    def step(t, carry):
        tr = tc - 1 - t

        def tile(q):
            return jnp.where(fwd, xa_ref[q, t, 0], xb_ref[q, tr, 0])

        r_t = tile(0)
        v_t = tile(1)
        kk_t = tile(2)
        w_t = tile(3)
        kd_t = tile(4)
        b_t = tile(5)
        wr = w_t * r_t
        kr = jnp.sum(kd_t * r_t, axis=0, keepdims=True)
        br = jnp.sum(b_t * r_t, axis=0, keepdims=True)
        ytiles = []
        for g in range(QK // 8):
            ys = []
            for j in range(8):
                vi = g * 8 + j
                s = st_ref[vi]
                sa = jnp.sum(s * kk_t, axis=0, keepdims=True)
                y0 = jnp.sum(s * wr, axis=0, keepdims=True)
                vv = v_t[vi:vi + 1]
                st_ref[vi] = s * w_t + (vv * kd_t - sa * b_t)
                ys.append(y0 + vv * kr - sa * br)
            ytiles.append(jnp.concatenate(ys, axis=0))
        y_t = jnp.transpose(jnp.concatenate(ytiles, axis=0))
        yf_ref[t, 0] = y_t[0:64]
        yr_ref[tr, 0] = y_t[64:128]
        return carry

    lax.fori_loop(0, tc, step, 0)

    @pl.when(c == pl.num_programs(1) - 1)
    def _():
        sf_ref[...] = st_ref[...]


def _wkv_scan(xs, s0):
    _, L, G, _, _ = xs.shape
    lanes = G * 128
    tc = 16
    nc = L // tc
    return pl.pallas_call(
        functools.partial(_wkv_kernel, tc=tc),
        grid=(G, nc),
        in_specs=[pl.BlockSpec((6, tc, 1, QK, 128), lambda g, c: (0, c, g, 0, 0)),
                  pl.BlockSpec((6, tc, 1, QK, 128), lambda g, c: (0, nc - 1 - c, g, 0, 0)),
                  pl.BlockSpec((QK, QK, 128), lambda g, c: (0, 0, g))],
        out_specs=[pl.BlockSpec((tc, 1, 64, QK), lambda g, c: (c, g, 0, 0)),
                   pl.BlockSpec((tc, 1, 64, QK), lambda g, c: (nc - 1 - c, g, 0, 0)),
                   pl.BlockSpec((QK, QK, 128), lambda g, c: (0, 0, g))],
        out_shape=[jax.ShapeDtypeStruct((L, lanes // 128, 64, QK), F32),
                   jax.ShapeDtypeStruct((L, lanes // 128, 64, QK), F32),
                   jax.ShapeDtypeStruct((QK, QK, lanes), F32)],
        scratch_shapes=[pltpu.VMEM((QK, QK, 128), F32)],
        compiler_params=_cp(("arbitrary", "arbitrary")),
        name="wkv7_scan",
    )(xs, xs, s0)


def _rwkv_mix(proj, lp, segmat, seg, s0_bdhvk):
    B, L = seg[0], seg[1]
    G = B // 8
    shared, perdir, misc = _rwkv_pre(proj, lp, segmat, seg)
    pd = jnp.moveaxis(perdir.reshape(6, 2, L, G, 8 * HEADS, QK), 1, 3)
    xs = jnp.swapaxes(pd.reshape(6, L, G, 128, QK), -1, -2)
    if s0_bdhvk is None:
        s0 = jnp.zeros((QK, QK, G * 128), F32)
    else:
        s0 = (s0_bdhvk.astype(F32).reshape(G, 8, 2, HEADS, QK, QK)
              .transpose(4, 5, 0, 2, 1, 3).reshape(QK, QK, G * 128))
    yf, yr, sf = _wkv_scan(xs, s0)
    o_sum = (yf + yr).reshape(L, B * RW)
    s_fin = (sf.reshape(QK, QK, G, 2, 8, HEADS).transpose(2, 4, 3, 5, 0, 1)
             .reshape(B, 2, HEADS, QK, QK))
    return o_sum, shared, misc, s_fin


def _s5_kernel(u_ref, wb_ref, wc_ref, ab_ref, x0_ref, y_ref, xf_ref, bx_ref, st_ref, *, tc, nb):
    d = pl.program_id(0)
    c = pl.program_id(1)

    @pl.when(c == 0)
    def _():
        st_ref[...] = x0_ref[0]

    u = u_ref[...].reshape(tc * nb, S5W).astype(BF16)
    for j in range(S5_PARTS):
        bj = _dot(u[:, j * S5_PW:(j + 1) * S5_PW], wb_ref[0, j])
        bx_ref[:, j * S5_PS:(j + 1) * S5_PS] = bj[:, 0:S5_PS]
        bx_ref[:, S5S + j * S5_PS:S5S + (j + 1) * S5_PS] = bj[:, S5_PS:2 * S5_PS]
    ar = jnp.broadcast_to(ab_ref[0, 0:1, :], (nb, S5S))
    ai = jnp.broadcast_to(ab_ref[0, 1:2, :], (nb, S5S))

    def step(i, carry):
        tt = jnp.where(d == 0, i, tc - 1 - i)
        rows = pl.ds(pl.multiple_of(tt * nb, nb), nb)
        xr = st_ref[0]
        xi = st_ref[1]
        nr = ar * xr - ai * xi + bx_ref[rows, 0:S5S]
        ni = ar * xi + ai * xr + bx_ref[rows, S5S:2 * S5S]
        st_ref[0] = nr
        st_ref[1] = ni
        bx_ref[rows, 0:S5S] = nr
        bx_ref[rows, S5S:2 * S5S] = ni
        return carry

    lax.fori_loop(0, tc, step, 0)
    ys = []
    for j in range(S5_PARTS):
        xr = bx_ref[:, j * S5_PS:(j + 1) * S5_PS].astype(BF16)
        xi = bx_ref[:, S5S + j * S5_PS:S5S + (j + 1) * S5_PS].astype(BF16)
        ys.append(_dot(xr, wc_ref[j, 0:S5_PS, :]) + _dot(xi, wc_ref[j, S5_PS:2 * S5_PS, :]))
    y_ref[0] = jnp.concatenate(ys, axis=-1).reshape(tc, nb, S5W)

    @pl.when(c == pl.num_programs(1) - 1)
    def _():
        xf_ref[0] = st_ref[...]


def _s5_scan(proj3, wb, wc, ab, x0, seg):
    B, L = seg[0], seg[1]
    tc = 64 if B <= 8 else 16
    nc = L // tc
    tmap = lambda d, c: jnp.where(d == 0, c, nc - 1 - c)
    return pl.pallas_call(
        functools.partial(_s5_kernel, tc=tc, nb=B),
        grid=(2, nc),
        in_specs=[pl.BlockSpec((tc, B, S5W), lambda d, c: (tmap(d, c), 0, COL_U // S5W)),
                  pl.BlockSpec((1, S5_PARTS, S5_PW, 2 * S5_PS), lambda d, c: (d, 0, 0, 0)),
                  pl.BlockSpec((S5_PARTS, 2 * S5_PS, S5_PW), lambda d, c: (0, 0, 0)),
                  pl.BlockSpec((1, 2, S5S), lambda d, c: (d, 0, 0)),
                  pl.BlockSpec((1, 2, B, S5S), lambda d, c: (d, 0, 0, 0))],
        out_specs=[pl.BlockSpec((1, tc, B, S5W), lambda d, c: (d, tmap(d, c), 0, 0)),
                   pl.BlockSpec((1, 2, B, S5S), lambda d, c: (d, 0, 0, 0))],
        out_shape=[jax.ShapeDtypeStruct((2, L, B, S5W), F32),
                   jax.ShapeDtypeStruct((2, 2, B, S5S), F32)],
        scratch_shapes=[pltpu.VMEM((tc * B, 2 * S5S), F32), pltpu.VMEM((2, B, S5S), F32)],
        compiler_params=_cp(("arbitrary", "arbitrary")),
        name="s5_scan",
    )(proj3, wb, wc, ab, x0)


def _s5_params(lp):
    lam_re, lam_im = lp['s5_lam_re'], lp['s5_lam_im']
    dt = jnp.exp(lp['s5_log_step'])[:, :, None]
    mag = jnp.exp(lam_re * dt)
    ab_re, ab_im = mag * jnp.cos(lam_im * dt), mag * jnp.sin(lam_im * dt)
    den = lam_re * lam_re + lam_im * lam_im
    f_re = ((ab_re - 1.0) * lam_re + ab_im * lam_im) / den
    f_im = (ab_im * lam_re - (ab_re - 1.0) * lam_im) / den
    b_re, b_im = lp['s5_b_re'], lp['s5_b_im']
    wre = f_re[..., None] * b_re[None] - f_im[..., None] * b_im[None]
    wim = f_re[..., None] * b_im[None] + f_im[..., None] * b_re[None]
    eye = jnp.eye(S5G, dtype=F32)

    def block_in(w):
        return jnp.einsum('dgnc,gh->dgchn', w, eye).reshape(2, S5W, S5S)

    wb_re, wb_im = block_in(wre), block_in(wim)
    wb = jnp.stack([jnp.concatenate([wb_re[:, j * S5_PW:(j + 1) * S5_PW, j * S5_PS:(j + 1) * S5_PS],
                                     wb_im[:, j * S5_PW:(j + 1) * S5_PW, j * S5_PS:(j + 1) * S5_PS]], axis=-1)
                    for j in range(S5_PARTS)], axis=1).astype(BF16)

    def block_out(cm):
        return jnp.einsum('gcn,gh->gnhc', cm, eye).reshape(S5S, S5W)

    wc_re, wc_im = block_out(lp['s5_c_re']), -block_out(lp['s5_c_im'])
    wc = jnp.stack([jnp.concatenate([wc_re[j * S5_PS:(j + 1) * S5_PS, j * S5_PW:(j + 1) * S5_PW],
                                     wc_im[j * S5_PS:(j + 1) * S5_PS, j * S5_PW:(j + 1) * S5_PW]], axis=0)
                    for j in range(S5_PARTS)], axis=0).astype(BF16)
    ab = jnp.stack([ab_re.reshape(2, S5S), ab_im.reshape(2, S5S)], axis=1)
    return wb, wc, ab


def _mix_kernel(oatt_ref, osum_ref, sh_ref, ms_ref, y_ref, u_ref, seg_ref, lng_ref, lnb_ref,
                rk_ref, d_ref, wglu_ref, o_ref):
    segm = seg_ref[...]
    o = osum_ref[...]
    mean = _segsum(o, segm) * (1.0 / QK)
    oc = o - mean
    var = _segsum(oc * oc, segm) * (1.0 / QK)
    o_n = oc * lax.rsqrt(var + GN_EPS) * lng_ref[...] + lnb_ref[...]
    r = sh_ref[0]
    v = sh_ref[1]
    bonus = _segsum(r * 0.5 * ms_ref[1] * rk_ref[...], segm) * v
    rw = (o_n + bonus) * ms_ref[0]
    u = u_ref[...]
    y = d_ref[...] * u + y_ref[0] + y_ref[1]
    hg = 0.5 * y * (1.0 + jnp.tanh(math.sqrt(2.0 / math.pi) * (y + 0.044715 * (y * y * y))))
    s5 = hg * _sigmoid(_dot(hg.astype(BF16), wglu_ref[...]))
    o_ref[:, 0:ATT_W] = oatt_ref[...].astype(BF16)
    o_ref[:, ATT_W:ATT_W + RW] = rw.astype(BF16)
    o_ref[:, ATT_W + RW:D] = s5.astype(BF16)


def _mix_post(o_att, o_sum, shared, misc, y_s5, proj, segmat, lp, seg):
    B, L = seg[0], seg[1]
    tm = min(L, 256)
    nt = L // tm
    us = PROJ_STRIDE // S5W
    c2 = lambda b, t: (0, 0)
    return pl.pallas_call(
        _mix_kernel,
        grid=(B, nt),
        in_specs=[pl.BlockSpec((tm, ATT_W), lambda b, t: (b * nt + t, 0)),
                  pl.BlockSpec((tm, RW), lambda b, t: (t, b)),
                  pl.BlockSpec((3, tm, RW), lambda b, t: (0, t, b)),
                  pl.BlockSpec((2, tm, RW), lambda b, t: (0, t, b)),
                  pl.BlockSpec((2, tm, S5W), lambda b, t: (0, t, b)),
                  pl.BlockSpec((tm, S5W), lambda b, t: (t, b * us + COL_U // S5W)),
                  pl.BlockSpec((RW, RW), c2),
                  pl.BlockSpec((1, RW), c2),
                  pl.BlockSpec((1, RW), c2),
                  pl.BlockSpec((1, RW), c2),
                  pl.BlockSpec((1, S5W), c2),
                  pl.BlockSpec((S5W, S5W), c2)],
        out_specs=pl.BlockSpec((tm, D), lambda b, t: (b * nt + t, 0)),
        out_shape=jax.ShapeDtypeStruct((B * L, D), BF16),
        compiler_params=_cp(("arbitrary", "arbitrary")),
        name="mix_post",
    )(o_att, o_sum, shared, misc, y_s5.reshape(2, L, B * S5W), proj, segmat,
      lp['rwkv_ln_g'].reshape(1, RW), lp['rwkv_ln_b'].reshape(1, RW), lp['rwkv_r_k'].reshape(1, RW),
      lp['s5_d'].reshape(1, S5W), lp['s5_w_glu'].astype(BF16))


def _out_kernel(oc_ref, x_ref, w_ref, g1_ref, sc_ref, sh_ref, n2_ref, rt_ref, rb_ref,
                x1_ref, hp_ref, tw_ref, ti_ref):
    x1 = x_ref[...] + g1_ref[0] * _dot(oc_ref[...], w_ref[...])
    x1_ref[...] = x1
    y = x1 * lax.rsqrt(jnp.mean(x1 * x1, axis=-1, keepdims=True) + EPS)
    h2 = (y * n2_ref[...]) * (1.0 + sc_ref[0]) + sh_ref[0]
    hp_ref[...] = _pack_bf16_pair(h2[:, 0:D // 2], h2[:, D // 2:D])
    h_hi = h2.astype(BF16)
    h_lo = (h2 - h_hi.astype(F32)).astype(BF16)
    r_hi = rt_ref[0]
    r_lo = rt_ref[1]
    logits = _dot_nt(r_hi, h_hi) + (_dot_nt(r_hi, h_lo) + _dot_nt(r_lo, h_hi))
    scores = _sigmoid(logits)
    sel = scores + rb_ref[...]
    row = lax.broadcasted_iota(jnp.int32, sel.shape, 0)
    tws, tis = [], []
    for k in range(TOPK):
        m = jnp.max(sel, axis=0, keepdims=True)
        idx = jnp.min(jnp.where(sel == m, row, NE), axis=0, keepdims=True)
        hit = row == idx
        tws.append(jnp.sum(jnp.where(hit, scores, 0.0), axis=0, keepdims=True))
        tis.append(idx)
        sel = jnp.where(hit, -jnp.inf, sel)
    tw = jnp.concatenate(tws, axis=0)
    tw_ref[...] = tw / jnp.sum(tw, axis=0, keepdims=True) * ROUTE_SCALE
    ti_ref[...] = jnp.concatenate(tis, axis=0)


def _out_proj(o_cat, x, w_out_bf, mods6, norm2, router_t2, moe_bias, seg):
    B, L, mod_base, mod_stride = seg
    tm = min(L, 256)
    nt = L // tm

    def mrow(j):
        return lambda i: ((mod_base + (i // nt) * mod_stride) * 6 + j, 0, 0)

    c2 = lambda i: (0, 0)
    n = B * L
    return pl.pallas_call(
        _out_kernel,
        grid=(n // tm,),
        in_specs=[pl.BlockSpec((tm, D), lambda i: (i, 0)),
                  pl.BlockSpec((tm, D), lambda i: (i, 0)),
                  pl.BlockSpec((D, D), c2),
                  pl.BlockSpec((1, 1, D), mrow(2)),
                  pl.BlockSpec((1, 1, D), mrow(4)),
                  pl.BlockSpec((1, 1, D), mrow(3)),
                  pl.BlockSpec((1, D), c2),
                  pl.BlockSpec((2, NE, D), lambda i: (0, 0, 0)),
                  pl.BlockSpec((NE, 1), c2)],
        out_specs=[pl.BlockSpec((tm, D), lambda i: (i, 0)),
                   pl.BlockSpec((tm, D // 2), lambda i: (i, 0)),
                   pl.BlockSpec((TOPK, tm), lambda i: (0, i)),
                   pl.BlockSpec((TOPK, tm), lambda i: (0, i))],
        out_shape=[jax.ShapeDtypeStruct((n, D), F32),
                   jax.ShapeDtypeStruct((n, D // 2), U32),
                   jax.ShapeDtypeStruct((TOPK, n), F32),
                   jax.ShapeDtypeStruct((TOPK, n), jnp.int32)],
        compiler_params=_cp(("arbitrary",)),
        name="out_proj",
    )(o_cat, x, w_out_bf, mods6, mods6, mods6, norm2.reshape(1, D), router_t2, moe_bias.reshape(NE, 1))


DISP_TM = 128


def _dispatch_copy(hp_ref, xs_ref, sem, r, dst_row):
    return pltpu.make_async_copy(hp_ref.at[pl.ds(r, 1)], xs_ref.at[pl.ds(dst_row, 1)], sem.at[0])


def _dispatch_kernel(idx_ref, hp_ref, xs_in_ref, xs_ref, sem):
    del xs_in_ref
    for r in range(DISP_TM):
        for k in range(TOPK):
            _dispatch_copy(hp_ref, xs_ref, sem, r, idx_ref[0, 0, r * TOPK + k]).start(priority=k % 2)
    for r in range(DISP_TM):
        for k in range(TOPK):
            _dispatch_copy(hp_ref, xs_ref, sem, r, 0).wait()


def _moe_dispatch(slot_of, h2p, xs_init):
    n = slot_of.shape[0]
    tm = DISP_TM
    idx3 = slot_of.reshape(n // tm, 1, tm * TOPK)
    return pl.pallas_call(
        _dispatch_kernel,
        grid=(n // tm,),
        in_specs=[pl.BlockSpec((1, 1, tm * TOPK), lambda i: (i, 0, 0), memory_space=pltpu.SMEM),
                  pl.BlockSpec((tm, D // 2), lambda i: (i, 0)),
                  pl.BlockSpec(memory_space=pl.ANY)],
        out_specs=pl.BlockSpec(memory_space=pl.ANY),
        out_shape=jax.ShapeDtypeStruct(xs_init.shape, U32),
        scratch_shapes=[pltpu.SemaphoreType.DMA((1,))],
        input_output_aliases={2: 0},
        compiler_params=_cp(("arbitrary",)),
        name="moe_dispatch",
    )(idx3, h2p, xs_init)


def _moe_kernel(be_ref, nu_ref, x_ref, wgu_ref, wdn_ref, o_ref, wgu_bf, wdn_bf):
    i = pl.program_id(0)

    @pl.when(i < nu_ref[0])
    def _():
        @pl.when(jnp.logical_or(i == 0, be_ref[i] != be_ref[jnp.maximum(i - 1, 0)]))
        def _():
            wgu_bf[...] = wgu_ref[0, 0].astype(BF16)
            wdn_bf[...] = wdn_ref[0, 0].astype(BF16)

        for part in range(MOE_SPLIT):
            rows = slice(part * (MOE_BM // MOE_SPLIT), (part + 1) * (MOE_BM // MOE_SPLIT))
            lo, hi = _unpack_bf16_pair(x_ref[rows, :])
            h = (_dot(lo.astype(BF16), wgu_bf[0:D // 2, :])
                 + _dot(hi.astype(BF16), wgu_bf[D // 2:D, :]))
            gte = h[:, 0:EDIM]
            act = (gte * _sigmoid(gte)) * h[:, EDIM:2 * EDIM]
            y = _dot(act.astype(BF16), wdn_bf[...])
            o_ref[rows, :] = _pack_bf16_pair(y[:, 0:D // 2], y[:, D // 2:D])

    @pl.when(i >= nu_ref[0])
    def _():
        o_ref[...] = jnp.zeros(o_ref.shape, U32)


def _moe_experts(block_e, n_used, x_sorted, w_gu_all, w_dn_all, layer):
    n_slots = x_sorted.shape[0]
    nb = n_slots // MOE_BM
    grid_spec = pltpu.PrefetchScalarGridSpec(
        num_scalar_prefetch=2,
        grid=(nb,),
        in_specs=[pl.BlockSpec((MOE_BM, D // 2), lambda i, be, nu: (i, 0)),
                  pl.BlockSpec((1, 1, D, 2 * EDIM), lambda i, be, nu: (layer, be[i], 0, 0)),
                  pl.BlockSpec((1, 1, EDIM, D), lambda i, be, nu: (layer, be[i], 0, 0))],
        out_specs=pl.BlockSpec((MOE_BM, D // 2), lambda i, be, nu: (i, 0)),
        scratch_shapes=[pltpu.VMEM((D, 2 * EDIM), BF16),
                        pltpu.VMEM((EDIM, D), BF16)],
    )
    return pl.pallas_call(
        _moe_kernel,
        grid_spec=grid_spec,
        out_shape=jax.ShapeDtypeStruct((n_slots, D // 2), U32),
        compiler_params=_cp(("arbitrary",)),
        name="moe_experts",
    )(block_e, n_used, x_sorted, w_gu_all, w_dn_all)


COMB_TM = 128


def _comb_gather(idx_ref, y_hbm, ybuf, sem, slot):
    for r in range(COMB_TM):
        for k in range(TOPK):
            pltpu.make_async_copy(y_hbm.at[pl.ds(idx_ref[0, 0, r * TOPK + k], 1)],
                                  ybuf.at[slot, k, pl.ds(r, 1)], sem.at[slot]).start(priority=k % 2)


def _comb_wait(y_hbm, ybuf, sem, slot):
    for r in range(COMB_TM):
        for k in range(TOPK):
            pltpu.make_async_copy(y_hbm.at[pl.ds(0, 1)], ybuf.at[slot, k, pl.ds(r, 1)],
                                  sem.at[slot]).wait()


def _comb_kernel(idxc_ref, idxn_ref, y_hbm, hp_ref, tw_ref, x1_ref, g2_ref, wgu_ref, wdn_ref, fn_ref,
                 o_ref, ybuf, sem, *, final):
    i = pl.program_id(0)
    nb = pl.num_programs(0)
    slot = i % 2

    @pl.when(i == 0)
    def _():
        _comb_gather(idxc_ref, y_hbm, ybuf, sem, 0)

    _comb_gather(idxn_ref, y_hbm, ybuf, sem, 1 - slot)

    lo, hi = _unpack_bf16_pair(hp_ref[...])
    h = _dot(lo.astype(BF16), wgu_ref[0:D // 2, :]) + _dot(hi.astype(BF16), wgu_ref[D // 2:D, :])
    gte = h[:, 0:EDIM]
    act = (gte * _sigmoid(gte)) * h[:, EDIM:2 * EDIM]
    shared = _dot(act.astype(BF16), wdn_ref[...])

    _comb_wait(y_hbm, ybuf, sem, slot)
    rlo = jnp.zeros((COMB_TM, D // 2), F32)
    rhi = jnp.zeros((COMB_TM, D // 2), F32)
    tw = tw_ref[...]
    for k in range(TOPK):
        a, b = _unpack_bf16_pair(ybuf[slot, k])
        wk = tw[:, k:k + 1]
        rlo = rlo + wk * a
        rhi = rhi + wk * b
    g2 = g2_ref[0]
    x1 = x1_ref[...]
    out_lo = x1[:, 0:D // 2] + g2[:, 0:D // 2] * (rlo + shared[:, 0:D // 2])
    out_hi = x1[:, D // 2:D] + g2[:, D // 2:D] * (rhi + shared[:, D // 2:D])
    if final:
        ms = (jnp.sum(out_lo * out_lo, axis=-1, keepdims=True)
              + jnp.sum(out_hi * out_hi, axis=-1, keepdims=True)) * (1.0 / D)
        inv = lax.rsqrt(ms + EPS)
        fn = fn_ref[...]
        out_lo = out_lo * inv * fn[:, 0:D // 2]
        out_hi = out_hi * inv * fn[:, D // 2:D]
    o_ref[:, 0:D // 2] = out_lo
    o_ref[:, D // 2:D] = out_hi

    @pl.when(i == nb - 1)
    def _():
        _comb_wait(y_hbm, ybuf, sem, 1 - slot)


def _combine(slot_of, top_w, y_slots, h2p, x1, mods6, w_sgu_bf, w_sdn_bf, final_norm, seg, row0, final):
    B, L, mod_base, mod_stride = seg
    n = B * L
    tm = COMB_TM
    nb = n // tm
    nt = L // tm
    blk0 = row0 // tm
    idx3 = slot_of.reshape(-1, 1, tm * TOPK)
    c2 = lambda i: (0, 0)
    return pl.pallas_call(
        functools.partial(_comb_kernel, final=final),
        grid=(nb,),
        in_specs=[pl.BlockSpec((1, 1, tm * TOPK), lambda i: (blk0 + i, 0, 0), memory_space=pltpu.SMEM),
                  pl.BlockSpec((1, 1, tm * TOPK), lambda i: (blk0 + jnp.minimum(i + 1, nb - 1), 0, 0),
                               memory_space=pltpu.SMEM),
                  pl.BlockSpec(memory_space=pl.ANY),
                  pl.BlockSpec((tm, D // 2), lambda i: (blk0 + i, 0)),
                  pl.BlockSpec((tm, TOPK), lambda i: (blk0 + i, 0)),
                  pl.BlockSpec((tm, D), lambda i: (i, 0)),
                  pl.BlockSpec((1, 1, D), lambda i: ((mod_base + (i // nt) * mod_stride) * 6 + 5, 0, 0)),
                  pl.BlockSpec((D, 2 * EDIM), c2),
                  pl.BlockSpec((EDIM, D), c2),
                  pl.BlockSpec((1, D), c2)],
        out_specs=pl.BlockSpec((tm, D), lambda i: (i, 0)),
        out_shape=jax.ShapeDtypeStruct((n, D), F32),
        scratch_shapes=[pltpu.VMEM((2, TOPK, tm, D // 2), U32), pltpu.SemaphoreType.DMA((2,))],
        compiler_params=_cp(("arbitrary",)),
        name="moe_combine",
    )(idx3, idx3, y_slots, h2p, top_w, x1, mods6, w_sgu_bf, w_sdn_bf, final_norm.reshape(1, D))


def _routing_tables(top_i):
    n = top_i.shape[0]
    onehot = (top_i[:, :, None] == jnp.arange(NE, dtype=jnp.int32)[None, None, :])
    mask = jnp.any(onehot, axis=1).astype(jnp.int32)
    counts = jnp.sum(mask, axis=0)
    rank = jnp.cumsum(mask, axis=0) - mask
    padded = (counts + MOE_BM - 1) // MOE_BM * MOE_BM
    pad_end = jnp.cumsum(padded)
    pad_start = pad_end - padded
    slot_all = pad_start[None, :] + rank
    slot_of = jnp.take_along_axis(slot_all, top_i, axis=1).astype(jnp.int32)
    n_blocks = n * TOPK // MOE_BM + NE
    starts = jnp.arange(n_blocks, dtype=jnp.int32) * MOE_BM
    block_e = jnp.minimum(jnp.sum((pad_end[None, :] <= starts[:, None]).astype(jnp.int32), axis=1),
                          NE - 1).astype(jnp.int32)
    n_used = (pad_end[NE - 1:NE] // MOE_BM).astype(jnp.int32)
    return slot_of, block_e, n_used


def _rope_tables(n_tok):
    rows = n_tok // GRID_W
    row = jnp.repeat(jnp.arange(rows), GRID_W).astype(F32)
    col = jnp.tile(jnp.arange(GRID_W), rows).astype(F32)
    half = QK // 2
    inv = ROPE_THETA ** (-jnp.arange(0, half, 2, dtype=F32) / half)
    ang = jnp.concatenate([row[:, None] * inv, col[:, None] * inv], axis=-1)
    cos = jnp.repeat(jnp.cos(ang), 2, axis=-1)
    sin = jnp.repeat(jnp.sin(ang), 2, axis=-1)
    sign = jnp.tile(jnp.array([-1.0, 1.0], F32), QK // 2)
    return jnp.tile(cos, (1, 2)), jnp.tile(sin * sign, (1, 2))


def kernel(x_prompt, x_sample, cache_attn_k, cache_attn_v, state_rwkv, state_s5, c, c_ctx, w_mod, b_mod, norm1, norm2, w_in, w_out, att_lambda, att_subln, rwkv_mu, rwkv_w0, rwkv_w_up, rwkv_a0, rwkv_a_up, rwkv_g_up, rwkv_k_k, rwkv_k_a, rwkv_r_k, rwkv_ln_g, rwkv_ln_b, s5_lam_re, s5_lam_im, s5_log_step, s5_b_re, s5_b_im, s5_c_re, s5_c_im, s5_d, s5_w_glu, moe_router, moe_bias, moe_w_gate_up, moe_w_down, shared_w_gate_up, shared_w_down, final_norm):
    params = dict(w_mod=w_mod, b_mod=b_mod, norm1=norm1, norm2=norm2, w_in=w_in, w_out=w_out,
                  att_lambda=att_lambda, att_subln=att_subln,
                  rwkv_mu=rwkv_mu, rwkv_w0=rwkv_w0, rwkv_w_up=rwkv_w_up, rwkv_a0=rwkv_a0,
                  rwkv_a_up=rwkv_a_up, rwkv_g_up=rwkv_g_up, rwkv_k_k=rwkv_k_k, rwkv_k_a=rwkv_k_a,
                  rwkv_r_k=rwkv_r_k, rwkv_ln_g=rwkv_ln_g, rwkv_ln_b=rwkv_ln_b,
                  s5_lam_re=s5_lam_re, s5_lam_im=s5_lam_im, s5_log_step=s5_log_step,
                  s5_b_re=s5_b_re, s5_b_im=s5_b_im, s5_c_re=s5_c_re, s5_c_im=s5_c_im,
                  s5_d=s5_d, s5_w_glu=s5_w_glu,
                  moe_router=moe_router, moe_bias=moe_bias, moe_w_gate_up=moe_w_gate_up,
                  moe_w_down=moe_w_down, shared_w_gate_up=shared_w_gate_up,
                  shared_w_down=shared_w_down)
    bp, lp_len, _ = x_prompt.shape
    bs, ls_len, _ = x_sample.shape
    segs = ((bp, lp_len, 0, 0), (bs, ls_len, 1, 1))
    xs = [x_prompt.reshape(bp * lp_len, D), x_sample.reshape(bs * ls_len, D)]
    cpad = jnp.zeros((16, D), F32).at[0].set(c_ctx).at[1:1 + bs].set(c)
    new_k, new_v, new_r, new_s = [], [], [], []
    n_slots = ((bp * lp_len + bs * ls_len) * TOPK // MOE_BM + NE) * MOE_BM
    x_sorted = jnp.zeros((n_slots, D // 2), U32)
    for l in range(DEPTH):
        lp = {name: arr[l] for name, arr in params.items() if name not in _WHOLE}
        whole = {name: params[name] for name in _WHOLE}
        lam_init = 0.8 - 0.6 * math.exp(-0.3 * l)
        ctxs = (None, (cache_attn_k[:, l], cache_attn_v[:, l], state_rwkv[:, l], state_s5[:, l]))
        xs, caches, x_sorted = _layer(xs, segs, cpad, lp, whole, l, lam_init, ctxs, final_norm,
                                      l == DEPTH - 1, x_sorted)
        ck, cv, cr, cs = caches[0]
        new_k.append(ck)
        new_v.append(cv)
        new_r.append(cr)
        new_s.append(cs)
    return (xs[0].reshape(bp, lp_len, D), xs[1].reshape(bs, ls_len, D),
            jnp.stack(new_k, axis=1), jnp.stack(new_v, axis=1),
            jnp.stack(new_r, axis=1), jnp.stack(new_s, axis=1))


_WHOLE = ('w_mod', 'moe_w_gate_up', 'moe_w_down')


def _layer(xs, segs, cpad, lp, whole, layer, lam_init, ctxs, final_norm, final, x_sorted):
    hh = jnp.arange(RW) // QK
    segmat = (hh[:, None] == hh[None, :]).astype(BF16)
    perm = jnp.concatenate([jnp.arange(0, 3072), jnp.arange(3072 + RW_PROJ, PROJ_W),
                            jnp.arange(3072, 3072 + RW_PROJ)])
    mods6 = _modulation(cpad, whole['w_mod'], lp['b_mod'], layer).reshape(16 * 6, 1, D)
    w_in_bf = lp['w_in'][:, perm].astype(BF16)
    w_out_bf = lp['w_out'].astype(BF16)
    wb, wc, ab = _s5_params(lp)
    rt = lp['moe_router'].T
    rt_hi = rt.astype(BF16)
    router_t2 = jnp.stack([rt_hi, (rt - rt_hi.astype(F32)).astype(BF16)], axis=0)
    x1s, h2ps, tws, tis, caches = [], [], [], [], []
    for si, seg in enumerate(segs):
        B, L = seg[0], seg[1]
        proj = _in_proj(xs[si], lp['norm1'], mods6, w_in_bf, seg)
        proj3 = proj.reshape(L, B, PROJ_STRIDE)
        if ctxs[si] is None:
            o_att = _attention(proj, lp['att_lambda'], lp['att_subln'], lam_init, seg)
            s0_rwkv = None
            x0 = jnp.zeros((2, 2, B, S5S), F32)
        else:
            ck, cv, s0_rwkv, s0_s5 = ctxs[si]
            ctx = (ck.reshape(B, -1, ATT_W), cv.reshape(B, -1, ATT_W))
            o_att = _attention(proj, lp['att_lambda'], lp['att_subln'], lam_init, seg, ctx,
                               _rope_tables(L))
            x0 = s0_s5.astype(F32).reshape(B, 2, 2, S5S).transpose(1, 2, 0, 3)
        o_sum, shared, misc, s_fin = _rwkv_mix(proj, lp, segmat, seg, s0_rwkv)
        y_s5, xf = _s5_scan(proj3, wb, wc, ab, x0, seg)
        o_cat = _mix_post(o_att, o_sum, shared, misc, y_s5, proj, segmat, lp, seg)
        x1, h2p, tw, ti = _out_proj(o_cat, xs[si], w_out_bf, mods6, lp['norm2'],
                                    router_t2, lp['moe_bias'], seg)
        x1s.append(x1)
        h2ps.append(h2p)
        tws.append(tw)
        tis.append(ti)
        caches.append((proj3[:, :, COL_K:COL_K + ATT_W].transpose(1, 0, 2).reshape(B, L, HEADS, 2, QK),
                       proj3[:, :, COL_V:COL_V + ATT_W].transpose(1, 0, 2).reshape(B, L, HEADS, 2 * QK),
                       s_fin,
                       xf.transpose(2, 0, 1, 3).reshape(B, 2, 2, S5G, S5N)))
    h2p_all = jnp.concatenate(h2ps, axis=0)
    top_w = jnp.concatenate(tws, axis=1).T
    slot_of, block_e, n_used = _routing_tables(jnp.concatenate(tis, axis=1).T)
    x_sorted = _moe_dispatch(slot_of, h2p_all, x_sorted)
    y_slots = _moe_experts(block_e, n_used, x_sorted, whole['moe_w_gate_up'], whole['moe_w_down'], layer)
    w_sgu_bf = lp['shared_w_gate_up'].astype(BF16)
    w_sdn_bf = lp['shared_w_down'].astype(BF16)
    outs = []
    row0 = 0
    for si, seg in enumerate(segs):
        outs.append(_combine(slot_of, top_w, y_slots, h2p_all, x1s[si], mods6, w_sgu_bf, w_sdn_bf,
                             final_norm, seg, row0, final))
        row0 += seg[0] * seg[1]
    return outs, caches, x_sorted
```

```python
import functools
import math

import jax
import jax.numpy as jnp
from jax import lax
from jax.experimental import pallas as pl
from jax.experimental.pallas import tpu as pltpu

F32 = jnp.float32
BF16 = jnp.bfloat16
U32 = jnp.uint32

D = 2048
DEPTH = 2
EPS = 1e-6
GRID_W = 64
ROPE_THETA = 10000.0
HEADS = 8
QK = 64
ATT_W = 1024
RW = 512
RW_PROJ = 1792
S5W = 512
S5G = 32
S5N = 64
S5C = 16
S5S = S5G * S5N
S5_PARTS = 4
S5_PW = S5W // S5_PARTS
S5_PS = S5S // S5_PARTS
GN_EPS = 64e-5
NE = 64
TOPK = 8
EDIM = 512
ROUTE_SCALE = 2.5
PROJ_W = 5376
PROJ_STRIDE = 7168
COL_Q, COL_K, COL_V, COL_U, COL_Z = 0, 1024, 2048, 3072, 3584
PROJ_TN = 1792
MOE_BM = 512
MOE_SPLIT = 2
ATT_TQS = 128
ATT_TK = 256
VMEM_LIMIT = 56 * 1024 * 1024


def _cp(sem):
    return pltpu.CompilerParams(dimension_semantics=sem, vmem_limit_bytes=VMEM_LIMIT)


def _sigmoid(x):
    return 1.0 / (1.0 + jnp.exp(-x))


def _dot(a, b):
    return jnp.dot(a, b, preferred_element_type=F32)


def _dot_nt(a, b):
    return lax.dot_general(a, b, (((1,), (1,)), ((), ())), preferred_element_type=F32)


def _split_bf16(x):
    hi = x.astype(BF16)
    return hi, (x - hi.astype(F32)).astype(BF16)


def _dot_hi(a, b):
    a_hi, a_lo = _split_bf16(a)
    b_hi, b_lo = _split_bf16(b)
    return _dot(a_hi, b_hi) + (_dot(a_hi, b_lo) + _dot(a_lo, b_hi))


def _segsum(x, seg_bf):
    x_hi, x_lo = _split_bf16(x)
    return _dot(x_hi, seg_bf) + _dot(x_lo, seg_bf)


def _bf16_bits(x):
    b = lax.bitcast_convert_type(x, U32)
    return b + jnp.uint32(0x7FFF) + ((b >> 16) & jnp.uint32(1))


def _pack_bf16_pair(lo, hi):
    return (_bf16_bits(lo) >> 16) | (_bf16_bits(hi) & jnp.uint32(0xFFFF0000))


def _unpack_bf16_pair(p):
    lo = lax.bitcast_convert_type(p << 16, F32)
    hi = lax.bitcast_convert_type(p & jnp.uint32(0xFFFF0000), F32)
    return lo, hi


def _mod_kernel(c_ref, w_ref, b_ref, o_ref):
    c = c_ref[...]
    s = c * _sigmoid(c)
    o_ref[...] = _dot(s.astype(BF16), w_ref[0].astype(BF16)) + b_ref[...]


def _modulation(cpad, w_mod_all, b_mod, layer):
    tn = 1024
    return pl.pallas_call(
        _mod_kernel,
        grid=(6 * D // tn,),
        in_specs=[pl.BlockSpec((16, D), lambda j: (0, 0)),
                  pl.BlockSpec((1, D, tn), lambda j: (layer, 0, j)),
                  pl.BlockSpec((1, tn), lambda j: (0, j))],
        out_specs=pl.BlockSpec((16, tn), lambda j: (0, j)),
        out_shape=jax.ShapeDtypeStruct((16, 6 * D), F32),
        compiler_params=_cp(("arbitrary",)),
        name="modulation",
    )(cpad, w_mod_all, b_mod.reshape(1, 6 * D))


def _in_kernel(x_ref, g_ref, sc_ref, sh_ref, w_ref, o_ref):
    x = x_ref[...]
    y = x * lax.rsqrt(jnp.mean(x * x, axis=-1, keepdims=True) + EPS)
    h = (y * g_ref[...]) * (1.0 + sc_ref[0]) + sh_ref[0]
    o_ref[...] = _dot(h.astype(BF16), w_ref[...])


def _in_proj(x, norm_g, mods6, w_in_bf, seg):
    B, L, mod_base, mod_stride = seg
    tm = min(L, 512)
    nt = L // tm
    ncol = PROJ_W // PROJ_TN
    nstride = PROJ_STRIDE // PROJ_TN

    def mrow(j):
        return lambda c, i: ((mod_base + (i // nt) * mod_stride) * 6 + j, 0, 0)

    return pl.pallas_call(
        _in_kernel,
        grid=(ncol, B * nt),
        in_specs=[pl.BlockSpec((tm, D), lambda c, i: (i, 0)),
                  pl.BlockSpec((1, D), lambda c, i: (0, 0)),
                  pl.BlockSpec((1, 1, D), mrow(1)),
                  pl.BlockSpec((1, 1, D), mrow(0)),
                  pl.BlockSpec((D, PROJ_TN), lambda c, i: (0, c))],
        out_specs=pl.BlockSpec((tm, PROJ_TN), lambda c, i: (i % nt, (i // nt) * nstride + c)),
        out_shape=jax.ShapeDtypeStruct((L, B * PROJ_STRIDE), F32),
        compiler_params=_cp(("arbitrary", "arbitrary")),
        name="in_proj",
    )(x, norm_g.reshape(1, D), mods6, mods6, w_in_bf)


def _rope(x, c, s):
    lane = lax.broadcasted_iota(jnp.int32, x.shape, 1)
    nxt = pltpu.roll(x, 127, 1)
    prv = pltpu.roll(x, 1, 1)
    swapped = jnp.where((lane & 1) == 0, nxt, prv)
    return x * c + swapped * s


def _attn_kernel(*refs, lam_init, n_ctx, rope, hb):
    if rope:
        (q_ref, k_ref, v_ref, ck_ref, cv_ref, cq_ref, sq_ref, ckk_ref, skk_ref,
         lam_ref, g_ref, o_ref, kall_ref, vall_ref) = refs
    else:
        q_ref, k_ref, v_ref, lam_ref, g_ref, o_ref, kall_ref, vall_ref = refs

    @pl.when(pl.program_id(2) == 0)
    def _():
        for hh in range(hb):
            cols = slice(hh * 128, (hh + 1) * 128)
            k = k_ref[:, cols]
            if rope:
                k = _rope(k, ckk_ref[...], skk_ref[...])
                kall_ref[hh, 0:n_ctx, :] = ck_ref[0, :, cols].astype(BF16)
                vall_ref[hh, 0:n_ctx, 0:128] = cv_ref[0, :, cols].astype(BF16)
            kall_ref[hh, n_ctx:, :] = k.astype(BF16)
            vall_ref[hh, n_ctx:, 0:128] = v_ref[:, cols].astype(BF16)
            vall_ref[hh, :, 128:256] = jnp.ones((vall_ref.shape[1], 128), BF16)

    lv = lam_ref[...]
    lam = (jnp.exp(jnp.sum(lv[0:1] * lv[1:2], axis=-1, keepdims=True))
           - jnp.exp(jnp.sum(lv[2:3] * lv[3:4], axis=-1, keepdims=True)) + lam_init)
    n_kt = kall_ref.shape[1] // ATT_TK
    lane = lax.broadcasted_iota(jnp.int32, (ATT_TQS, 128), 1)
    for hh in range(hb):
        cols = slice(hh * 128, (hh + 1) * 128)
        q = q_ref[:, cols]
        if rope:
            q = _rope(q, cq_ref[...], sq_ref[...])
        q = q * (QK ** -0.5 * math.log2(math.e))
        for qs in range(q.shape[0] // ATT_TQS):
            rows = slice(qs * ATT_TQS, (qs + 1) * ATT_TQS)
            outs = []
            for m in range(2):
                qm = jnp.where((lane < QK) == (m == 0), q[rows], 0.0).astype(BF16)
                macc = _dot_nt(qm, kall_ref[hh, 0:ATT_TK, :])
                for kt in range(1, n_kt):
                    macc = jnp.maximum(macc, _dot_nt(qm, kall_ref[hh, kt * ATT_TK:(kt + 1) * ATT_TK, :]))
                mx = jnp.max(macc, axis=-1, keepdims=True)
                acc = jnp.zeros((ATT_TQS, 256), F32)
                for kt in range(n_kt):
                    keys = slice(kt * ATT_TK, (kt + 1) * ATT_TK)
                    e = jnp.exp2(_dot_nt(qm, kall_ref[hh, keys, :]) - mx).astype(BF16)
                    acc = acc + _dot(e, vall_ref[hh, keys, :])
                outs.append(acc[:, 0:128] / acc[:, 128:129])
            o = outs[0] - lam * outs[1]
            o = o * lax.rsqrt(jnp.mean(o * o, axis=-1, keepdims=True) + EPS) * g_ref[...]
            o_ref[rows, cols] = o * (1.0 - lam_init)


def _attention(proj, att_lambda, subln, lam_init, seg, ctx=None, tables=None):
    B, L = seg[0], seg[1]
    tq = min(L, 512)
    nq = L // tq
    rope = ctx is not None
    n_ctx = ctx[0].shape[1] if rope else 0
    hb = 1 if rope else HEADS
    bw = 128 * hb
    cs = PROJ_STRIDE // bw
    specs = [pl.BlockSpec((tq, bw), lambda b, h, t: (t, b * cs + COL_Q // bw + h)),
             pl.BlockSpec((L, bw), lambda b, h, t: (0, b * cs + COL_K // bw + h)),
             pl.BlockSpec((L, bw), lambda b, h, t: (0, b * cs + COL_V // bw + h))]
    args = [proj, proj, proj]
    if rope:
        cos_t, sin_t = tables
        specs += [pl.BlockSpec((1, n_ctx, bw), lambda b, h, t: (b, 0, h)),
                  pl.BlockSpec((1, n_ctx, bw), lambda b, h, t: (b, 0, h)),
                  pl.BlockSpec((tq, 128), lambda b, h, t: (t, 0)),
                  pl.BlockSpec((tq, 128), lambda b, h, t: (t, 0)),
                  pl.BlockSpec((L, 128), lambda b, h, t: (0, 0)),
                  pl.BlockSpec((L, 128), lambda b, h, t: (0, 0))]
        args += [ctx[0], ctx[1], cos_t, sin_t, cos_t, sin_t]
    specs += [pl.BlockSpec((4, QK), lambda b, h, t: (0, 0)),
              pl.BlockSpec((1, 128), lambda b, h, t: (0, 0))]
    args += [att_lambda, subln.reshape(1, 128)]
    return pl.pallas_call(
        functools.partial(_attn_kernel, lam_init=lam_init, n_ctx=n_ctx, rope=rope, hb=hb),
        grid=(B, HEADS // hb, nq),
        in_specs=specs,
        out_specs=pl.BlockSpec((tq, bw), lambda b, h, t: (b * nq + t, h)),
        out_shape=jax.ShapeDtypeStruct((B * L, ATT_W), F32),
        scratch_shapes=[pltpu.VMEM((hb, n_ctx + L, 128), BF16), pltpu.VMEM((hb, n_ctx + L, 256), BF16)],
        compiler_params=_cp(("arbitrary", "arbitrary", "arbitrary")),
        name="diff_attention",
    )(*args)


def _rwkv_pre_kernel(z_ref, zp_ref, zn_ref, mu_ref, seg_ref, wup_ref, aup_ref, gup_ref,
                     w0_ref, a0_ref, kk_ref, ka_ref, pd_ref, ms_ref, *, tm):
    t = pl.program_id(1)
    nt = pl.num_programs(1)
    z = z_ref[...]
    row = lax.broadcasted_iota(jnp.int32, z.shape, 0)
    prev_row = jnp.where(t > 0, zp_ref[7:8, :], 0.0)
    next_row = jnp.where(t < nt - 1, zn_ref[0:1, :], 0.0)
    zp = jnp.where(row == 0, prev_row, pltpu.roll(z, 1, 0))
    zn = jnp.where(row == tm - 1, next_row, pltpu.roll(z, tm - 1, 0))
    zs = z + mu_ref[...] * (0.5 * (zp + zn) - z)
    r = zs[:, 0:RW]
    k = zs[:, RW:2 * RW]
    v = zs[:, 2 * RW:3 * RW]
    wa = zs[:, 3 * RW:3 * RW + 128]
    gl = zs[:, 3 * RW + 128:3 * RW + 256]
    lane = lax.broadcasted_iota(jnp.int32, wa.shape, 1)
    wa = jnp.where(lane < 64, jnp.tanh(wa), wa)
    g = _dot_hi(_sigmoid(gl), gup_ref[...])
    kk = k * kk_ref[...]
    kk = kk * lax.rsqrt(_segsum(kk * kk, seg_ref[...]) + EPS)
    kb = jnp.zeros_like(k)
    for d in range(2):
        xw = w0_ref[d] + _dot_hi(wa, wup_ref[d])
        w = jnp.exp(-math.exp(-0.5) * _sigmoid(xw))
        a = _sigmoid(a0_ref[d] + _dot_hi(wa, aup_ref[d]))
        kd = k * (1.0 + (a - 1.0) * ka_ref[...])
        pd_ref[0, d] = r
        pd_ref[1, d] = v
        pd_ref[2, d] = kk
        pd_ref[3, d] = w
        pd_ref[4, d] = kd
        pd_ref[5, d] = kk * a
        kb = kb + kd
    ms_ref[0] = g
    ms_ref[1] = kb


def _rwkv_pre(proj, lp, segmat, seg):
    B, L = seg[0], seg[1]
    tm = min(L, 256)
    nt = L // tm
    zs = PROJ_STRIDE // RW_PROJ
    zc = COL_Z // RW_PROJ
    nb8 = L // 8
    zeros64 = jnp.zeros((2, 64, RW), F32)
    wup = jnp.concatenate([lp['rwkv_w_up'], zeros64], axis=1)
    aup = jnp.concatenate([zeros64, lp['rwkv_a_up']], axis=1)
    c2 = lambda b, t: (0, 0)
    c3 = lambda b, t: (0, 0, 0)
    return pl.pallas_call(
        functools.partial(_rwkv_pre_kernel, tm=tm),
        grid=(B, nt),
        in_specs=[pl.BlockSpec((tm, RW_PROJ), lambda b, t: (t, b * zs + zc)),
                  pl.BlockSpec((8, RW_PROJ), lambda b, t: (jnp.maximum(t * (tm // 8) - 1, 0), b * zs + zc)),
                  pl.BlockSpec((8, RW_PROJ), lambda b, t: (jnp.minimum((t + 1) * (tm // 8), nb8 - 1), b * zs + zc)),
                  pl.BlockSpec((1, RW_PROJ), c2),
                  pl.BlockSpec((RW, RW), c2),
                  pl.BlockSpec((2, 128, RW), c3),
                  pl.BlockSpec((2, 128, RW), c3),
                  pl.BlockSpec((128, RW), c2),
                  pl.BlockSpec((2, 1, RW), c3),
                  pl.BlockSpec((2, 1, RW), c3),
                  pl.BlockSpec((1, RW), c2),
                  pl.BlockSpec((1, RW), c2)],
        out_specs=[
                   pl.BlockSpec((6, 2, tm, RW), lambda b, t: (0, 0, t, b)),
                   pl.BlockSpec((2, tm, RW), lambda b, t: (0, t, b))],
        out_shape=[
                   jax.ShapeDtypeStruct((6, 2, L, B * RW), F32),
                   jax.ShapeDtypeStruct((2, L, B * RW), F32)],
        compiler_params=_cp(("arbitrary", "arbitrary")),
        name="rwkv_pre",
    )(proj, proj, proj, lp['rwkv_mu'].reshape(1, RW_PROJ), segmat, wup, aup, lp['rwkv_g_up'],
      lp['rwkv_w0'].reshape(2, 1, RW), lp['rwkv_a0'].reshape(2, 1, RW),
      lp['rwkv_k_k'].reshape(1, RW), lp['rwkv_k_a'].reshape(1, RW))


def _wkv_kernel(xa_ref, xb_ref, s0_ref, yf_ref, yr_ref, sf_ref, st_ref, *, tc):
    c = pl.program_id(1)

    @pl.when(c == 0)
    def _():
        st_ref[...] = s0_ref[...]

    fwd = lax.broadcasted_iota(jnp.int32, (QK, 128), 1) < 64

    def step(t, carry):
        tr = tc - 1 - t

        def tile(q):
            return jnp.where(fwd, xa_ref[q, t, 0], xb_ref[q, tr, 0])

        r_t = tile(0)
        v_t = tile(1)
        kk_t = tile(2)
        w_t = tile(3)
        kd_t = tile(4)
        b_t = tile(5)
        wr = w_t * r_t
        kr = jnp.sum(kd_t * r_t, axis=0, keepdims=True)
        br = jnp.sum(b_t * r_t, axis=0, keepdims=True)
        ytiles = []
        for g in range(QK // 8):
            ys = []
            for j in range(8):
                vi = g * 8 + j
                s = st_ref[vi]
                sa = jnp.sum(s * kk_t, axis=0, keepdims=True)
                y0 = jnp.sum(s * wr, axis=0, keepdims=True)
                vv = v_t[vi:vi + 1]
                st_ref[vi] = s * w_t + (vv * kd_t - sa * b_t)
                ys.append(y0 + vv * kr - sa * br)
            ytiles.append(jnp.concatenate(ys, axis=0))
        y_t = jnp.transpose(jnp.concatenate(ytiles, axis=0))
        yf_ref[t, 0] = y_t[0:64]
        yr_ref[tr, 0] = y_t[64:128]
        return carry

    lax.fori_loop(0, tc, step, 0)

    @pl.when(c == pl.num_programs(1) - 1)
    def _():
        sf_ref[...] = st_ref[...]


def _wkv_scan(xs, s0):
    _, L, G, _, _ = xs.shape
    lanes = G * 128
    tc = 16
    nc = L // tc
    return pl.pallas_call(
        functools.partial(_wkv_kernel, tc=tc),
        grid=(G, nc),
        in_specs=[pl.BlockSpec((6, tc, 1, QK, 128), lambda g, c: (0, c, g, 0, 0)),
                  pl.BlockSpec((6, tc, 1, QK, 128), lambda g, c: (0, nc - 1 - c, g, 0, 0)),
                  pl.BlockSpec((QK, QK, 128), lambda g, c: (0, 0, g))],
        out_specs=[pl.BlockSpec((tc, 1, 64, QK), lambda g, c: (c, g, 0, 0)),
                   pl.BlockSpec((tc, 1, 64, QK), lambda g, c: (nc - 1 - c, g, 0, 0)),
                   pl.BlockSpec((QK, QK, 128), lambda g, c: (0, 0, g))],
        out_shape=[jax.ShapeDtypeStruct((L, lanes // 128, 64, QK), F32),
                   jax.ShapeDtypeStruct((L, lanes // 128, 64, QK), F32),
                   jax.ShapeDtypeStruct((QK, QK, lanes), F32)],
        scratch_shapes=[pltpu.VMEM((QK, QK, 128), F32)],
        compiler_params=_cp(("arbitrary", "arbitrary")),
        name="wkv7_scan",
    )(xs, xs, s0)


def _rwkv_mix(proj, lp, segmat, seg, s0_bdhvk):
    B, L = seg[0], seg[1]
    G = B // 8
    perdir, misc = _rwkv_pre(proj, lp, segmat, seg)
    pd = jnp.moveaxis(perdir.reshape(6, 2, L, G, 8 * HEADS, QK), 1, 3)
    xs = jnp.swapaxes(pd.reshape(6, L, G, 128, QK), -1, -2)
    if s0_bdhvk is None:
        s0 = jnp.zeros((QK, QK, G * 128), F32)
    else:
        s0 = (s0_bdhvk.astype(F32).reshape(G, 8, 2, HEADS, QK, QK)
              .transpose(4, 5, 0, 2, 1, 3).reshape(QK, QK, G * 128))
    yf, yr, sf = _wkv_scan(xs, s0)
    o_sum = (yf + yr).reshape(L, B * RW)
    s_fin = (sf.reshape(QK, QK, G, 2, 8, HEADS).transpose(2, 4, 3, 5, 0, 1)
             .reshape(B, 2, HEADS, QK, QK))
    return o_sum, perdir, misc, s_fin


def _s5_kernel(u_ref, wb_ref, wc_ref, ab_ref, x0_ref, y_ref, xf_ref, bx_ref, st_ref, *, tc, nb):
    d = pl.program_id(0)
    c = pl.program_id(1)

    @pl.when(c == 0)
    def _():
        st_ref[...] = x0_ref[0]

    u = u_ref[...].reshape(tc * nb, S5W).astype(BF16)
    for j in range(S5_PARTS):
        bj = _dot(u[:, j * S5_PW:(j + 1) * S5_PW], wb_ref[0, j])
        bx_ref[:, j * S5_PS:(j + 1) * S5_PS] = bj[:, 0:S5_PS]
        bx_ref[:, S5S + j * S5_PS:S5S + (j + 1) * S5_PS] = bj[:, S5_PS:2 * S5_PS]
    ar = jnp.broadcast_to(ab_ref[0, 0:1, :], (nb, S5S))
    ai = jnp.broadcast_to(ab_ref[0, 1:2, :], (nb, S5S))

    def step(i, carry):
        tt = jnp.where(d == 0, i, tc - 1 - i)
        rows = pl.ds(pl.multiple_of(tt * nb, nb), nb)
        xr = st_ref[0]
        xi = st_ref[1]
        nr = ar * xr - ai * xi + bx_ref[rows, 0:S5S]
        ni = ar * xi + ai * xr + bx_ref[rows, S5S:2 * S5S]
        st_ref[0] = nr
        st_ref[1] = ni
        bx_ref[rows, 0:S5S] = nr
        bx_ref[rows, S5S:2 * S5S] = ni
        return carry

    lax.fori_loop(0, tc, step, 0)
    ys = []
    for j in range(S5_PARTS):
        xr = bx_ref[:, j * S5_PS:(j + 1) * S5_PS].astype(BF16)
        xi = bx_ref[:, S5S + j * S5_PS:S5S + (j + 1) * S5_PS].astype(BF16)
        ys.append(_dot(xr, wc_ref[j, 0:S5_PS, :]) + _dot(xi, wc_ref[j, S5_PS:2 * S5_PS, :]))
    y_ref[0] = jnp.concatenate(ys, axis=-1).reshape(tc, nb, S5W)

    @pl.when(c == pl.num_programs(1) - 1)
    def _():
        xf_ref[0] = st_ref[...]


def _s5_scan(proj3, wb, wc, ab, x0, seg):
    B, L = seg[0], seg[1]
    tc = 64 if B <= 8 else 16
    nc = L // tc
    tmap = lambda d, c: jnp.where(d == 0, c, nc - 1 - c)
    return pl.pallas_call(
        functools.partial(_s5_kernel, tc=tc, nb=B),
        grid=(2, nc),
        in_specs=[pl.BlockSpec((tc, B, S5W), lambda d, c: (tmap(d, c), 0, COL_U // S5W)),
                  pl.BlockSpec((1, S5_PARTS, S5_PW, 2 * S5_PS), lambda d, c: (d, 0, 0, 0)),
                  pl.BlockSpec((S5_PARTS, 2 * S5_PS, S5_PW), lambda d, c: (0, 0, 0)),
                  pl.BlockSpec((1, 2, S5S), lambda d, c: (d, 0, 0)),
                  pl.BlockSpec((1, 2, B, S5S), lambda d, c: (d, 0, 0, 0))],
        out_specs=[pl.BlockSpec((1, tc, B, S5W), lambda d, c: (d, tmap(d, c), 0, 0)),
                   pl.BlockSpec((1, 2, B, S5S), lambda d, c: (d, 0, 0, 0))],
        out_shape=[jax.ShapeDtypeStruct((2, L, B, S5W), F32),
                   jax.ShapeDtypeStruct((2, 2, B, S5S), F32)],
        scratch_shapes=[pltpu.VMEM((tc * B, 2 * S5S), F32), pltpu.VMEM((2, B, S5S), F32)],
        compiler_params=_cp(("arbitrary", "arbitrary")),
        name="s5_scan",
    )(proj3, wb, wc, ab, x0)


def _s5_params(lp):
    lam_re, lam_im = lp['s5_lam_re'], lp['s5_lam_im']
    dt = jnp.exp(lp['s5_log_step'])[:, :, None]
    mag = jnp.exp(lam_re * dt)
    ab_re, ab_im = mag * jnp.cos(lam_im * dt), mag * jnp.sin(lam_im * dt)
    den = lam_re * lam_re + lam_im * lam_im
    f_re = ((ab_re - 1.0) * lam_re + ab_im * lam_im) / den
    f_im = (ab_im * lam_re - (ab_re - 1.0) * lam_im) / den
    b_re, b_im = lp['s5_b_re'], lp['s5_b_im']
    wre = f_re[..., None] * b_re[None] - f_im[..., None] * b_im[None]
    wim = f_re[..., None] * b_im[None] + f_im[..., None] * b_re[None]
    eye = jnp.eye(S5G, dtype=F32)

    def block_in(w):
        return jnp.einsum('dgnc,gh->dgchn', w, eye).reshape(2, S5W, S5S)

    wb_re, wb_im = block_in(wre), block_in(wim)
    wb = jnp.stack([jnp.concatenate([wb_re[:, j * S5_PW:(j + 1) * S5_PW, j * S5_PS:(j + 1) * S5_PS],
                                     wb_im[:, j * S5_PW:(j + 1) * S5_PW, j * S5_PS:(j + 1) * S5_PS]], axis=-1)
                    for j in range(S5_PARTS)], axis=1).astype(BF16)

    def block_out(cm):
        return jnp.einsum('gcn,gh->gnhc', cm, eye).reshape(S5S, S5W)

    wc_re, wc_im = block_out(lp['s5_c_re']), -block_out(lp['s5_c_im'])
    wc = jnp.stack([jnp.concatenate([wc_re[j * S5_PS:(j + 1) * S5_PS, j * S5_PW:(j + 1) * S5_PW],
                                     wc_im[j * S5_PS:(j + 1) * S5_PS, j * S5_PW:(j + 1) * S5_PW]], axis=0)
                    for j in range(S5_PARTS)], axis=0).astype(BF16)
    ab = jnp.stack([ab_re.reshape(2, S5S), ab_im.reshape(2, S5S)], axis=1)
    return wb, wc, ab


def _mix_kernel(oatt_ref, osum_ref, sh_ref, ms_ref, y_ref, u_ref, seg_ref, lng_ref, lnb_ref,
                rk_ref, d_ref, wglu_ref, o_ref):
    segm = seg_ref[...]
    o = osum_ref[...]
    mean = _segsum(o, segm) * (1.0 / QK)
    oc = o - mean
    var = _segsum(oc * oc, segm) * (1.0 / QK)
    o_n = oc * lax.rsqrt(var + GN_EPS) * lng_ref[...] + lnb_ref[...]
    r = sh_ref[0, 0]
    v = sh_ref[1, 0]
    bonus = _segsum(r * 0.5 * ms_ref[1] * rk_ref[...], segm) * v
    rw = (o_n + bonus) * ms_ref[0]
    u = u_ref[...]
    y = d_ref[...] * u + y_ref[0] + y_ref[1]
    hg = 0.5 * y * (1.0 + jnp.tanh(math.sqrt(2.0 / math.pi) * (y + 0.044715 * (y * y * y))))
    s5 = hg * _sigmoid(_dot(hg.astype(BF16), wglu_ref[...]))
    o_ref[:, 0:ATT_W] = oatt_ref[...].astype(BF16)
    o_ref[:, ATT_W:ATT_W + RW] = rw.astype(BF16)
    o_ref[:, ATT_W + RW:D] = s5.astype(BF16)


def _mix_post(o_att, o_sum, shared, misc, y_s5, proj, segmat, lp, seg):
    B, L = seg[0], seg[1]
    tm = min(L, 256)
    nt = L // tm
    us = PROJ_STRIDE // S5W
    c2 = lambda b, t: (0, 0)
    return pl.pallas_call(
        _mix_kernel,
        grid=(B, nt),
        in_specs=[pl.BlockSpec((tm, ATT_W), lambda b, t: (b * nt + t, 0)),
                  pl.BlockSpec((tm, RW), lambda b, t: (t, b)),
                  pl.BlockSpec((2, 1, tm, RW), lambda b, t: (0, 0, t, b)),
                  pl.BlockSpec((2, tm, RW), lambda b, t: (0, t, b)),
                  pl.BlockSpec((2, tm, S5W), lambda b, t: (0, t, b)),
                  pl.BlockSpec((tm, S5W), lambda b, t: (t, b * us + COL_U // S5W)),
                  pl.BlockSpec((RW, RW), c2),
                  pl.BlockSpec((1, RW), c2),
                  pl.BlockSpec((1, RW), c2),
                  pl.BlockSpec((1, RW), c2),
                  pl.BlockSpec((1, S5W), c2),
                  pl.BlockSpec((S5W, S5W), c2)],
        out_specs=pl.BlockSpec((tm, D), lambda b, t: (b * nt + t, 0)),
        out_shape=jax.ShapeDtypeStruct((B * L, D), BF16),
        compiler_params=_cp(("arbitrary", "arbitrary")),
        name="mix_post",
    )(o_att, o_sum, shared, misc, y_s5.reshape(2, L, B * S5W), proj, segmat,
      lp['rwkv_ln_g'].reshape(1, RW), lp['rwkv_ln_b'].reshape(1, RW), lp['rwkv_r_k'].reshape(1, RW),
      lp['s5_d'].reshape(1, S5W), lp['s5_w_glu'].astype(BF16))


def _out_kernel(oc_ref, x_ref, w_ref, g1_ref, sc_ref, sh_ref, n2_ref, rt_ref, rb_ref,
                x1_ref, hp_ref, tw_ref, ti_ref):
    x1 = x_ref[...] + g1_ref[0] * _dot(oc_ref[...], w_ref[...])
    x1_ref[...] = x1
    y = x1 * lax.rsqrt(jnp.mean(x1 * x1, axis=-1, keepdims=True) + EPS)
    h2 = (y * n2_ref[...]) * (1.0 + sc_ref[0]) + sh_ref[0]
    hp_ref[...] = _pack_bf16_pair(h2[:, 0:D // 2], h2[:, D // 2:D])
    h_hi = h2.astype(BF16)
    h_lo = (h2 - h_hi.astype(F32)).astype(BF16)
    r_hi = rt_ref[0]
    r_lo = rt_ref[1]
    logits = _dot_nt(r_hi, h_hi) + (_dot_nt(r_hi, h_lo) + _dot_nt(r_lo, h_hi))
    scores = _sigmoid(logits)
    sel = scores + rb_ref[...]
    row = lax.broadcasted_iota(jnp.int32, sel.shape, 0)
    tws, tis = [], []
    for k in range(TOPK):
        m = jnp.max(sel, axis=0, keepdims=True)
        idx = jnp.min(jnp.where(sel == m, row, NE), axis=0, keepdims=True)
        hit = row == idx
        tws.append(jnp.sum(jnp.where(hit, scores, 0.0), axis=0, keepdims=True))
        tis.append(idx)
        sel = jnp.where(hit, -jnp.inf, sel)
    tw = jnp.concatenate(tws, axis=0)
    tw_ref[...] = tw / jnp.sum(tw, axis=0, keepdims=True) * ROUTE_SCALE
    ti_ref[...] = jnp.concatenate(tis, axis=0)


def _out_proj(o_cat, x, w_out_bf, mods6, norm2, router_t2, moe_bias, seg):
    B, L, mod_base, mod_stride = seg
    tm = min(L, 256)
    nt = L // tm

    def mrow(j):
        return lambda i: ((mod_base + (i // nt) * mod_stride) * 6 + j, 0, 0)

    c2 = lambda i: (0, 0)
    n = B * L
    return pl.pallas_call(
        _out_kernel,
        grid=(n // tm,),
        in_specs=[pl.BlockSpec((tm, D), lambda i: (i, 0)),
                  pl.BlockSpec((tm, D), lambda i: (i, 0)),
                  pl.BlockSpec((D, D), c2),
                  pl.BlockSpec((1, 1, D), mrow(2)),
                  pl.BlockSpec((1, 1, D), mrow(4)),
                  pl.BlockSpec((1, 1, D), mrow(3)),
                  pl.BlockSpec((1, D), c2),
                  pl.BlockSpec((2, NE, D), lambda i: (0, 0, 0)),
                  pl.BlockSpec((NE, 1), c2)],
        out_specs=[pl.BlockSpec((tm, D), lambda i: (i, 0)),
                   pl.BlockSpec((tm, D // 2), lambda i: (i, 0)),
                   pl.BlockSpec((TOPK, tm), lambda i: (0, i)),
                   pl.BlockSpec((TOPK, tm), lambda i: (0, i))],
        out_shape=[jax.ShapeDtypeStruct((n, D), F32),
                   jax.ShapeDtypeStruct((n, D // 2), U32),
                   jax.ShapeDtypeStruct((TOPK, n), F32),
                   jax.ShapeDtypeStruct((TOPK, n), jnp.int32)],
        compiler_params=_cp(("arbitrary",)),
        name="out_proj",
    )(o_cat, x, w_out_bf, mods6, mods6, mods6, norm2.reshape(1, D), router_t2, moe_bias.reshape(NE, 1))


DISP_TM = 128


def _dispatch_copy(hp_ref, xs_ref, sem, r, dst_row):
    return pltpu.make_async_copy(hp_ref.at[pl.ds(r, 1)], xs_ref.at[pl.ds(dst_row, 1)], sem.at[0])


def _dispatch_kernel(idx_ref, hp_ref, xs_in_ref, xs_ref, sem):
    del xs_in_ref
    for r in range(DISP_TM):
        for k in range(TOPK):
            _dispatch_copy(hp_ref, xs_ref, sem, r, idx_ref[0, 0, r * TOPK + k]).start(priority=k % 2)
    for r in range(DISP_TM):
        for k in range(TOPK):
            _dispatch_copy(hp_ref, xs_ref, sem, r, 0).wait()


def _moe_dispatch(slot_of, h2p, xs_init):
    n = slot_of.shape[0]
    tm = DISP_TM
    idx3 = slot_of.reshape(n // tm, 1, tm * TOPK)
    return pl.pallas_call(
        _dispatch_kernel,
        grid=(n // tm,),
        in_specs=[pl.BlockSpec((1, 1, tm * TOPK), lambda i: (i, 0, 0), memory_space=pltpu.SMEM),
                  pl.BlockSpec((tm, D // 2), lambda i: (i, 0)),
                  pl.BlockSpec(memory_space=pl.ANY)],
        out_specs=pl.BlockSpec(memory_space=pl.ANY),
        out_shape=jax.ShapeDtypeStruct(xs_init.shape, U32),
        scratch_shapes=[pltpu.SemaphoreType.DMA((1,))],
        input_output_aliases={2: 0},
        compiler_params=_cp(("arbitrary",)),
        name="moe_dispatch",
    )(idx3, h2p, xs_init)


def _moe_kernel(be_ref, nu_ref, x_ref, wgu_ref, wdn_ref, o_ref, wgu_bf, wdn_bf):
    i = pl.program_id(0)

    @pl.when(i < nu_ref[0])
    def _():
        @pl.when(jnp.logical_or(i == 0, be_ref[i] != be_ref[jnp.maximum(i - 1, 0)]))
        def _():
            wgu_bf[...] = wgu_ref[0, 0].astype(BF16)
            wdn_bf[...] = wdn_ref[0, 0].astype(BF16)

        for part in range(MOE_SPLIT):
            rows = slice(part * (MOE_BM // MOE_SPLIT), (part + 1) * (MOE_BM // MOE_SPLIT))
            lo, hi = _unpack_bf16_pair(x_ref[rows, :])
            h = (_dot(lo.astype(BF16), wgu_bf[0:D // 2, :])
                 + _dot(hi.astype(BF16), wgu_bf[D // 2:D, :]))
            gte = h[:, 0:EDIM]
            act = (gte * _sigmoid(gte)) * h[:, EDIM:2 * EDIM]
            y = _dot(act.astype(BF16), wdn_bf[...])
            o_ref[rows, :] = _pack_bf16_pair(y[:, 0:D // 2], y[:, D // 2:D])

    @pl.when(i >= nu_ref[0])
    def _():
        o_ref[...] = jnp.zeros(o_ref.shape, U32)


def _moe_experts(block_e, n_used, x_sorted, w_gu_all, w_dn_all, layer):
    n_slots = x_sorted.shape[0]
    nb = n_slots // MOE_BM
    grid_spec = pltpu.PrefetchScalarGridSpec(
        num_scalar_prefetch=2,
        grid=(nb,),
        in_specs=[pl.BlockSpec((MOE_BM, D // 2), lambda i, be, nu: (i, 0)),
                  pl.BlockSpec((1, 1, D, 2 * EDIM), lambda i, be, nu: (layer, be[i], 0, 0)),
                  pl.BlockSpec((1, 1, EDIM, D), lambda i, be, nu: (layer, be[i], 0, 0))],
        out_specs=pl.BlockSpec((MOE_BM, D // 2), lambda i, be, nu: (i, 0)),
        scratch_shapes=[pltpu.VMEM((D, 2 * EDIM), BF16),
                        pltpu.VMEM((EDIM, D), BF16)],
    )
    return pl.pallas_call(
        _moe_kernel,
        grid_spec=grid_spec,
        out_shape=jax.ShapeDtypeStruct((n_slots, D // 2), U32),
        compiler_params=_cp(("arbitrary",)),
        name="moe_experts",
    )(block_e, n_used, x_sorted, w_gu_all, w_dn_all)


COMB_TM = 128


def _comb_gather(idx_ref, y_hbm, ybuf, sem, slot):
    for r in range(COMB_TM):
        for k in range(TOPK):
            pltpu.make_async_copy(y_hbm.at[pl.ds(idx_ref[0, 0, r * TOPK + k], 1)],
                                  ybuf.at[slot, k, pl.ds(r, 1)], sem.at[slot]).start(priority=k % 2)


def _comb_wait(y_hbm, ybuf, sem, slot):
    for r in range(COMB_TM):
        for k in range(TOPK):
            pltpu.make_async_copy(y_hbm.at[pl.ds(0, 1)], ybuf.at[slot, k, pl.ds(r, 1)],
                                  sem.at[slot]).wait()


def _comb_kernel(idxc_ref, idxn_ref, y_hbm, hp_ref, tw_ref, x1_ref, g2_ref, wgu_ref, wdn_ref, fn_ref,
                 o_ref, ybuf, sem, *, final):
    i = pl.program_id(0)
    nb = pl.num_programs(0)
    slot = i % 2

    @pl.when(i == 0)
    def _():
        _comb_gather(idxc_ref, y_hbm, ybuf, sem, 0)

    _comb_gather(idxn_ref, y_hbm, ybuf, sem, 1 - slot)

    lo, hi = _unpack_bf16_pair(hp_ref[...])
    h = _dot(lo.astype(BF16), wgu_ref[0:D // 2, :]) + _dot(hi.astype(BF16), wgu_ref[D // 2:D, :])
    gte = h[:, 0:EDIM]
    act = (gte * _sigmoid(gte)) * h[:, EDIM:2 * EDIM]
    shared = _dot(act.astype(BF16), wdn_ref[...])

    _comb_wait(y_hbm, ybuf, sem, slot)
    rlo = jnp.zeros((COMB_TM, D // 2), F32)
    rhi = jnp.zeros((COMB_TM, D // 2), F32)
    tw = tw_ref[...]
    for k in range(TOPK):
        a, b = _unpack_bf16_pair(ybuf[slot, k])
        wk = tw[:, k:k + 1]
        rlo = rlo + wk * a
        rhi = rhi + wk * b
    g2 = g2_ref[0]
    x1 = x1_ref[...]
    out_lo = x1[:, 0:D // 2] + g2[:, 0:D // 2] * (rlo + shared[:, 0:D // 2])
    out_hi = x1[:, D // 2:D] + g2[:, D // 2:D] * (rhi + shared[:, D // 2:D])
    if final:
        ms = (jnp.sum(out_lo * out_lo, axis=-1, keepdims=True)
              + jnp.sum(out_hi * out_hi, axis=-1, keepdims=True)) * (1.0 / D)
        inv = lax.rsqrt(ms + EPS)
        fn = fn_ref[...]
        out_lo = out_lo * inv * fn[:, 0:D // 2]
        out_hi = out_hi * inv * fn[:, D // 2:D]
    o_ref[:, 0:D // 2] = out_lo
    o_ref[:, D // 2:D] = out_hi

    @pl.when(i == nb - 1)
    def _():
        _comb_wait(y_hbm, ybuf, sem, 1 - slot)


def _combine(slot_of, top_w, y_slots, h2p, x1, mods6, w_sgu_bf, w_sdn_bf, final_norm, seg, row0, final):
    B, L, mod_base, mod_stride = seg
    n = B * L
    tm = COMB_TM
    nb = n // tm
    nt = L // tm
    blk0 = row0 // tm
    idx3 = slot_of.reshape(-1, 1, tm * TOPK)
    c2 = lambda i: (0, 0)
    return pl.pallas_call(
        functools.partial(_comb_kernel, final=final),
        grid=(nb,),
        in_specs=[pl.BlockSpec((1, 1, tm * TOPK), lambda i: (blk0 + i, 0, 0), memory_space=pltpu.SMEM),
                  pl.BlockSpec((1, 1, tm * TOPK), lambda i: (blk0 + jnp.minimum(i + 1, nb - 1), 0, 0),
                               memory_space=pltpu.SMEM),
                  pl.BlockSpec(memory_space=pl.ANY),
                  pl.BlockSpec((tm, D // 2), lambda i: (blk0 + i, 0)),
                  pl.BlockSpec((tm, TOPK), lambda i: (blk0 + i, 0)),
                  pl.BlockSpec((tm, D), lambda i: (i, 0)),
                  pl.BlockSpec((1, 1, D), lambda i: ((mod_base + (i // nt) * mod_stride) * 6 + 5, 0, 0)),
                  pl.BlockSpec((D, 2 * EDIM), c2),
                  pl.BlockSpec((EDIM, D), c2),
                  pl.BlockSpec((1, D), c2)],
        out_specs=pl.BlockSpec((tm, D), lambda i: (i, 0)),
        out_shape=jax.ShapeDtypeStruct((n, D), F32),
        scratch_shapes=[pltpu.VMEM((2, TOPK, tm, D // 2), U32), pltpu.SemaphoreType.DMA((2,))],
        compiler_params=_cp(("arbitrary",)),
        name="moe_combine",
    )(idx3, idx3, y_slots, h2p, top_w, x1, mods6, w_sgu_bf, w_sdn_bf, final_norm.reshape(1, D))


def _routing_tables(top_i):
    n = top_i.shape[0]
    onehot = (top_i[:, :, None] == jnp.arange(NE, dtype=jnp.int32)[None, None, :])
    mask = jnp.any(onehot, axis=1).astype(jnp.int32)
    counts = jnp.sum(mask, axis=0)
    rank = jnp.cumsum(mask, axis=0) - mask
    padded = (counts + MOE_BM - 1) // MOE_BM * MOE_BM
    pad_end = jnp.cumsum(padded)
    pad_start = pad_end - padded
    slot_all = pad_start[None, :] + rank
    slot_of = jnp.take_along_axis(slot_all, top_i, axis=1).astype(jnp.int32)
    n_blocks = n * TOPK // MOE_BM + NE
    starts = jnp.arange(n_blocks, dtype=jnp.int32) * MOE_BM
    block_e = jnp.minimum(jnp.sum((pad_end[None, :] <= starts[:, None]).astype(jnp.int32), axis=1),
                          NE - 1).astype(jnp.int32)
    n_used = (pad_end[NE - 1:NE] // MOE_BM).astype(jnp.int32)
    return slot_of, block_e, n_used


def _rope_tables(n_tok):
    rows = n_tok // GRID_W
    row = jnp.repeat(jnp.arange(rows), GRID_W).astype(F32)
    col = jnp.tile(jnp.arange(GRID_W), rows).astype(F32)
    half = QK // 2
    inv = ROPE_THETA ** (-jnp.arange(0, half, 2, dtype=F32) / half)
    ang = jnp.concatenate([row[:, None] * inv, col[:, None] * inv], axis=-1)
    cos = jnp.repeat(jnp.cos(ang), 2, axis=-1)
    sin = jnp.repeat(jnp.sin(ang), 2, axis=-1)
    sign = jnp.tile(jnp.array([-1.0, 1.0], F32), QK // 2)
    return jnp.tile(cos, (1, 2)), jnp.tile(sin * sign, (1, 2))


def kernel(x_prompt, x_sample, cache_attn_k, cache_attn_v, state_rwkv, state_s5, c, c_ctx, w_mod, b_mod, norm1, norm2, w_in, w_out, att_lambda, att_subln, rwkv_mu, rwkv_w0, rwkv_w_up, rwkv_a0, rwkv_a_up, rwkv_g_up, rwkv_k_k, rwkv_k_a, rwkv_r_k, rwkv_ln_g, rwkv_ln_b, s5_lam_re, s5_lam_im, s5_log_step, s5_b_re, s5_b_im, s5_c_re, s5_c_im, s5_d, s5_w_glu, moe_router, moe_bias, moe_w_gate_up, moe_w_down, shared_w_gate_up, shared_w_down, final_norm):
    params = dict(w_mod=w_mod, b_mod=b_mod, norm1=norm1, norm2=norm2, w_in=w_in, w_out=w_out,
                  att_lambda=att_lambda, att_subln=att_subln,
                  rwkv_mu=rwkv_mu, rwkv_w0=rwkv_w0, rwkv_w_up=rwkv_w_up, rwkv_a0=rwkv_a0,
                  rwkv_a_up=rwkv_a_up, rwkv_g_up=rwkv_g_up, rwkv_k_k=rwkv_k_k, rwkv_k_a=rwkv_k_a,
                  rwkv_r_k=rwkv_r_k, rwkv_ln_g=rwkv_ln_g, rwkv_ln_b=rwkv_ln_b,
                  s5_lam_re=s5_lam_re, s5_lam_im=s5_lam_im, s5_log_step=s5_log_step,
                  s5_b_re=s5_b_re, s5_b_im=s5_b_im, s5_c_re=s5_c_re, s5_c_im=s5_c_im,
                  s5_d=s5_d, s5_w_glu=s5_w_glu,
                  moe_router=moe_router, moe_bias=moe_bias, moe_w_gate_up=moe_w_gate_up,
                  moe_w_down=moe_w_down, shared_w_gate_up=shared_w_gate_up,
                  shared_w_down=shared_w_down)
    bp, lp_len, _ = x_prompt.shape
    bs, ls_len, _ = x_sample.shape
    segs = ((bp, lp_len, 0, 0), (bs, ls_len, 1, 1))
    xs = [x_prompt.reshape(bp * lp_len, D), x_sample.reshape(bs * ls_len, D)]
    cpad = jnp.zeros((16, D), F32).at[0].set(c_ctx).at[1:1 + bs].set(c)
    new_k, new_v, new_r, new_s = [], [], [], []
    n_slots = ((bp * lp_len + bs * ls_len) * TOPK // MOE_BM + NE) * MOE_BM
    x_sorted = jnp.zeros((n_slots, D // 2), U32)
    for l in range(DEPTH):
        lp = {name: arr[l] for name, arr in params.items() if name not in _WHOLE}
        whole = {name: params[name] for name in _WHOLE}
        lam_init = 0.8 - 0.6 * math.exp(-0.3 * l)
        ctxs = (None, (cache_attn_k[:, l], cache_attn_v[:, l], state_rwkv[:, l], state_s5[:, l]))
        xs, caches, x_sorted = _layer(xs, segs, cpad, lp, whole, l, lam_init, ctxs, final_norm,
                                      l == DEPTH - 1, x_sorted)
        ck, cv, cr, cs = caches[0]
        new_k.append(ck)
        new_v.append(cv)
        new_r.append(cr)
        new_s.append(cs)
    return (xs[0].reshape(bp, lp_len, D), xs[1].reshape(bs, ls_len, D),
            jnp.stack(new_k, axis=1), jnp.stack(new_v, axis=1),
            jnp.stack(new_r, axis=1), jnp.stack(new_s, axis=1))


_WHOLE = ('w_mod', 'moe_w_gate_up', 'moe_w_down')


def _layer(xs, segs, cpad, lp, whole, layer, lam_init, ctxs, final_norm, final, x_sorted):
    hh = jnp.arange(RW) // QK
    segmat = (hh[:, None] == hh[None, :]).astype(BF16)
    perm = jnp.concatenate([jnp.arange(0, 3072), jnp.arange(3072 + RW_PROJ, PROJ_W),
                            jnp.arange(3072, 3072 + RW_PROJ)])
    mods6 = _modulation(cpad, whole['w_mod'], lp['b_mod'], layer).reshape(16 * 6, 1, D)
    w_in_bf = lp['w_in'][:, perm].astype(BF16)
    w_out_bf = lp['w_out'].astype(BF16)
    wb, wc, ab = _s5_params(lp)
    rt = lp['moe_router'].T
    rt_hi = rt.astype(BF16)
    router_t2 = jnp.stack([rt_hi, (rt - rt_hi.astype(F32)).astype(BF16)], axis=0)
    x1s, h2ps, tws, tis, caches = [], [], [], [], []
    for si, seg in enumerate(segs):
        B, L = seg[0], seg[1]
        proj = _in_proj(xs[si], lp['norm1'], mods6, w_in_bf, seg)
        proj3 = proj.reshape(L, B, PROJ_STRIDE)
        if ctxs[si] is None:
            o_att = _attention(proj, lp['att_lambda'], lp['att_subln'], lam_init, seg)
            s0_rwkv = None
            x0 = jnp.zeros((2, 2, B, S5S), F32)
        else:
            ck, cv, s0_rwkv, s0_s5 = ctxs[si]
            ctx = (ck.reshape(B, -1, ATT_W), cv.reshape(B, -1, ATT_W))
            o_att = _attention(proj, lp['att_lambda'], lp['att_subln'], lam_init, seg, ctx,
                               _rope_tables(L))
            x0 = s0_s5.astype(F32).reshape(B, 2, 2, S5S).transpose(1, 2, 0, 3)
        o_sum, shared, misc, s_fin = _rwkv_mix(proj, lp, segmat, seg, s0_rwkv)
        y_s5, xf = _s5_scan(proj3, wb, wc, ab, x0, seg)
        o_cat = _mix_post(o_att, o_sum, shared, misc, y_s5, proj, segmat, lp, seg)
        x1, h2p, tw, ti = _out_proj(o_cat, xs[si], w_out_bf, mods6, lp['norm2'],
                                    router_t2, lp['moe_bias'], seg)
        x1s.append(x1)
        h2ps.append(h2p)
        tws.append(tw)
        tis.append(ti)
        caches.append((proj3[:, :, COL_K:COL_K + ATT_W].transpose(1, 0, 2).reshape(B, L, HEADS, 2, QK),
                       proj3[:, :, COL_V:COL_V + ATT_W].transpose(1, 0, 2).reshape(B, L, HEADS, 2 * QK),
                       s_fin,
                       xf.transpose(2, 0, 1, 3).reshape(B, 2, 2, S5G, S5N)))
    h2p_all = jnp.concatenate(h2ps, axis=0)
    top_w = jnp.concatenate(tws, axis=1).T
    slot_of, block_e, n_used = _routing_tables(jnp.concatenate(tis, axis=1).T)
    x_sorted = _moe_dispatch(slot_of, h2p_all, x_sorted)
    y_slots = _moe_experts(block_e, n_used, x_sorted, whole['moe_w_gate_up'], whole['moe_w_down'], layer)
    w_sgu_bf = lp['shared_w_gate_up'].astype(BF16)
    w_sdn_bf = lp['shared_w_down'].astype(BF16)
    outs = []
    row0 = 0
    for si, seg in enumerate(segs):
        outs.append(_combine(slot_of, top_w, y_slots, h2p_all, x1s[si], mods6, w_sgu_bf, w_sdn_bf,
                             final_norm, seg, row0, final))
        row0 += seg[0] * seg[1]
    return outs, caches, x_sorted
```
